```python
import math
import jax, jax.numpy as jnp
from jax import lax
import numpy as np

D_MODEL = 2048
BATCH = 8
SEQ = 2048
DEPTH = 1

HEAD_DIM = 64
BLOCK = 128
SWA_WINDOW = 128
SWA_Q_HEADS = (D_MODEL // 2) // HEAD_DIM
SWA_KV_HEADS = max(1, SWA_Q_HEADS // 8)
SWA_GROUP = SWA_Q_HEADS // SWA_KV_HEADS
DIFF_V_DIM = 2 * HEAD_DIM
DIFF_HEADS = (D_MODEL // 2) // DIFF_V_DIM
D_FF = -(-(8 * D_MODEL) // (3 * 256)) * 256
PLE_DIM = 256
EPS = 1e-6

W_SWA_Q = SWA_Q_HEADS * HEAD_DIM
W_SWA_KV = SWA_KV_HEADS * HEAD_DIM
W_DIFF_QK = DIFF_HEADS * 2 * HEAD_DIM
W_DIFF_V = DIFF_HEADS * DIFF_V_DIM
IN_WIDTH = W_SWA_Q + 2 * W_SWA_KV + 2 * W_DIFF_QK + W_DIFF_V
SPLITS = tuple(int(c) for c in np.cumsum([W_SWA_Q, W_SWA_KV, W_SWA_KV, W_DIFF_QK, W_DIFF_QK]))
MIX_WIDTH = W_SWA_Q + W_DIFF_V

kernel_name = "hybrid_swa_sink_diff_attn_block"


def rms_norm(x, g):
    x32 = x.astype(jnp.float32)
    y = x32 * lax.rsqrt(jnp.mean(x32 * x32, axis=-1, keepdims=True) + EPS)
    return (y * g.astype(jnp.float32)).astype(x.dtype)


def alibi_slopes(n):
    return jnp.asarray([2.0 ** (-8.0 * (h + 1) / n) for h in range(n)], dtype=jnp.float32)


def swa_sink_attention(q, k, v, sinks, slopes):
    b, s, _, d = q.shape
    nb = s // BLOCK
    qb = q.reshape(b, nb, BLOCK, SWA_KV_HEADS, SWA_GROUP, d)

    def window(t):
        tb = t.reshape(b, nb, BLOCK, SWA_KV_HEADS, d)
        prev = jnp.pad(tb, ((0, 0), (1, 0), (0, 0), (0, 0), (0, 0)))[:, :-1]
        return jnp.concatenate([prev, tb], axis=2)

    kw, vw = window(k), window(v)
    scores = jnp.einsum('bnqhgd,bnkhd->bnhgqk', qb, kw,
                        preferred_element_type=jnp.float32) * (1.0 / math.sqrt(d))
    qi = jnp.arange(BLOCK)[:, None]
    kj = jnp.arange(2 * BLOCK)[None, :]
    dist = qi - kj + BLOCK
    key_pos = jnp.arange(nb)[:, None] * BLOCK - BLOCK + jnp.arange(2 * BLOCK)[None, :]
    valid = ((dist >= 0) & (dist < SWA_WINDOW))[None] & (key_pos >= 0)[:, None, :]
    bias = -slopes.reshape(SWA_KV_HEADS, SWA_GROUP)[:, :, None, None] * dist.astype(jnp.float32)
    logits = jnp.where(valid[None, :, None, None], scores + bias[None, None], -jnp.inf)
    sink = sinks.astype(jnp.float32).reshape(SWA_KV_HEADS, SWA_GROUP)[None, None, :, :, None, None]
    m = jnp.maximum(jnp.max(logits, axis=-1, keepdims=True), sink)
    e = jnp.exp(logits - m)
    probs = e / (jnp.sum(e, axis=-1, keepdims=True) + jnp.exp(sink - m))
    out = jnp.einsum('bnhgqk,bnkhd->bnqhgd', probs, vw.astype(jnp.float32))
    return out.reshape(b, s, SWA_Q_HEADS * d).astype(q.dtype)


def diff_attention(q, k, v, lam, slopes):
    b, s = q.shape[:2]
    nb = s // BLOCK
    qb = jnp.moveaxis(q.reshape(b, nb, BLOCK, DIFF_HEADS, 2, HEAD_DIM), 1, 0)
    key_pos = jnp.arange(s)
    scale = 1.0 / math.sqrt(HEAD_DIM)
    v32 = v.astype(jnp.float32)

    def one_block(args):
        qblk, n = args
        q_pos = n * BLOCK + jnp.arange(BLOCK)
        dist = (q_pos[:, None] - key_pos[None, :]).astype(jnp.float32)
        sc = jnp.einsum('bqhcd,bkhcd->bhcqk', qblk, k,
                        preferred_element_type=jnp.float32) * scale
        sc = jnp.where(dist >= 0, sc - slopes[:, None, None, None] * dist, -jnp.inf)
        pr = jax.nn.softmax(sc, axis=-1)
        attn = pr[:, :, 0] - lam * pr[:, :, 1]
        return jnp.einsum('bhqk,bkhe->bqhe', attn, v32)

    out = lax.map(one_block, (qb, jnp.arange(nb)))
    return jnp.moveaxis(out, 0, 1).reshape(b, s, DIFF_HEADS, DIFF_V_DIM)


def setup_inputs(seed: int = 0) -> dict:
    key = jax.random.key(seed)
    ks = jax.random.split(key, 24)
    f32 = jnp.float32

    def w(k, shape, fan_in):
        return jax.random.normal(k, shape, f32) * (fan_in ** -0.5)

    def gain(k, shape):
        return 1.0 + 0.01 * jax.random.normal(k, shape, f32)

    return {
        "x": jax.random.normal(ks[0], (BATCH, SEQ, D_MODEL), f32),
        "p": jax.random.normal(ks[1], (DEPTH, BATCH, SEQ, PLE_DIM), f32),
        "g_attn": gain(ks[2], (DEPTH, D_MODEL)),
        "w_in": w(ks[3], (DEPTH, D_MODEL, IN_WIDTH), D_MODEL),
        "qn_swa": gain(ks[4], (DEPTH, HEAD_DIM)),
        "kn_swa": gain(ks[5], (DEPTH, HEAD_DIM)),
        "sinks": jax.random.normal(ks[6], (DEPTH, SWA_Q_HEADS), f32),
        "qn_diff": gain(ks[7], (DEPTH, HEAD_DIM)),
        "kn_diff": gain(ks[8], (DEPTH, HEAD_DIM)),
        "lambda_q1": 0.1 * jax.random.normal(ks[9], (DEPTH, HEAD_DIM), f32),
        "lambda_k1": 0.1 * jax.random.normal(ks[10], (DEPTH, HEAD_DIM), f32),
        "lambda_q2": 0.1 * jax.random.normal(ks[11], (DEPTH, HEAD_DIM), f32),
        "lambda_k2": 0.1 * jax.random.normal(ks[12], (DEPTH, HEAD_DIM), f32),
        "g_sub": gain(ks[13], (DEPTH, DIFF_V_DIM)),
        "w_out": w(ks[14], (DEPTH, MIX_WIDTH, D_MODEL), MIX_WIDTH),
        "g_ffn": gain(ks[15], (DEPTH, D_MODEL)),
        "w_gate": w(ks[16], (DEPTH, D_MODEL, D_FF), D_MODEL),
        "w_up": w(ks[17], (DEPTH, D_MODEL, D_FF), D_MODEL),
        "w_down": w(ks[18], (DEPTH, D_FF, D_MODEL), D_FF),
        "g_ple": gain(ks[19], (DEPTH, D_MODEL)),
        "w_ple_gate": w(ks[20], (DEPTH, D_MODEL, D_MODEL), D_MODEL),
        "w_ple_proj": w(ks[21], (DEPTH, PLE_DIM, D_MODEL), PLE_DIM),
        "g_ple_out": gain(ks[22], (DEPTH, D_MODEL)),
    }


def reference(x, p, g_attn, w_in, qn_swa, kn_swa, sinks, qn_diff, kn_diff,
              lambda_q1, lambda_k1, lambda_q2, lambda_k2, g_sub, w_out,
              g_ffn, w_gate, w_up, w_down, g_ple, w_ple_gate, w_ple_proj, g_ple_out):
    b, s, _ = x.shape
    slopes_swa = alibi_slopes(SWA_Q_HEADS)
    slopes_diff = alibi_slopes(DIFF_HEADS)
    h = x
    for i in range(DEPTH):
        lam_init = 0.8 - 0.6 * math.exp(-0.3 * i)
        u = rms_norm(h, g_attn[i])
        z = u @ w_in[i]
        qa, ka, va, qd, kd, vd = jnp.split(z, SPLITS, axis=-1)
        qa = rms_norm(qa.reshape(b, s, SWA_Q_HEADS, HEAD_DIM), qn_swa[i])
        ka = rms_norm(ka.reshape(b, s, SWA_KV_HEADS, HEAD_DIM), kn_swa[i])
        va = va.reshape(b, s, SWA_KV_HEADS, HEAD_DIM)
        ya = swa_sink_attention(qa, ka, va, sinks[i], slopes_swa)

        qd = rms_norm(qd.reshape(b, s, DIFF_HEADS, 2, HEAD_DIM), qn_diff[i])
        kd = rms_norm(kd.reshape(b, s, DIFF_HEADS, 2, HEAD_DIM), kn_diff[i])
        vd = vd.reshape(b, s, DIFF_HEADS, DIFF_V_DIM)
        lam = (jnp.exp(jnp.sum(lambda_q1[i].astype(jnp.float32) * lambda_k1[i].astype(jnp.float32)))
               - jnp.exp(jnp.sum(lambda_q2[i].astype(jnp.float32) * lambda_k2[i].astype(jnp.float32)))
               + lam_init)
        od = diff_attention(qd, kd, vd, lam, slopes_diff)
        yd = (rms_norm(od, g_sub[i]) * (1.0 - lam_init)).reshape(b, s, W_DIFF_V).astype(x.dtype)

        h = h + jnp.concatenate([ya, yd], axis=-1) @ w_out[i]
        u2 = rms_norm(h, g_ffn[i])
        h = h + (jax.nn.silu(u2 @ w_gate[i]) * (u2 @ w_up[i])) @ w_down[i]
        gate = jax.nn.sigmoid(rms_norm(h, g_ple[i]) @ w_ple_gate[i])
        h = h + gate * rms_norm(p[i] @ w_ple_proj[i], g_ple_out[i])
    return h
```

```python
import functools
import math

import jax
import jax.numpy as jnp
import numpy as np
from jax import lax
from jax.experimental import pallas as pl
from jax.experimental.pallas import tpu as pltpu

F32 = jnp.float32
BF16 = jnp.bfloat16

HEAD_DIM = 64
SWA_BLOCK = 128
SWA_KV_HEADS = 2
SWA_GROUP = 8
DIFF_V_DIM = 2 * HEAD_DIM
EPS = 1e-6
NEG_INF = float("-inf")

TOKEN_TILE = 256
VMEM_LIMIT_BYTES = 56 * 1024 * 1024

_NT = (((1,), (1,)), ((), ()))


def _resident(shape, index_map):
    return pl.BlockSpec(shape, index_map, pipeline_mode=pl.Buffered(1))


def _rms_rows(x, gain):
    ms = jnp.mean(x * x, axis=-1, keepdims=True)
    return x * lax.rsqrt(ms + EPS) * gain


def _inproj_kernel(x_ref, g_ref, wfm_ref, wtm_ref, gqa_ref, gqd_ref, gka_ref, gkd_ref, bd_ref,
                   qa_ref, va_ref, qd_ref, vd_ref, ka_ref, kd_ref, *, n_qa, n_va, n_qd, n_vd, n_ka, n_kd):
    u = _rms_rows(x_ref[...], g_ref[...]).astype(BF16)
    tm = u.shape[0]

    def fm(row0, nrows):
        return lax.dot_general(wfm_ref[row0:row0 + nrows, :], u, _NT, preferred_element_type=F32)

    def headnorm_fm(z, gain, store):
        for h in range(z.shape[0] // HEAD_DIM):
            zh = z[h * HEAD_DIM:(h + 1) * HEAD_DIM, :]
            ms = jnp.mean(zh * zh, axis=0, keepdims=True)
            store(h, (zh * lax.rsqrt(ms + EPS) * gain).astype(BF16))

    def store_qa(h, val):
        for t in range(tm // SWA_BLOCK):
            qa_ref[0, t, h * HEAD_DIM:(h + 1) * HEAD_DIM, :] = val[:, t * SWA_BLOCK:(t + 1) * SWA_BLOCK]

    def store_qd(h, val):
        qd_ref[0, 0, h * HEAD_DIM:(h + 1) * HEAD_DIM, :] = val

    r = 0
    headnorm_fm(fm(r, n_qa), gqa_ref[...], store_qa)
    r += n_qa
    zva = fm(r, n_va).astype(BF16)
    for t in range(tm // SWA_BLOCK):
        va_ref[0, t] = zva[:, t * SWA_BLOCK:(t + 1) * SWA_BLOCK]
    r += n_va
    headnorm_fm(fm(r, n_qd), gqd_ref[...], store_qd)
    r += n_qd
    vd_ref[0, 0] = fm(r, n_vd).astype(BF16)

    def headnorm_tm(col0, ncols, gain):
        z = jnp.dot(u, wtm_ref[:, col0:col0 + ncols], preferred_element_type=F32)
        ssq = jnp.dot((z * z).astype(BF16), bd_ref[0:ncols, 0:ncols], preferred_element_type=F32)
        return (z * lax.rsqrt(ssq * (1.0 / HEAD_DIM) + EPS) * gain).astype(BF16)

    ka_ref[0] = headnorm_tm(0, n_ka, gka_ref[...])
    cw = bd_ref.shape[0]
    for c in range(n_kd // cw):
        kd_ref[0, :, c * cw:(c + 1) * cw] = headnorm_tm(n_ka + c * cw, cw, gkd_ref[...])


def _inproj(x, g_attn, wfm, wtm, qn_swa, kn_swa, qn_diff, kn_diff, dims):
    b, s, d = x.shape
    n_qa, n_ka, n_va, n_qd, n_kd, n_vd = dims
    tm = TOKEN_TILE
    nt = s // tm
    scale = 1.0 / math.sqrt(HEAD_DIM)
    gqa = jnp.broadcast_to((qn_swa * scale)[:, None], (HEAD_DIM, tm)).astype(F32)
    gqd = jnp.broadcast_to((qn_diff * scale)[:, None], (HEAD_DIM, tm)).astype(F32)
    gka = jnp.tile(kn_swa, n_ka // HEAD_DIM)[None, :].astype(F32)
    gkd = jnp.tile(kn_diff, 256 // HEAD_DIM)[None, :].astype(F32)
    idx = np.arange(256) // HEAD_DIM
    bd = jnp.asarray(idx[:, None] == idx[None, :], dtype=BF16)
    const = lambda shape: _resident(shape, lambda bi, ti: (0,) * len(shape))
    kern = functools.partial(_inproj_kernel, n_qa=n_qa, n_va=n_va, n_qd=n_qd, n_vd=n_vd, n_ka=n_ka, n_kd=n_kd)
    return pl.pallas_call(
        kern,
        grid=(b, nt),
        in_specs=[
            pl.BlockSpec((None, tm, d), lambda bi, ti: (bi, ti, 0)),
            const((1, d)), const(wfm.shape), const(wtm.shape),
            const((HEAD_DIM, tm)), const((HEAD_DIM, tm)), const((1, n_ka)), const((1, 256)), const((256, 256)),
        ],
        out_specs=[
            pl.BlockSpec((1, tm // SWA_BLOCK, n_qa, SWA_BLOCK), lambda bi, ti: (bi, ti, 0, 0)),
            pl.BlockSpec((1, tm // SWA_BLOCK, n_va, SWA_BLOCK), lambda bi, ti: (bi, ti, 0, 0)),
            pl.BlockSpec((1, 1, n_qd, tm), lambda bi, ti: (bi, ti, 0, 0)),
            pl.BlockSpec((1, 1, n_vd, tm), lambda bi, ti: (bi, ti, 0, 0)),
            pl.BlockSpec((1, tm, n_ka), lambda bi, ti: (bi, ti, 0)),
            pl.BlockSpec((1, tm, n_kd), lambda bi, ti: (bi, ti, 0)),
        ],
        out_shape=[
            jax.ShapeDtypeStruct((b, s // SWA_BLOCK, n_qa, SWA_BLOCK), BF16),
            jax.ShapeDtypeStruct((b, s // SWA_BLOCK, n_va, SWA_BLOCK), BF16),
            jax.ShapeDtypeStruct((b, nt, n_qd, tm), BF16),
            jax.ShapeDtypeStruct((b, nt, n_vd, tm), BF16),
            jax.ShapeDtypeStruct((b, s, n_ka), BF16),
            jax.ShapeDtypeStruct((b, s, n_kd), BF16),
        ],
        compiler_params=pltpu.CompilerParams(
            dimension_semantics=("parallel", "parallel"), vmem_limit_bytes=VMEM_LIMIT_BYTES),
        name="inproj",
    )(x, g_attn[None, :], wfm, wtm, gqa, gqd, gka, gkd, bd)


def _swa_kernel(q_ref, k_ref, v_ref, bias_ref, sink_ref, o_ref):
    hk = pl.program_id(1)
    nblk = q_ref.shape[1]
    w = SWA_BLOCK
    sink = sink_ref[0]

    def body(n, carry):
        qblk = q_ref[0, n]
        qg = jnp.concatenate([qblk[g * HEAD_DIM:(g + 1) * HEAD_DIM, :] for g in range(SWA_GROUP)], axis=1)
        zero = jnp.zeros_like(qg)
        qpad = jnp.concatenate([jnp.where(hk == 0, qg, zero), jnp.where(hk == 1, qg, zero)], axis=0)
        first = jnp.maximum(n - 1, 0)
        kwin = k_ref[0, pl.ds(pl.multiple_of(first * w, w), 2 * w), :]
        t = jnp.dot(kwin, qpad, preferred_element_type=F32) + bias_ref[jnp.minimum(n, 1)]
        m = jnp.maximum(jnp.max(t, axis=0, keepdims=True), sink)
        e = jnp.exp(t - m)
        denom = jnp.sum(e, axis=0, keepdims=True) + jnp.exp(sink - m)
        vwin = jnp.concatenate([v_ref[0, first], v_ref[0, first + 1]], axis=1)
        o = jnp.dot(vwin, e.astype(BF16), preferred_element_type=F32) * (1.0 / denom)
        for gp in range(SWA_GROUP // 2):
            pair = jnp.concatenate([o[:, (2 * gp) * w:(2 * gp + 1) * w],
                                    o[:, (2 * gp + 1) * w:(2 * gp + 2) * w]], axis=0)
            o_ref[0, pl.ds(pl.multiple_of(n * w, w), w), gp * 2 * HEAD_DIM:(gp + 1) * 2 * HEAD_DIM] = (
                pair.T.astype(BF16))
        return carry

    lax.fori_loop(0, nblk, body, 0)


def _swa_bias(slopes):
    w = SWA_BLOCK
    kj = np.arange(2 * w)[:, None]
    qi = np.arange(w)[None, :]
    out = np.empty((2, SWA_KV_HEADS, 2 * w, SWA_GROUP * w), np.float32)
    for first, dist in ((0, qi - kj), (1, qi - kj + w)):
        valid = (dist >= 0) & (dist < w)
        for hk in range(SWA_KV_HEADS):
            for g in range(SWA_GROUP):
                sl = slopes[hk * SWA_GROUP + g]
                out[first, hk, :, g * w:(g + 1) * w] = np.where(valid, -sl * dist, -np.inf)
    return out


def _swa_attention(qa_t, ka, va_t, sinks, slopes):
    b, nblk, n_qa, w = qa_t.shape
    s = nblk * w
    gw = SWA_GROUP * w
    bias = jnp.asarray(_swa_bias(slopes))
    sink_cols = jnp.repeat(sinks.astype(F32).reshape(SWA_KV_HEADS, 1, SWA_GROUP), w, axis=-1)
    return pl.pallas_call(
        _swa_kernel,
        grid=(b, SWA_KV_HEADS),
        in_specs=[
            pl.BlockSpec((1, nblk, SWA_GROUP * HEAD_DIM, w), lambda bi, hk: (bi, 0, hk, 0)),
            pl.BlockSpec((1, s, SWA_KV_HEADS * HEAD_DIM), lambda bi, hk: (bi, 0, 0)),
            pl.BlockSpec((1, nblk, HEAD_DIM, w), lambda bi, hk: (bi, 0, hk, 0)),
            pl.BlockSpec((2, None, 2 * w, gw), lambda bi, hk: (0, hk, 0, 0)),
            pl.BlockSpec((1, 1, gw), lambda bi, hk: (hk, 0, 0)),
        ],
        out_specs=pl.BlockSpec((1, s, SWA_GROUP * HEAD_DIM), lambda bi, hk: (bi, 0, hk)),
        out_shape=jax.ShapeDtypeStruct((b, s, n_qa), BF16),
        compiler_params=pltpu.CompilerParams(
            dimension_semantics=("parallel", "parallel"), vmem_limit_bytes=VMEM_LIMIT_BYTES),
        name="swa_attn",
    )(qa_t, ka, va_t, bias, sink_cols)


def _diff_kernel(slope_ref, q_ref, k_ref, v_ref, bias_ref, lam_ref, gsub_ref, o_ref, acc_ref, *, lam_init):
    h = pl.program_id(1)
    iq = pl.program_id(2)
    t_q = q_ref.shape[3]
    q = q_ref[0, 0]
    zero = jnp.zeros((HEAD_DIM, t_q), BF16)
    qb = jnp.concatenate([jnp.concatenate([q[:HEAD_DIM], zero], axis=0),
                          jnp.concatenate([zero, q[HEAD_DIM:]], axis=0)], axis=1)
    slope = slope_ref[h]
    acc_ref[...] = jnp.zeros_like(acc_ref)

    def step(j, m, l, bias):
        kblk = k_ref[0, pl.ds(pl.multiple_of(j * t_q, t_q), t_q), :]
        t = jnp.dot(kblk, qb, preferred_element_type=F32) + bias
        cj = slope * ((j - iq) * t_q).astype(F32)
        mnew = jnp.maximum(m, jnp.max(t, axis=0, keepdims=True) + cj)
        alpha = jnp.exp(m - mnew)
        p = jnp.exp(t + (cj - mnew))
        l = alpha * l + jnp.sum(p, axis=0, keepdims=True)
        acc_ref[...] = alpha * acc_ref[...] + jnp.dot(v_ref[0, j], p.astype(BF16), preferred_element_type=F32)
        return mnew, l

    m0 = jnp.full((1, 2 * t_q), NEG_INF, F32)
    l0 = jnp.zeros((1, 2 * t_q), F32)
    m, l = lax.fori_loop(0, iq, lambda j, c: step(j, c[0], c[1], bias_ref[0, 0]), (m0, l0))
    m, l = step(iq, m, l, bias_ref[0, 1])

    acc = acc_ref[...] * (1.0 / l)
    lp = lam_ref[...]
    lam = (jnp.exp(jnp.sum(lp[0:1] * lp[1:2], axis=-1, keepdims=True))
           - jnp.exp(jnp.sum(lp[2:3] * lp[3:4], axis=-1, keepdims=True)) + lam_init)
    od = acc[:, :t_q] - lam * acc[:, t_q:]
    ms = jnp.mean(od * od, axis=0, keepdims=True)
    y = od * lax.rsqrt(ms + EPS) * gsub_ref[...]
    o_ref[0] = y.T.astype(BF16)


def _diff_bias(slopes, t):
    kj = np.arange(t)[:, None]
    qi = np.arange(2 * t)[None, :] % t
    rel = (kj - qi).astype(np.float32)
    out = np.empty((len(slopes), 2, t, 2 * t), np.float32)
    for h, sl in enumerate(slopes):
        out[h, 0] = sl * rel
        out[h, 1] = np.where(kj <= qi, sl * rel, -np.inf)
    return out


def _diff_attention(qd_t, kd, vd_t, lam_params, g_sub, slopes, lam_init):
    b, nt, n_qd, t = qd_t.shape
    s = nt * t
    nh = n_qd // DIFF_V_DIM
    bias = jnp.asarray(_diff_bias(slopes, t))
    gsub = jnp.broadcast_to((g_sub * (1.0 - lam_init))[:, None], (DIFF_V_DIM, t)).astype(F32)
    kern = functools.partial(_diff_kernel, lam_init=lam_init)
    return pl.pallas_call(
        kern,
        grid=(b, nh, nt),
        in_specs=[
            pl.BlockSpec(memory_space=pltpu.SMEM),
            pl.BlockSpec((1, 1, DIFF_V_DIM, t), lambda bi, h, iq: (bi, iq, h, 0)),
            pl.BlockSpec((1, s, DIFF_V_DIM), lambda bi, h, iq: (bi, 0, h)),
            pl.BlockSpec((1, nt, DIFF_V_DIM, t), lambda bi, h, iq: (bi, 0, h, 0)),
            pl.BlockSpec((1, 2, t, 2 * t), lambda bi, h, iq: (h, 0, 0, 0)),
            pl.BlockSpec((4, HEAD_DIM), lambda bi, h, iq: (0, 0)),
            pl.BlockSpec((DIFF_V_DIM, t), lambda bi, h, iq: (0, 0)),
        ],
        out_specs=pl.BlockSpec((1, t, DIFF_V_DIM), lambda bi, h, iq: (bi, iq, h)),
        out_shape=jax.ShapeDtypeStruct((b, s, n_qd), BF16),
        scratch_shapes=[pltpu.VMEM((DIFF_V_DIM, 2 * t), F32)],
        compiler_params=pltpu.CompilerParams(
            dimension_semantics=("parallel", "parallel", "arbitrary"), vmem_limit_bytes=VMEM_LIMIT_BYTES),
        name="diff_attn",
    )(jnp.asarray(slopes, F32), qd_t, kd, vd_t, bias, lam_params, gsub)


def _outproj_kernel(x_ref, ya_ref, yd_ref, wa_ref, wd_ref, o_ref):
    o_ref[...] = (x_ref[...]
                  + jnp.dot(ya_ref[...], wa_ref[...], preferred_element_type=F32)
                  + jnp.dot(yd_ref[...], wd_ref[...], preferred_element_type=F32))


def _outproj(x2, ya2, yd2, wa, wd, tm=512):
    n, d = x2.shape
    const = lambda shape: _resident(shape, lambda i: (0, 0))
    return pl.pallas_call(
        _outproj_kernel,
        grid=(n // tm,),
        in_specs=[
            pl.BlockSpec((tm, d), lambda i: (i, 0)),
            pl.BlockSpec((tm, ya2.shape[1]), lambda i: (i, 0)),
            pl.BlockSpec((tm, yd2.shape[1]), lambda i: (i, 0)),
            const(wa.shape), const(wd.shape),
        ],
        out_specs=pl.BlockSpec((tm, d), lambda i: (i, 0)),
        out_shape=jax.ShapeDtypeStruct((n, d), F32),
        compiler_params=pltpu.CompilerParams(
            dimension_semantics=("parallel",), vmem_limit_bytes=VMEM_LIMIT_BYTES),
        name="outproj",
    )(x2, ya2, yd2, wa, wd)


def _ffn_kernel(h_ref, g_ref, wg_ref, wu_ref, wd_ref, o_ref, u_ref):
    f = pl.program_id(1)

    @pl.when(f == 0)
    def _():
        h = h_ref[...]
        u_ref[...] = _rms_rows(h, g_ref[...]).astype(BF16)
        o_ref[...] = h

    u = u_ref[...]
    gate = jnp.dot(u, wg_ref[...], preferred_element_type=F32)
    up = jnp.dot(u, wu_ref[...], preferred_element_type=F32)
    act = (gate * (1.0 / (1.0 + jnp.exp(-gate))) * up).astype(BF16)
    o_ref[...] += jnp.dot(act, wd_ref[...], preferred_element_type=F32)


def _ffn(h2, g_ffn, wg, wu, wd, tm=512, tf=512):
    n, d = h2.shape
    dff = wg.shape[1]
    return pl.pallas_call(
        _ffn_kernel,
        grid=(n // tm, dff // tf),
        in_specs=[
            pl.BlockSpec((tm, d), lambda i, f: (i, 0)),
            _resident((1, d), lambda i, f: (0, 0)),
            pl.BlockSpec((d, tf), lambda i, f: (0, f)),
            pl.BlockSpec((d, tf), lambda i, f: (0, f)),
            pl.BlockSpec((tf, d), lambda i, f: (f, 0)),
        ],
        out_specs=pl.BlockSpec((tm, d), lambda i, f: (i, 0)),
        out_shape=jax.ShapeDtypeStruct((n, d), F32),
        scratch_shapes=[pltpu.VMEM((tm, d), BF16)],
        compiler_params=pltpu.CompilerParams(
            dimension_semantics=("parallel", "arbitrary"), vmem_limit_bytes=VMEM_LIMIT_BYTES),
        name="ffn",
    )(h2, g_ffn[None, :], wg, wu, wd)


def _ple_kernel(h_ref, p_ref, g_ref, wg_ref, wp_ref, go_ref, o_ref):
    h = h_ref[...]
    u = _rms_rows(h, g_ref[...]).astype(BF16)
    z = jnp.dot(u, wg_ref[...], preferred_element_type=F32)
    gate = 1.0 / (1.0 + jnp.exp(-z))
    pp = jnp.dot(p_ref[...].astype(BF16), wp_ref[...], preferred_element_type=F32)
    o_ref[...] = h + gate * _rms_rows(pp, go_ref[...])


def _ple(h2, p2, g_ple, wg, wp, g_out, tm=512):
    n, d = h2.shape
    const = lambda shape: _resident(shape, lambda i: (0, 0))
    return pl.pallas_call(
        _ple_kernel,
        grid=(n // tm,),
        in_specs=[
            pl.BlockSpec((tm, d), lambda i: (i, 0)),
            pl.BlockSpec((tm, p2.shape[1]), lambda i: (i, 0)),
            const((1, d)), const(wg.shape), const(wp.shape), const((1, d)),
        ],
        out_specs=pl.BlockSpec((tm, d), lambda i: (i, 0)),
        out_shape=jax.ShapeDtypeStruct((n, d), F32),
        compiler_params=pltpu.CompilerParams(
            dimension_semantics=("parallel",), vmem_limit_bytes=VMEM_LIMIT_BYTES),
        name="ple",
    )(h2, p2, g_ple[None, :], wg, wp, g_out[None, :])


def _alibi_slopes(n):
    return [2.0 ** (-8.0 * (h + 1) / n) for h in range(n)]


def kernel(x, p, g_attn, w_in, qn_swa, kn_swa, sinks, qn_diff, kn_diff, lambda_q1, lambda_k1, lambda_q2,
           lambda_k2, g_sub, w_out, g_ffn, w_gate, w_up, w_down, g_ple, w_ple_gate, w_ple_proj, g_ple_out):
    b, s, d = x.shape
    depth = p.shape[0]
    n_qa = d // 2
    n_ka = n_va = SWA_KV_HEADS * HEAD_DIM
    n_qd = n_kd = n_vd = d // 2
    diff_heads = n_vd // DIFF_V_DIM
    swa_heads = n_qa // HEAD_DIM
    assert swa_heads == SWA_KV_HEADS * SWA_GROUP and s % TOKEN_TILE == 0
    c = np.cumsum([0, n_qa, n_ka, n_va, n_qd, n_kd, n_vd])
    h = x
    for i in range(depth):
        lam_init = 0.8 - 0.6 * math.exp(-0.3 * i)
        w = w_in[i]
        col = lambda k: w[:, c[k]:c[k + 1]]
        wfm = jnp.concatenate([col(0), col(2), col(3), col(5)], axis=1).T.astype(BF16)
        wtm = jnp.concatenate([col(1), col(4)], axis=1).astype(BF16)
        qa_t, va_t, qd_t, vd_t, ka, kd = _inproj(
            h, g_attn[i], wfm, wtm, qn_swa[i], kn_swa[i], qn_diff[i], kn_diff[i],
            (n_qa, n_ka, n_va, n_qd, n_kd, n_vd))
        ya = _swa_attention(qa_t, ka, va_t, sinks[i], _alibi_slopes(swa_heads))
        lam_params = jnp.stack([lambda_q1[i], lambda_k1[i], lambda_q2[i], lambda_k2[i]]).astype(F32)
        yd = _diff_attention(qd_t, kd, vd_t, lam_params, g_sub[i], _alibi_slopes(diff_heads), lam_init)
        wo = w_out[i].astype(BF16)
        h2 = _outproj(h.reshape(b * s, d), ya.reshape(b * s, n_qa), yd.reshape(b * s, n_vd),
                      wo[:n_qa], wo[n_qa:])
        h2 = _ffn(h2, g_ffn[i], w_gate[i].astype(BF16), w_up[i].astype(BF16), w_down[i].astype(BF16))
        h2 = _ple(h2, p[i].reshape(b * s, -1), g_ple[i], w_ple_gate[i].astype(BF16),
                  w_ple_proj[i].astype(BF16), g_ple_out[i])
        h = h2.reshape(b, s, d)
    return h
```

```python
import functools
import math

import jax
import jax.numpy as jnp
import numpy as np
from jax import lax
from jax.experimental import pallas as pl
from jax.experimental.pallas import tpu as pltpu

F32 = jnp.float32
BF16 = jnp.bfloat16

HEAD_DIM = 64
SWA_BLOCK = 128
SWA_KV_HEADS = 2
SWA_GROUP = 8
DIFF_V_DIM = 2 * HEAD_DIM
EPS = 1e-6
NEG_INF = float("-inf")
LOG2E = math.log2(math.e)
MAX_UNSHIFTED_LOGIT = 60.0

TOKEN_TILE = 256
DIFF_HEADS_PER_STEP = 4
SWA_BLOCKS_PER_ITER = 4
VMEM_LIMIT_BYTES = 56 * 1024 * 1024

_NT = (((1,), (1,)), ((), ()))


def _resident(shape, index_map):
    return pl.BlockSpec(shape, index_map, pipeline_mode=pl.Buffered(1))


def _rms_rows(x, gain):
    ms = jnp.mean(x * x, axis=-1, keepdims=True)
    return x * lax.rsqrt(ms + EPS) * gain


def _inproj_kernel(x_ref, g_ref, wfm_ref, wtm_ref, gqa_ref, gqd_ref, gka_ref, gkd_ref, bd_ref,
                   qa_ref, va_ref, qd_ref, vd_ref, ka_ref, kd_ref, *, n_qa, n_va, n_qd, n_vd, n_ka, n_kd):
    u = _rms_rows(x_ref[...], g_ref[...]).astype(BF16)
    tm = u.shape[0]

    def fm(row0, nrows):
        return lax.dot_general(wfm_ref[row0:row0 + nrows, :], u, _NT, preferred_element_type=F32)

    def headnorm_fm(z, gain, store):
        for h in range(z.shape[0] // HEAD_DIM):
            zh = z[h * HEAD_DIM:(h + 1) * HEAD_DIM, :]
            ms = jnp.mean(zh * zh, axis=0, keepdims=True)
            store(h, (zh * lax.rsqrt(ms + EPS) * gain).astype(BF16))

    def store_qa(h, val):
        for t in range(tm // SWA_BLOCK):
            qa_ref[0, t, h * HEAD_DIM:(h + 1) * HEAD_DIM, :] = val[:, t * SWA_BLOCK:(t + 1) * SWA_BLOCK]

    def store_qd(h, val):
        qd_ref[0, 0, h * HEAD_DIM:(h + 1) * HEAD_DIM, :] = val

    r = 0
    headnorm_fm(fm(r, n_qa), gqa_ref[...], store_qa)
    r += n_qa
    zva = fm(r, n_va).astype(BF16)
    for t in range(tm // SWA_BLOCK):
        va_ref[0, t] = zva[:, t * SWA_BLOCK:(t + 1) * SWA_BLOCK]
    r += n_va
    headnorm_fm(fm(r, n_qd), gqd_ref[...], store_qd)
    r += n_qd
    vd_ref[0, 0] = fm(r, n_vd).astype(BF16)

    def headnorm_tm(col0, ncols, gain):
        z = jnp.dot(u, wtm_ref[:, col0:col0 + ncols], preferred_element_type=F32)
        ssq = jnp.dot((z * z).astype(BF16), bd_ref[0:ncols, 0:ncols], preferred_element_type=F32)
        return (z * lax.rsqrt(ssq * (1.0 / HEAD_DIM) + EPS) * gain).astype(BF16)

    ka_ref[0] = headnorm_tm(0, n_ka, gka_ref[...])
    cw = bd_ref.shape[0]
    for c in range(n_kd // cw):
        kd_ref[0, :, c * cw:(c + 1) * cw] = headnorm_tm(n_ka + c * cw, cw, gkd_ref[...])


def _inproj(x, g_attn, wfm, wtm, qn_swa, kn_swa, qn_diff, kn_diff, dims):
    b, s, d = x.shape
    n_qa, n_ka, n_va, n_qd, n_kd, n_vd = dims
    tm = TOKEN_TILE
    nt = s // tm
    scale = LOG2E / math.sqrt(HEAD_DIM)
    gqa = jnp.broadcast_to((qn_swa * scale)[:, None], (HEAD_DIM, tm)).astype(F32)
    gqd = jnp.broadcast_to((qn_diff * scale)[:, None], (HEAD_DIM, tm)).astype(F32)
    gka = jnp.tile(kn_swa, n_ka // HEAD_DIM)[None, :].astype(F32)
    gkd = jnp.tile(kn_diff, 256 // HEAD_DIM)[None, :].astype(F32)
    idx = np.arange(256) // HEAD_DIM
    bd = jnp.asarray(idx[:, None] == idx[None, :], dtype=BF16)
    const = lambda shape: _resident(shape, lambda bi, ti: (0,) * len(shape))
    kern = functools.partial(_inproj_kernel, n_qa=n_qa, n_va=n_va, n_qd=n_qd, n_vd=n_vd, n_ka=n_ka, n_kd=n_kd)
    return pl.pallas_call(
        kern,
        grid=(b, nt),
        in_specs=[
            pl.BlockSpec((None, tm, d), lambda bi, ti: (bi, ti, 0)),
            const((1, d)), const(wfm.shape), const(wtm.shape),
            const((HEAD_DIM, tm)), const((HEAD_DIM, tm)), const((1, n_ka)), const((1, 256)), const((256, 256)),
        ],
        out_specs=[
            pl.BlockSpec((1, tm // SWA_BLOCK, n_qa, SWA_BLOCK), lambda bi, ti: (bi, ti, 0, 0)),
            pl.BlockSpec((1, tm // SWA_BLOCK, n_va, SWA_BLOCK), lambda bi, ti: (bi, ti, 0, 0)),
            pl.BlockSpec((1, 1, n_qd, tm), lambda bi, ti: (bi, ti, 0, 0)),
            pl.BlockSpec((1, 1, n_vd, tm), lambda bi, ti: (bi, ti, 0, 0)),
            pl.BlockSpec((1, tm, n_ka), lambda bi, ti: (bi, ti, 0)),
            pl.BlockSpec((1, tm, n_kd), lambda bi, ti: (bi, ti, 0)),
        ],
        out_shape=[
            jax.ShapeDtypeStruct((b, s // SWA_BLOCK, n_qa, SWA_BLOCK), BF16),
            jax.ShapeDtypeStruct((b, s // SWA_BLOCK, n_va, SWA_BLOCK), BF16),
            jax.ShapeDtypeStruct((b, nt, n_qd, tm), BF16),
            jax.ShapeDtypeStruct((b, nt, n_vd, tm), BF16),
            jax.ShapeDtypeStruct((b, s, n_ka), BF16),
            jax.ShapeDtypeStruct((b, s, n_kd), BF16),
        ],
        compiler_params=pltpu.CompilerParams(
            dimension_semantics=("parallel", "parallel"), vmem_limit_bytes=VMEM_LIMIT_BYTES),
        name="inproj",
    )(x, g_attn[None, :], wfm, wtm, gqa, gqd, gka, gkd, bd)


def _swa_kernel(bounded_ref, q_ref, k_ref, v_ref, bias_ref, sink_ref, o_ref):
    hk = pl.program_id(1)
    nblk = q_ref.shape[1]
    w = SWA_BLOCK
    sink = sink_ref[0]

    def body(n, carry, *, bounded):
        qblk = q_ref[0, n]
        qg = jnp.concatenate([qblk[g * HEAD_DIM:(g + 1) * HEAD_DIM, :] for g in range(SWA_GROUP)], axis=1)
        zero = jnp.zeros_like(qg)
        qpad = jnp.concatenate([jnp.where(hk == 0, qg, zero), jnp.where(hk == 1, qg, zero)], axis=0)
        first = jnp.maximum(n - 1, 0)
        kwin = k_ref[0, pl.ds(pl.multiple_of(first * w, w), 2 * w), :]
        t = jnp.dot(kwin, qpad, preferred_element_type=F32) + bias_ref[jnp.minimum(n, 1)]
        if bounded:
            e = jnp.exp2(t)
            denom = jnp.sum(e, axis=0, keepdims=True) + jnp.exp2(sink)
        else:
            m = jnp.maximum(jnp.max(t, axis=0, keepdims=True), sink)
            e = jnp.exp2(t - m)
            denom = jnp.sum(e, axis=0, keepdims=True) + jnp.exp2(sink - m)
        vwin = jnp.concatenate([v_ref[0, first], v_ref[0, first + 1]], axis=1)
        o = jnp.dot(vwin, e.astype(BF16), preferred_element_type=F32) * (1.0 / denom)
        for gp in range(SWA_GROUP // 2):
            pair = jnp.concatenate([o[:, (2 * gp) * w:(2 * gp + 1) * w],
                                    o[:, (2 * gp + 1) * w:(2 * gp + 2) * w]], axis=0)
            o_ref[0, pl.ds(pl.multiple_of(n * w, w), w), gp * 2 * HEAD_DIM:(gp + 1) * 2 * HEAD_DIM] = (
                pair.T.astype(BF16))
        return carry

    @pl.when(bounded_ref[0] != 0)
    def _():
        lax.fori_loop(0, nblk, functools.partial(body, bounded=True), 0, unroll=SWA_BLOCKS_PER_ITER)

    @pl.when(bounded_ref[0] == 0)
    def _():
        lax.fori_loop(0, nblk, functools.partial(body, bounded=False), 0)


def _swa_bias(slopes):
    w = SWA_BLOCK
    kj = np.arange(2 * w)[:, None]
    qi = np.arange(w)[None, :]
    out = np.empty((2, SWA_KV_HEADS, 2 * w, SWA_GROUP * w), np.float32)
    for first, dist in ((0, qi - kj), (1, qi - kj + w)):
        valid = (dist >= 0) & (dist < w)
        for hk in range(SWA_KV_HEADS):
            for g in range(SWA_GROUP):
                sl = slopes[hk * SWA_GROUP + g]
                out[first, hk, :, g * w:(g + 1) * w] = np.where(valid, -sl * dist, -np.inf)
    return out


def _swa_attention(qa_t, ka, va_t, sinks, slopes, logit_bound):
    b, nblk, n_qa, w = qa_t.shape
    s = nblk * w
    gw = SWA_GROUP * w
    bias = jnp.asarray(_swa_bias(slopes) * LOG2E)
    sinks = sinks.astype(F32)
    sink_cols = jnp.repeat((sinks * LOG2E).reshape(SWA_KV_HEADS, 1, SWA_GROUP), w, axis=-1)
    bounded = (jnp.maximum(logit_bound, jnp.max(jnp.abs(sinks))) <= MAX_UNSHIFTED_LOGIT).astype(jnp.int32)
    return pl.pallas_call(
        _swa_kernel,
        grid=(b, SWA_KV_HEADS),
        in_specs=[
            pl.BlockSpec(memory_space=pltpu.SMEM),
            pl.BlockSpec((1, nblk, SWA_GROUP * HEAD_DIM, w), lambda bi, hk: (bi, 0, hk, 0)),
            pl.BlockSpec((1, s, SWA_KV_HEADS * HEAD_DIM), lambda bi, hk: (bi, 0, 0)),
            pl.BlockSpec((1, nblk, HEAD_DIM, w), lambda bi, hk: (bi, 0, hk, 0)),
            pl.BlockSpec((2, None, 2 * w, gw), lambda bi, hk: (0, hk, 0, 0)),
            pl.BlockSpec((1, 1, gw), lambda bi, hk: (hk, 0, 0)),
        ],
        out_specs=pl.BlockSpec((1, s, SWA_GROUP * HEAD_DIM), lambda bi, hk: (bi, 0, hk)),
        out_shape=jax.ShapeDtypeStruct((b, s, n_qa), BF16),
        compiler_params=pltpu.CompilerParams(
            dimension_semantics=("parallel", "parallel"), vmem_limit_bytes=VMEM_LIMIT_BYTES),
        name="swa_attn",
    )(bounded.reshape(1), qa_t, ka, va_t, bias, sink_cols)


def _diff_kernel(bounded_ref, slope_ref, iq_tbl_ref, j_tbl_ref, q_ref, k_ref, v_ref, bias_ref, lam_ref, gsub_ref,
                 o_ref, acc_ref, l_ref, p0_ref, p1_ref, *, lam_init):
    hg = pl.program_id(1)
    nt, t_q = q_ref.shape[1], q_ref.shape[3]
    hpb = acc_ref.shape[0]
    n_steps = nt * (nt + 1) // 2
    dv = DIFF_V_DIM
    lp = lam_ref[...]
    lam = (jnp.exp(jnp.sum(lp[0:1] * lp[1:2], axis=-1, keepdims=True))
           - jnp.exp(jnp.sum(lp[2:3] * lp[3:4], axis=-1, keepdims=True)) + lam_init)
    zero = jnp.zeros((HEAD_DIM, t_q), BF16)

    def padded_q(iq, hh):
        q = q_ref[0, iq, hh * dv:(hh + 1) * dv, :]
        return jnp.concatenate([jnp.concatenate([q[:HEAD_DIM], zero], axis=0),
                                jnp.concatenate([zero, q[HEAD_DIM:]], axis=0)], axis=1)

    def logits(iq, j, hh, qb):
        kblk = k_ref[0, pl.ds(pl.multiple_of(j * t_q, t_q), t_q), hh * dv:(hh + 1) * dv]
        diag = (j == iq).astype(jnp.int32)
        cj = slope_ref[hg * hpb + hh] * ((j - iq) * t_q).astype(F32)
        return jnp.dot(kblk, qb, preferred_element_type=F32) + bias_ref[hh, diag], cj

    def finalize(iq, hh, acc, l):
        acc = acc * (1.0 / l)
        od = acc[:, :t_q] - lam * acc[:, t_q:]
        ms = jnp.mean(od * od, axis=0, keepdims=True)
        y = od * lax.rsqrt(ms + EPS) * gsub_ref[...]
        o_ref[0, pl.ds(pl.multiple_of(iq * t_q, t_q), t_q), hh * dv:(hh + 1) * dv] = y.T.astype(BF16)

    def scores_exp(t, p_ref):
        iq, j = iq_tbl_ref[t], j_tbl_ref[t]
        for hh in range(hpb):
            s, cj = logits(iq, j, hh, padded_q(iq, hh))
            p = jnp.exp2(s + cj)
            l_ref[hh, iq] += jnp.sum(p.reshape(t_q // 8, 8, 2 * t_q), axis=0)
            p_ref[hh] = p.astype(BF16)

    def pv(t, p_ref):
        iq, j = iq_tbl_ref[t], j_tbl_ref[t]
        for hh in range(hpb):
            acc_ref[hh, iq] += jnp.dot(v_ref[0, j, hh * dv:(hh + 1) * dv, :], p_ref[hh],
                                       preferred_element_type=F32)

    @pl.when(bounded_ref[0] != 0)
    def _():
        acc_ref[...] = jnp.zeros_like(acc_ref)
        l_ref[...] = jnp.zeros_like(l_ref)
        scores_exp(0, p0_ref)

        def pair(i, carry):
            pv(2 * i, p0_ref)
            scores_exp(2 * i + 1, p1_ref)
            pv(2 * i + 1, p1_ref)
            scores_exp(2 * i + 2, p0_ref)
            return carry

        lax.fori_loop(0, (n_steps - 1) // 2, pair, 0)
        if n_steps % 2 == 0:
            pv(n_steps - 2, p0_ref)
            scores_exp(n_steps - 1, p1_ref)
            pv(n_steps - 1, p1_ref)
        else:
            pv(n_steps - 1, p0_ref)

        def fin(iq, carry):
            for hh in range(hpb):
                finalize(iq, hh, acc_ref[hh, iq], jnp.sum(l_ref[hh, iq], axis=0, keepdims=True))
            return carry

        lax.fori_loop(0, nt, fin, 0)

    @pl.when(bounded_ref[0] == 0)
    def _():
        def q_block(iq, carry):
            qbs = [padded_q(iq, hh) for hh in range(hpb)]
            acc_ref[:, 0] = jnp.zeros((hpb, dv, 2 * t_q), F32)

            def step(j, stats):
                out = []
                for hh in range(hpb):
                    m, l = stats[2 * hh], stats[2 * hh + 1]
                    s, cj = logits(iq, j, hh, qbs[hh])
                    mnew = jnp.maximum(m, jnp.max(s, axis=0, keepdims=True) + cj)
                    alpha = jnp.exp2(m - mnew)
                    p = jnp.exp2(s + (cj - mnew))
                    out += [mnew, alpha * l + jnp.sum(p, axis=0, keepdims=True)]
                    acc_ref[hh, 0] = alpha * acc_ref[hh, 0] + jnp.dot(
                        v_ref[0, j, hh * dv:(hh + 1) * dv, :], p.astype(BF16), preferred_element_type=F32)
                return tuple(out)

            init = (jnp.full((1, 2 * t_q), NEG_INF, F32), jnp.zeros((1, 2 * t_q), F32)) * hpb
            stats = lax.fori_loop(0, iq + 1, step, init)
            for hh in range(hpb):
                finalize(iq, hh, acc_ref[hh, 0], stats[2 * hh + 1])
            return carry

        lax.fori_loop(0, nt, q_block, 0)


def _diff_bias(slopes, t):
    kj = np.arange(t)[:, None]
    qi = np.arange(2 * t)[None, :] % t
    rel = (kj - qi).astype(np.float32)
    out = np.empty((len(slopes), 2, t, 2 * t), np.float32)
    for h, sl in enumerate(slopes):
        out[h, 0] = sl * rel
        out[h, 1] = np.where(kj <= qi, sl * rel, -np.inf)
    return out


def _diff_attention(qd_t, kd, vd_t, lam_params, g_sub, slopes, lam_init, logit_bound, hpb=DIFF_HEADS_PER_STEP):
    b, nt, n_qd, t = qd_t.shape
    s = nt * t
    nh = n_qd // DIFF_V_DIM
    bias = jnp.asarray(_diff_bias(slopes, t) * LOG2E)
    gsub = jnp.broadcast_to((g_sub * (1.0 - lam_init))[:, None], (DIFF_V_DIM, t)).astype(F32)
    bounded = (logit_bound <= MAX_UNSHIFTED_LOGIT).astype(jnp.int32)
    pairs = [(iq, j) for iq in range(nt) for j in range(iq + 1)]
    iq_tbl = jnp.asarray([pq[0] for pq in pairs], jnp.int32)
    j_tbl = jnp.asarray([pq[1] for pq in pairs], jnp.int32)
    kern = functools.partial(_diff_kernel, lam_init=lam_init)
    return pl.pallas_call(
        kern,
        grid=(b, nh // hpb),
        in_specs=[
            pl.BlockSpec(memory_space=pltpu.SMEM),
            pl.BlockSpec(memory_space=pltpu.SMEM),
            pl.BlockSpec(memory_space=pltpu.SMEM),
            pl.BlockSpec(memory_space=pltpu.SMEM),
            pl.BlockSpec((1, nt, hpb * DIFF_V_DIM, t), lambda bi, hg: (bi, 0, hg, 0)),
            pl.BlockSpec((1, s, hpb * DIFF_V_DIM), lambda bi, hg: (bi, 0, hg)),
            pl.BlockSpec((1, nt, hpb * DIFF_V_DIM, t), lambda bi, hg: (bi, 0, hg, 0)),
            pl.BlockSpec((hpb, 2, t, 2 * t), lambda bi, hg: (hg, 0, 0, 0)),
            pl.BlockSpec((4, HEAD_DIM), lambda bi, hg: (0, 0)),
            pl.BlockSpec((DIFF_V_DIM, t), lambda bi, hg: (0, 0)),
        ],
        out_specs=pl.BlockSpec((1, s, hpb * DIFF_V_DIM), lambda bi, hg: (bi, 0, hg)),
        out_shape=jax.ShapeDtypeStruct((b, s, n_qd), BF16),
        scratch_shapes=[pltpu.VMEM((hpb, nt, DIFF_V_DIM, 2 * t), F32),
                        pltpu.VMEM((hpb, nt, 8, 2 * t), F32),
                        pltpu.VMEM((hpb, t, 2 * t), BF16),
                        pltpu.VMEM((hpb, t, 2 * t), BF16)],
        compiler_params=pltpu.CompilerParams(
            dimension_semantics=("parallel", "parallel"), vmem_limit_bytes=VMEM_LIMIT_BYTES),
        name="diff_attn",
    )(bounded.reshape(1), jnp.asarray(np.asarray(slopes) * LOG2E, F32), iq_tbl, j_tbl,
      qd_t, kd, vd_t, bias, lam_params, gsub)


def _outproj_kernel(x_ref, ya_ref, yd_ref, wa_ref, wd_ref, o_ref):
    o_ref[...] = (x_ref[...]
                  + jnp.dot(ya_ref[...], wa_ref[...], preferred_element_type=F32)
                  + jnp.dot(yd_ref[...], wd_ref[...], preferred_element_type=F32))


def _outproj(x2, ya2, yd2, wa, wd, tm=512):
    n, d = x2.shape
    const = lambda shape: _resident(shape, lambda i: (0, 0))
    return pl.pallas_call(
        _outproj_kernel,
        grid=(n // tm,),
        in_specs=[
            pl.BlockSpec((tm, d), lambda i: (i, 0)),
            pl.BlockSpec((tm, ya2.shape[1]), lambda i: (i, 0)),
            pl.BlockSpec((tm, yd2.shape[1]), lambda i: (i, 0)),
            const(wa.shape), const(wd.shape),
        ],
        out_specs=pl.BlockSpec((tm, d), lambda i: (i, 0)),
        out_shape=jax.ShapeDtypeStruct((n, d), F32),
        compiler_params=pltpu.CompilerParams(
            dimension_semantics=("parallel",), vmem_limit_bytes=VMEM_LIMIT_BYTES),
        name="outproj",
    )(x2, ya2, yd2, wa, wd)


def _ffn_kernel(h_ref, g_ref, wg_ref, wu_ref, wd_ref, o_ref, u_ref):
    f = pl.program_id(1)

    @pl.when(f == 0)
    def _():
        h = h_ref[...]
        u_ref[...] = _rms_rows(h, g_ref[...]).astype(BF16)
        o_ref[...] = h

    u = u_ref[...]
    gate = jnp.dot(u, wg_ref[...], preferred_element_type=F32)
    up = jnp.dot(u, wu_ref[...], preferred_element_type=F32)
    act = (gate * (1.0 / (1.0 + jnp.exp(-gate))) * up).astype(BF16)
    o_ref[...] += jnp.dot(act, wd_ref[...], preferred_element_type=F32)


def _ffn(h2, g_ffn, wg, wu, wd, tm=512, tf=512):
    n, d = h2.shape
    dff = wg.shape[1]
    return pl.pallas_call(
        _ffn_kernel,
        grid=(n // tm, dff // tf),
        in_specs=[
            pl.BlockSpec((tm, d), lambda i, f: (i, 0)),
            _resident((1, d), lambda i, f: (0, 0)),
            pl.BlockSpec((d, tf), lambda i, f: (0, f)),
            pl.BlockSpec((d, tf), lambda i, f: (0, f)),
            pl.BlockSpec((tf, d), lambda i, f: (f, 0)),
        ],
        out_specs=pl.BlockSpec((tm, d), lambda i, f: (i, 0)),
        out_shape=jax.ShapeDtypeStruct((n, d), F32),
        scratch_shapes=[pltpu.VMEM((tm, d), BF16)],
        compiler_params=pltpu.CompilerParams(
            dimension_semantics=("parallel", "arbitrary"), vmem_limit_bytes=VMEM_LIMIT_BYTES),
        name="ffn",
    )(h2, g_ffn[None, :], wg, wu, wd)


def _ple_kernel(h_ref, p_ref, g_ref, wg_ref, wp_ref, go_ref, o_ref):
    h = h_ref[...]
    u = _rms_rows(h, g_ref[...]).astype(BF16)
    z = jnp.dot(u, wg_ref[...], preferred_element_type=F32)
    gate = 1.0 / (1.0 + jnp.exp(-z))
    pp = jnp.dot(p_ref[...].astype(BF16), wp_ref[...], preferred_element_type=F32)
    o_ref[...] = h + gate * _rms_rows(pp, go_ref[...])


def _ple(h2, p2, g_ple, wg, wp, g_out, tm=512):
    n, d = h2.shape
    const = lambda shape: _resident(shape, lambda i: (0, 0))
    return pl.pallas_call(
        _ple_kernel,
        grid=(n // tm,),
        in_specs=[
            pl.BlockSpec((tm, d), lambda i: (i, 0)),
            pl.BlockSpec((tm, p2.shape[1]), lambda i: (i, 0)),
            const((1, d)), const(wg.shape), const(wp.shape), const((1, d)),
        ],
        out_specs=pl.BlockSpec((tm, d), lambda i: (i, 0)),
        out_shape=jax.ShapeDtypeStruct((n, d), F32),
        compiler_params=pltpu.CompilerParams(
            dimension_semantics=("parallel",), vmem_limit_bytes=VMEM_LIMIT_BYTES),
        name="ple",
    )(h2, p2, g_ple[None, :], wg, wp, g_out[None, :])


def _alibi_slopes(n):
    return [2.0 ** (-8.0 * (h + 1) / n) for h in range(n)]


def _logit_bound(q_gain, k_gain):
    return 1.02 * math.sqrt(HEAD_DIM) * jnp.max(jnp.abs(q_gain)) * jnp.max(jnp.abs(k_gain))


def kernel(x, p, g_attn, w_in, qn_swa, kn_swa, sinks, qn_diff, kn_diff, lambda_q1, lambda_k1, lambda_q2,
           lambda_k2, g_sub, w_out, g_ffn, w_gate, w_up, w_down, g_ple, w_ple_gate, w_ple_proj, g_ple_out):
    b, s, d = x.shape
    depth = p.shape[0]
    n_qa = d // 2
    n_ka = n_va = SWA_KV_HEADS * HEAD_DIM
    n_qd = n_kd = n_vd = d // 2
    diff_heads = n_vd // DIFF_V_DIM
    swa_heads = n_qa // HEAD_DIM
    assert swa_heads == SWA_KV_HEADS * SWA_GROUP and s % TOKEN_TILE == 0
    c = np.cumsum([0, n_qa, n_ka, n_va, n_qd, n_kd, n_vd])
    h = x
    for i in range(depth):
        lam_init = 0.8 - 0.6 * math.exp(-0.3 * i)
        w = w_in[i]
        col = lambda k: w[:, c[k]:c[k + 1]]
        wfm = jnp.concatenate([col(0), col(2), col(3), col(5)], axis=1).T.astype(BF16)
        wtm = jnp.concatenate([col(1), col(4)], axis=1).astype(BF16)
        qa_t, va_t, qd_t, vd_t, ka, kd = _inproj(
            h, g_attn[i], wfm, wtm, qn_swa[i], kn_swa[i], qn_diff[i], kn_diff[i],
            (n_qa, n_ka, n_va, n_qd, n_kd, n_vd))
        ya = _swa_attention(qa_t, ka, va_t, sinks[i], _alibi_slopes(swa_heads),
                            _logit_bound(qn_swa[i], kn_swa[i]))
        lam_params = jnp.stack([lambda_q1[i], lambda_k1[i], lambda_q2[i], lambda_k2[i]]).astype(F32)
        yd = _diff_attention(qd_t, kd, vd_t, lam_params, g_sub[i], _alibi_slopes(diff_heads), lam_init,
                             _logit_bound(qn_diff[i], kn_diff[i]))
        wo = w_out[i].astype(BF16)
        h2 = _outproj(h.reshape(b * s, d), ya.reshape(b * s, n_qa), yd.reshape(b * s, n_vd),
                      wo[:n_qa], wo[n_qa:])
        h2 = _ffn(h2, g_ffn[i], w_gate[i].astype(BF16), w_up[i].astype(BF16), w_down[i].astype(BF16))
        h2 = _ple(h2, p[i].reshape(b * s, -1), g_ple[i], w_ple_gate[i].astype(BF16),
                  w_ple_proj[i].astype(BF16), g_ple_out[i])
        h = h2.reshape(b, s, d)
    return h
```

```python
import functools
import math

import jax
import jax.numpy as jnp
import numpy as np
from jax import lax
from jax.experimental import pallas as pl
from jax.experimental.pallas import tpu as pltpu

F32 = jnp.float32
BF16 = jnp.bfloat16

HEAD_DIM = 64
SWA_BLOCK = 128
SWA_KV_HEADS = 2
SWA_GROUP = 8
DIFF_V_DIM = 2 * HEAD_DIM
EPS = 1e-6
NEG_INF = float("-inf")
LOG2E = math.log2(math.e)
MAX_UNSHIFTED_LOGIT = 60.0

TOKEN_TILE = 256
DIFF_HEADS_PER_STEP = 4
VMEM_LIMIT_BYTES = 56 * 1024 * 1024

_NT = (((1,), (1,)), ((), ()))


def _resident(shape, index_map):
    return pl.BlockSpec(shape, index_map, pipeline_mode=pl.Buffered(1))


def _rms_rows(x, gain):
    ms = jnp.mean(x * x, axis=-1, keepdims=True)
    return x * lax.rsqrt(ms + EPS) * gain


def _inproj_kernel(x_ref, g_ref, wfm_ref, wtm_ref, gqa_ref, gqd_ref, gka_ref, gkd_ref, bd_ref,
                   qa_ref, va_ref, qd_ref, vd_ref, ka_ref, kd_ref, *, n_qa, n_va, n_qd, n_vd, n_ka, n_kd):
    u = _rms_rows(x_ref[...], g_ref[...]).astype(BF16)
    tm = u.shape[0]

    def fm(row0, nrows):
        return lax.dot_general(wfm_ref[row0:row0 + nrows, :], u, _NT, preferred_element_type=F32)

    def headnorm_fm(z, gain, store):
        for h in range(z.shape[0] // HEAD_DIM):
            zh = z[h * HEAD_DIM:(h + 1) * HEAD_DIM, :]
            ms = jnp.mean(zh * zh, axis=0, keepdims=True)
            store(h, (zh * lax.rsqrt(ms + EPS) * gain).astype(BF16))

    def store_qa(h, val):
        for t in range(tm // SWA_BLOCK):
            qa_ref[0, t, h * HEAD_DIM:(h + 1) * HEAD_DIM, :] = val[:, t * SWA_BLOCK:(t + 1) * SWA_BLOCK]

    def store_qd(h, val):
        qd_ref[0, 0, h * HEAD_DIM:(h + 1) * HEAD_DIM, :] = val

    r = 0
    headnorm_fm(fm(r, n_qa), gqa_ref[...], store_qa)
    r += n_qa
    zva = fm(r, n_va).astype(BF16)
    for t in range(tm // SWA_BLOCK):
        va_ref[0, t] = zva[:, t * SWA_BLOCK:(t + 1) * SWA_BLOCK]
    r += n_va
    headnorm_fm(fm(r, n_qd), gqd_ref[...], store_qd)
    r += n_qd
    vd_ref[0, 0] = fm(r, n_vd).astype(BF16)

    def headnorm_tm(col0, ncols, gain):
        z = jnp.dot(u, wtm_ref[:, col0:col0 + ncols], preferred_element_type=F32)
        ssq = jnp.dot((z * z).astype(BF16), bd_ref[0:ncols, 0:ncols], preferred_element_type=F32)
        return (z * lax.rsqrt(ssq * (1.0 / HEAD_DIM) + EPS) * gain).astype(BF16)

    ka_ref[0] = headnorm_tm(0, n_ka, gka_ref[...])
    cw = bd_ref.shape[0]
    for c in range(n_kd // cw):
        kd_ref[0, :, c * cw:(c + 1) * cw] = headnorm_tm(n_ka + c * cw, cw, gkd_ref[...])


def _inproj(x, g_attn, wfm, wtm, qn_swa, kn_swa, qn_diff, kn_diff, dims):
    b, s, d = x.shape
    n_qa, n_ka, n_va, n_qd, n_kd, n_vd = dims
    tm = TOKEN_TILE
    nt = s // tm
    scale = LOG2E / math.sqrt(HEAD_DIM)
    gqa = jnp.broadcast_to((qn_swa * scale)[:, None], (HEAD_DIM, tm)).astype(F32)
    gqd = jnp.broadcast_to((qn_diff * scale)[:, None], (HEAD_DIM, tm)).astype(F32)
    gka = jnp.tile(kn_swa, n_ka // HEAD_DIM)[None, :].astype(F32)
    gkd = jnp.tile(kn_diff, 256 // HEAD_DIM)[None, :].astype(F32)
    idx = np.arange(256) // HEAD_DIM
    bd = jnp.asarray(idx[:, None] == idx[None, :], dtype=BF16)
    const = lambda shape: _resident(shape, lambda bi, ti: (0,) * len(shape))
    kern = functools.partial(_inproj_kernel, n_qa=n_qa, n_va=n_va, n_qd=n_qd, n_vd=n_vd, n_ka=n_ka, n_kd=n_kd)
    return pl.pallas_call(
        kern,
        grid=(b, nt),
        in_specs=[
            pl.BlockSpec((None, tm, d), lambda bi, ti: (bi, ti, 0)),
            const((1, d)), const(wfm.shape), const(wtm.shape),
            const((HEAD_DIM, tm)), const((HEAD_DIM, tm)), const((1, n_ka)), const((1, 256)), const((256, 256)),
        ],
        out_specs=[
            pl.BlockSpec((1, tm // SWA_BLOCK, n_qa, SWA_BLOCK), lambda bi, ti: (bi, ti, 0, 0)),
            pl.BlockSpec((1, tm // SWA_BLOCK, n_va, SWA_BLOCK), lambda bi, ti: (bi, ti, 0, 0)),
            pl.BlockSpec((1, 1, n_qd, tm), lambda bi, ti: (bi, ti, 0, 0)),
            pl.BlockSpec((1, 1, n_vd, tm), lambda bi, ti: (bi, ti, 0, 0)),
            pl.BlockSpec((1, tm, n_ka), lambda bi, ti: (bi, ti, 0)),
            pl.BlockSpec((1, tm, n_kd), lambda bi, ti: (bi, ti, 0)),
        ],
        out_shape=[
            jax.ShapeDtypeStruct((b, s // SWA_BLOCK, n_qa, SWA_BLOCK), BF16),
            jax.ShapeDtypeStruct((b, s // SWA_BLOCK, n_va, SWA_BLOCK), BF16),
            jax.ShapeDtypeStruct((b, nt, n_qd, tm), BF16),
            jax.ShapeDtypeStruct((b, nt, n_vd, tm), BF16),
            jax.ShapeDtypeStruct((b, s, n_ka), BF16),
            jax.ShapeDtypeStruct((b, s, n_kd), BF16),
        ],
        compiler_params=pltpu.CompilerParams(
            dimension_semantics=("parallel", "parallel"), vmem_limit_bytes=VMEM_LIMIT_BYTES),
        name="inproj",
    )(x, g_attn[None, :], wfm, wtm, gqa, gqd, gka, gkd, bd)


def _staged_pipeline(n_steps, stages):
    depth = len(stages)

    def iteration(i, parity):
        for k in reversed(range(depth)):
            t = i - k
            if isinstance(i, int) and not 0 <= t < n_steps:
                continue
            stages[k](t, (parity - k) % 2)

    fill_end = min(depth - 1, n_steps)
    for i in range(fill_end):
        iteration(i, i % 2)
    n_pairs = (n_steps - fill_end) // 2

    def pair(m, carry):
        i = fill_end + 2 * m
        iteration(i, fill_end % 2)
        iteration(i + 1, (fill_end + 1) % 2)
        return carry

    lax.fori_loop(0, n_pairs, pair, 0)
    for i in range(fill_end + 2 * n_pairs, n_steps + depth - 1):
        iteration(i, i % 2)


def _swa_kernel(bounded_ref, q_ref, k_ref, v_ref, bias_ref, sink_ref, o_ref, s_ref, p_ref, d_ref):
    hk = pl.program_id(1)
    nblk = q_ref.shape[1]
    w = SWA_BLOCK
    sink = sink_ref[0]

    def window_start(n):
        return jnp.maximum(n - 1, 0)

    def scores(n, slot):
        qblk = q_ref[0, n]
        qg = jnp.concatenate([qblk[g * HEAD_DIM:(g + 1) * HEAD_DIM, :] for g in range(SWA_GROUP)], axis=1)
        zero = jnp.zeros_like(qg)
        qpad = jnp.concatenate([jnp.where(hk == 0, qg, zero), jnp.where(hk == 1, qg, zero)], axis=0)
        kwin = k_ref[0, pl.ds(pl.multiple_of(window_start(n) * w, w), 2 * w), :]
        s_ref[slot] = jnp.dot(kwin, qpad, preferred_element_type=F32)

    def probs(n, slot, *, bounded):
        t = s_ref[slot] + bias_ref[jnp.minimum(n, 1)]
        if bounded:
            e = jnp.exp2(t)
            d_ref[slot] = jnp.sum(e, axis=0, keepdims=True) + jnp.exp2(sink)
        else:
            m = jnp.maximum(jnp.max(t, axis=0, keepdims=True), sink)
            e = jnp.exp2(t - m)
            d_ref[slot] = jnp.sum(e, axis=0, keepdims=True) + jnp.exp2(sink - m)
        p_ref[slot] = e.astype(BF16)

    def weighted_values(n, slot):
        first = window_start(n)
        vwin = jnp.concatenate([v_ref[0, first], v_ref[0, first + 1]], axis=1)
        o = jnp.dot(vwin, p_ref[slot], preferred_element_type=F32) * (1.0 / d_ref[slot])
        for gp in range(SWA_GROUP // 2):
            pair = jnp.concatenate([o[:, (2 * gp) * w:(2 * gp + 1) * w],
                                    o[:, (2 * gp + 1) * w:(2 * gp + 2) * w]], axis=0)
            o_ref[0, pl.ds(pl.multiple_of(n * w, w), w), gp * 2 * HEAD_DIM:(gp + 1) * 2 * HEAD_DIM] = (
                pair.T.astype(BF16))

    @pl.when(bounded_ref[0] != 0)
    def _():
        _staged_pipeline(nblk, [scores, functools.partial(probs, bounded=True), weighted_values])

    @pl.when(bounded_ref[0] == 0)
    def _():
        def block(n, carry):
            scores(n, 0)
            probs(n, 0, bounded=False)
            weighted_values(n, 0)
            return carry

        lax.fori_loop(0, nblk, block, 0)


def _swa_bias(slopes):
    w = SWA_BLOCK
    kj = np.arange(2 * w)[:, None]
    qi = np.arange(w)[None, :]
    out = np.empty((2, SWA_KV_HEADS, 2 * w, SWA_GROUP * w), np.float32)
    for first, dist in ((0, qi - kj), (1, qi - kj + w)):
        valid = (dist >= 0) & (dist < w)
        for hk in range(SWA_KV_HEADS):
            for g in range(SWA_GROUP):
                sl = slopes[hk * SWA_GROUP + g]
                out[first, hk, :, g * w:(g + 1) * w] = np.where(valid, -sl * dist, -np.inf)
    return out


def _swa_attention(qa_t, ka, va_t, sinks, slopes, logit_bound):
    b, nblk, n_qa, w = qa_t.shape
    s = nblk * w
    gw = SWA_GROUP * w
    bias = jnp.asarray(_swa_bias(slopes) * LOG2E)
    sinks = sinks.astype(F32)
    sink_cols = jnp.repeat((sinks * LOG2E).reshape(SWA_KV_HEADS, 1, SWA_GROUP), w, axis=-1)
    bounded = jnp.asarray(jnp.maximum(logit_bound, jnp.max(jnp.abs(sinks))) <= MAX_UNSHIFTED_LOGIT, jnp.int32)
    return pl.pallas_call(
        _swa_kernel,
        grid=(b, SWA_KV_HEADS),
        in_specs=[
            pl.BlockSpec(memory_space=pltpu.SMEM),
            pl.BlockSpec((1, nblk, SWA_GROUP * HEAD_DIM, w), lambda bi, hk: (bi, 0, hk, 0)),
            pl.BlockSpec((1, s, SWA_KV_HEADS * HEAD_DIM), lambda bi, hk: (bi, 0, 0)),
            pl.BlockSpec((1, nblk, HEAD_DIM, w), lambda bi, hk: (bi, 0, hk, 0)),
            pl.BlockSpec((2, None, 2 * w, gw), lambda bi, hk: (0, hk, 0, 0)),
            pl.BlockSpec((1, 1, gw), lambda bi, hk: (hk, 0, 0)),
        ],
        out_specs=pl.BlockSpec((1, s, SWA_GROUP * HEAD_DIM), lambda bi, hk: (bi, 0, hk)),
        out_shape=jax.ShapeDtypeStruct((b, s, n_qa), BF16),
        scratch_shapes=[pltpu.VMEM((2, 2 * w, gw), F32), pltpu.VMEM((2, 2 * w, gw), BF16),
                        pltpu.VMEM((2, 1, gw), F32)],
        compiler_params=pltpu.CompilerParams(
            dimension_semantics=("parallel", "parallel"), vmem_limit_bytes=VMEM_LIMIT_BYTES),
        name="swa_attn",
    )(bounded.reshape(1), qa_t, ka, va_t, bias, sink_cols)


def _diff_kernel(bounded_ref, slope_ref, iq_tbl_ref, j_tbl_ref, q_ref, k_ref, v_ref, bias_ref, lam_ref, gsub_ref,
                 o_ref, acc_ref, l_ref, p_ref, *, lam_init):
    hg = pl.program_id(1)
    nt, t_q = q_ref.shape[1], q_ref.shape[3]
    hpb = acc_ref.shape[0]
    n_steps = nt * (nt + 1) // 2
    dv = DIFF_V_DIM
    lp = lam_ref[...]
    lam = (jnp.exp(jnp.sum(lp[0:1] * lp[1:2], axis=-1, keepdims=True))
           - jnp.exp(jnp.sum(lp[2:3] * lp[3:4], axis=-1, keepdims=True)) + lam_init)
    zero = jnp.zeros((HEAD_DIM, t_q), BF16)

    def padded_q(iq, hh):
        q = q_ref[0, iq, hh * dv:(hh + 1) * dv, :]
        return jnp.concatenate([jnp.concatenate([q[:HEAD_DIM], zero], axis=0),
                                jnp.concatenate([zero, q[HEAD_DIM:]], axis=0)], axis=1)

    def logits(iq, j, hh, qb):
        kblk = k_ref[0, pl.ds(pl.multiple_of(j * t_q, t_q), t_q), hh * dv:(hh + 1) * dv]
        diag = (j == iq).astype(jnp.int32)
        cj = slope_ref[hg * hpb + hh] * ((j - iq) * t_q).astype(F32)
        return jnp.dot(kblk, qb, preferred_element_type=F32) + bias_ref[hh, diag], cj

    def finalize(iq, hh, acc, l):
        acc = acc * (1.0 / l)
        od = acc[:, :t_q] - lam * acc[:, t_q:]
        ms = jnp.mean(od * od, axis=0, keepdims=True)
        y = od * lax.rsqrt(ms + EPS) * gsub_ref[...]
        o_ref[0, pl.ds(pl.multiple_of(iq * t_q, t_q), t_q), hh * dv:(hh + 1) * dv] = y.T.astype(BF16)

    def probs(t, slot):
        iq, j = iq_tbl_ref[t], j_tbl_ref[t]
        for hh in range(hpb):
            s, cj = logits(iq, j, hh, padded_q(iq, hh))
            p = jnp.exp2(s + cj)
            l_ref[hh, iq] += jnp.sum(p.reshape(t_q // 8, 8, 2 * t_q), axis=0)
            p_ref[slot, hh] = p.astype(BF16)

    def pv(t, slot):
        iq, j = iq_tbl_ref[t], j_tbl_ref[t]
        for hh in range(hpb):
            acc_ref[hh, iq] += jnp.dot(v_ref[0, j, hh * dv:(hh + 1) * dv, :], p_ref[slot, hh],
                                       preferred_element_type=F32)

    @pl.when(bounded_ref[0] != 0)
    def _():
        acc_ref[...] = jnp.zeros_like(acc_ref)
        l_ref[...] = jnp.zeros_like(l_ref)
        _staged_pipeline(n_steps, [probs, pv])

        def fin(iq, carry):
            for hh in range(hpb):
                finalize(iq, hh, acc_ref[hh, iq], jnp.sum(l_ref[hh, iq], axis=0, keepdims=True))
            return carry

        lax.fori_loop(0, nt, fin, 0)

    @pl.when(bounded_ref[0] == 0)
    def _():
        def q_block(iq, carry):
            qbs = [padded_q(iq, hh) for hh in range(hpb)]
            acc_ref[:, 0] = jnp.zeros((hpb, dv, 2 * t_q), F32)

            def step(j, stats):
                out = []
                for hh in range(hpb):
                    m, l = stats[2 * hh], stats[2 * hh + 1]
                    s, cj = logits(iq, j, hh, qbs[hh])
                    mnew = jnp.maximum(m, jnp.max(s, axis=0, keepdims=True) + cj)
                    alpha = jnp.exp2(m - mnew)
                    p = jnp.exp2(s + (cj - mnew))
                    out += [mnew, alpha * l + jnp.sum(p, axis=0, keepdims=True)]
                    acc_ref[hh, 0] = alpha * acc_ref[hh, 0] + jnp.dot(
                        v_ref[0, j, hh * dv:(hh + 1) * dv, :], p.astype(BF16), preferred_element_type=F32)
                return tuple(out)

            init = (jnp.full((1, 2 * t_q), NEG_INF, F32), jnp.zeros((1, 2 * t_q), F32)) * hpb
            stats = lax.fori_loop(0, iq + 1, step, init)
            for hh in range(hpb):
                finalize(iq, hh, acc_ref[hh, 0], stats[2 * hh + 1])
            return carry

        lax.fori_loop(0, nt, q_block, 0)


def _diff_bias(slopes, t):
    kj = np.arange(t)[:, None]
    qi = np.arange(2 * t)[None, :] % t
    rel = (kj - qi).astype(np.float32)
    out = np.empty((len(slopes), 2, t, 2 * t), np.float32)
    for h, sl in enumerate(slopes):
        out[h, 0] = sl * rel
        out[h, 1] = np.where(kj <= qi, sl * rel, -np.inf)
    return out


def _diff_attention(qd_t, kd, vd_t, lam_params, g_sub, slopes, lam_init, logit_bound, hpb=DIFF_HEADS_PER_STEP):
    b, nt, n_qd, t = qd_t.shape
    s = nt * t
    nh = n_qd // DIFF_V_DIM
    bias = jnp.asarray(_diff_bias(slopes, t) * LOG2E)
    gsub = jnp.broadcast_to((g_sub * (1.0 - lam_init))[:, None], (DIFF_V_DIM, t)).astype(F32)
    bounded = jnp.asarray(logit_bound <= MAX_UNSHIFTED_LOGIT, jnp.int32)
    pairs = [(iq, j) for iq in range(nt) for j in range(iq + 1)]
    iq_tbl = jnp.asarray([pq[0] for pq in pairs], jnp.int32)
    j_tbl = jnp.asarray([pq[1] for pq in pairs], jnp.int32)
    kern = functools.partial(_diff_kernel, lam_init=lam_init)
    return pl.pallas_call(
        kern,
        grid=(b, nh // hpb),
        in_specs=[
            pl.BlockSpec(memory_space=pltpu.SMEM),
            pl.BlockSpec(memory_space=pltpu.SMEM),
            pl.BlockSpec(memory_space=pltpu.SMEM),
            pl.BlockSpec(memory_space=pltpu.SMEM),
            pl.BlockSpec((1, nt, hpb * DIFF_V_DIM, t), lambda bi, hg: (bi, 0, hg, 0)),
            pl.BlockSpec((1, s, hpb * DIFF_V_DIM), lambda bi, hg: (bi, 0, hg)),
            pl.BlockSpec((1, nt, hpb * DIFF_V_DIM, t), lambda bi, hg: (bi, 0, hg, 0)),
            pl.BlockSpec((hpb, 2, t, 2 * t), lambda bi, hg: (hg, 0, 0, 0)),
            pl.BlockSpec((4, HEAD_DIM), lambda bi, hg: (0, 0)),
            pl.BlockSpec((DIFF_V_DIM, t), lambda bi, hg: (0, 0)),
        ],
        out_specs=pl.BlockSpec((1, s, hpb * DIFF_V_DIM), lambda bi, hg: (bi, 0, hg)),
        out_shape=jax.ShapeDtypeStruct((b, s, n_qd), BF16),
        scratch_shapes=[pltpu.VMEM((hpb, nt, DIFF_V_DIM, 2 * t), F32),
                        pltpu.VMEM((hpb, nt, 8, 2 * t), F32),
                        pltpu.VMEM((2, hpb, t, 2 * t), BF16)],
        compiler_params=pltpu.CompilerParams(
            dimension_semantics=("parallel", "parallel"), vmem_limit_bytes=VMEM_LIMIT_BYTES),
        name="diff_attn",
    )(bounded.reshape(1), jnp.asarray(np.asarray(slopes) * LOG2E, F32), iq_tbl, j_tbl,
      qd_t, kd, vd_t, bias, lam_params, gsub)


def _outproj_kernel(x_ref, ya_ref, yd_ref, wa_ref, wd_ref, o_ref):
    o_ref[...] = (x_ref[...]
                  + jnp.dot(ya_ref[...], wa_ref[...], preferred_element_type=F32)
                  + jnp.dot(yd_ref[...], wd_ref[...], preferred_element_type=F32))


def _outproj(x2, ya2, yd2, wa, wd, tm=512):
    n, d = x2.shape
    tm = min(tm, n)
    assert n % tm == 0
    const = lambda shape: _resident(shape, lambda i: (0, 0))
    return pl.pallas_call(
        _outproj_kernel,
        grid=(n // tm,),
        in_specs=[
            pl.BlockSpec((tm, d), lambda i: (i, 0)),
            pl.BlockSpec((tm, ya2.shape[1]), lambda i: (i, 0)),
            pl.BlockSpec((tm, yd2.shape[1]), lambda i: (i, 0)),
            const(wa.shape), const(wd.shape),
        ],
        out_specs=pl.BlockSpec((tm, d), lambda i: (i, 0)),
        out_shape=jax.ShapeDtypeStruct((n, d), F32),
        compiler_params=pltpu.CompilerParams(
            dimension_semantics=("parallel",), vmem_limit_bytes=VMEM_LIMIT_BYTES),
        name="outproj",
    )(x2, ya2, yd2, wa, wd)


def _ffn_kernel(h_ref, g_ref, wg_ref, wu_ref, wd_ref, o_ref, u_ref):
    f = pl.program_id(1)

    @pl.when(f == 0)
    def _():
        h = h_ref[...]
        u_ref[...] = _rms_rows(h, g_ref[...]).astype(BF16)
        o_ref[...] = h

    u = u_ref[...]
    gate = jnp.dot(u, wg_ref[...], preferred_element_type=F32)
    up = jnp.dot(u, wu_ref[...], preferred_element_type=F32)
    act = (gate * (1.0 / (1.0 + jnp.exp(-gate))) * up).astype(BF16)
    o_ref[...] += jnp.dot(act, wd_ref[...], preferred_element_type=F32)


def _ffn(h2, g_ffn, wg, wu, wd, tm=1024, tf=512):
    n, d = h2.shape
    tm = min(tm, n)
    assert n % tm == 0
    dff = wg.shape[1]
    return pl.pallas_call(
        _ffn_kernel,
        grid=(n // tm, dff // tf),
        in_specs=[
            pl.BlockSpec((tm, d), lambda i, f: (i, 0)),
            _resident((1, d), lambda i, f: (0, 0)),
            pl.BlockSpec((d, tf), lambda i, f: (0, f)),
            pl.BlockSpec((d, tf), lambda i, f: (0, f)),
            pl.BlockSpec((tf, d), lambda i, f: (f, 0)),
        ],
        out_specs=pl.BlockSpec((tm, d), lambda i, f: (i, 0)),
        out_shape=jax.ShapeDtypeStruct((n, d), F32),
        scratch_shapes=[pltpu.VMEM((tm, d), BF16)],
        compiler_params=pltpu.CompilerParams(
            dimension_semantics=("parallel", "arbitrary"), vmem_limit_bytes=VMEM_LIMIT_BYTES),
        name="ffn",
    )(h2, g_ffn[None, :], wg, wu, wd)


def _ple_kernel(h_ref, p_ref, g_ref, wg_ref, wp_ref, go_ref, o_ref):
    h = h_ref[...]
    u = _rms_rows(h, g_ref[...]).astype(BF16)
    z = jnp.dot(u, wg_ref[...], preferred_element_type=F32)
    gate = 1.0 / (1.0 + jnp.exp(-z))
    pp = jnp.dot(p_ref[...].astype(BF16), wp_ref[...], preferred_element_type=F32)
    o_ref[...] = h + gate * _rms_rows(pp, go_ref[...])


def _ple(h2, p2, g_ple, wg, wp, g_out, tm=512):
    n, d = h2.shape
    tm = min(tm, n)
    assert n % tm == 0
    const = lambda shape: _resident(shape, lambda i: (0, 0))
    return pl.pallas_call(
        _ple_kernel,
        grid=(n // tm,),
        in_specs=[
            pl.BlockSpec((tm, d), lambda i: (i, 0)),
            pl.BlockSpec((tm, p2.shape[1]), lambda i: (i, 0)),
            const((1, d)), const(wg.shape), const(wp.shape), const((1, d)),
        ],
        out_specs=pl.BlockSpec((tm, d), lambda i: (i, 0)),
        out_shape=jax.ShapeDtypeStruct((n, d), F32),
        compiler_params=pltpu.CompilerParams(
            dimension_semantics=("parallel",), vmem_limit_bytes=VMEM_LIMIT_BYTES),
        name="ple",
    )(h2, p2, g_ple[None, :], wg, wp, g_out[None, :])


def _alibi_slopes(n):
    return [2.0 ** (-8.0 * (h + 1) / n) for h in range(n)]


def _logit_bound(q_gain, k_gain):
    return 1.02 * math.sqrt(HEAD_DIM) * jnp.max(jnp.abs(q_gain)) * jnp.max(jnp.abs(k_gain))


def kernel(x, p, g_attn, w_in, qn_swa, kn_swa, sinks, qn_diff, kn_diff, lambda_q1, lambda_k1, lambda_q2,
           lambda_k2, g_sub, w_out, g_ffn, w_gate, w_up, w_down, g_ple, w_ple_gate, w_ple_proj, g_ple_out):
    b, s, d = x.shape
    depth = p.shape[0]
    n_qa = d // 2
    n_ka = n_va = SWA_KV_HEADS * HEAD_DIM
    n_qd = n_kd = n_vd = d // 2
    diff_heads = n_vd // DIFF_V_DIM
    swa_heads = n_qa // HEAD_DIM
    assert swa_heads == SWA_KV_HEADS * SWA_GROUP and s % TOKEN_TILE == 0
    c = np.cumsum([0, n_qa, n_ka, n_va, n_qd, n_kd, n_vd])
    h = x
    for i in range(depth):
        lam_init = 0.8 - 0.6 * math.exp(-0.3 * i)
        w = w_in[i]
        col = lambda k: w[:, c[k]:c[k + 1]]
        wfm = jnp.concatenate([col(0), col(2), col(3), col(5)], axis=1).T.astype(BF16)
        wtm = jnp.concatenate([col(1), col(4)], axis=1).astype(BF16)
        qa_t, va_t, qd_t, vd_t, ka, kd = _inproj(
            h, g_attn[i], wfm, wtm, qn_swa[i], kn_swa[i], qn_diff[i], kn_diff[i],
            (n_qa, n_ka, n_va, n_qd, n_kd, n_vd))
        ya = _swa_attention(qa_t, ka, va_t, sinks[i], _alibi_slopes(swa_heads),
                            _logit_bound(qn_swa[i], kn_swa[i]))
        lam_params = jnp.stack([lambda_q1[i], lambda_k1[i], lambda_q2[i], lambda_k2[i]]).astype(F32)
        yd = _diff_attention(qd_t, kd, vd_t, lam_params, g_sub[i], _alibi_slopes(diff_heads), lam_init,
                             _logit_bound(qn_diff[i], kn_diff[i]))
        wo = w_out[i].astype(BF16)
        h2 = _outproj(h.reshape(b * s, d), ya.reshape(b * s, n_qa), yd.reshape(b * s, n_vd),
                      wo[:n_qa], wo[n_qa:])
        h2 = _ffn(h2, g_ffn[i], w_gate[i].astype(BF16), w_up[i].astype(BF16), w_down[i].astype(BF16))
        h2 = _ple(h2, p[i].reshape(b * s, -1), g_ple[i], w_ple_gate[i].astype(BF16),
                  w_ple_proj[i].astype(BF16), g_ple_out[i])
        h = h2.reshape(b, s, d)
    return h
```

```python
import functools
import math

import jax
import jax.numpy as jnp
import numpy as np
from jax import lax
from jax.experimental import pallas as pl
from jax.experimental.pallas import tpu as pltpu

F32 = jnp.float32
BF16 = jnp.bfloat16

HEAD_DIM = 64
SWA_BLOCK = 128
SWA_KV_HEADS = 2
SWA_GROUP = 8
DIFF_V_DIM = 2 * HEAD_DIM
EPS = 1e-6
NEG_INF = float("-inf")
LOG2E = math.log2(math.e)
MAX_UNSHIFTED_LOGIT = 60.0

TOKEN_TILE = 256
INPROJ_TILES_PER_STEP = 2
DIFF_HEADS_PER_STEP = 4
VMEM_LIMIT_BYTES = 56 * 1024 * 1024

_NT = (((1,), (1,)), ((), ()))


def _resident(shape, index_map):
    return pl.BlockSpec(shape, index_map, pipeline_mode=pl.Buffered(1))


def _rms_rows(x, gain):
    ms = jnp.mean(x * x, axis=-1, keepdims=True)
    return x * lax.rsqrt(ms + EPS) * gain


def _inproj_kernel(x_ref, g_ref, wfm_ref, wtm_ref, gqa_ref, gqd_ref, gka_ref, gkd_ref, bd_ref,
                   qa_ref, va_ref, qd_ref, vd_ref, ka_ref, kd_ref, *, n_qa, n_va, n_qd, n_vd, n_ka, n_kd):
    tm = TOKEN_TILE
    per_tile = tm // SWA_BLOCK

    def headnorm_fm(z, gain, store):
        for h in range(z.shape[0] // HEAD_DIM):
            zh = z[h * HEAD_DIM:(h + 1) * HEAD_DIM, :]
            ms = jnp.mean(zh * zh, axis=0, keepdims=True)
            store(h, (zh * lax.rsqrt(ms + EPS) * gain).astype(BF16))

    def headnorm_tm(z, gain):
        ncols = z.shape[1]
        ssq = jnp.dot((z * z).astype(BF16), bd_ref[0:ncols, 0:ncols], preferred_element_type=F32)
        return (z * lax.rsqrt(ssq * (1.0 / HEAD_DIM) + EPS) * gain).astype(BF16)

    for st in range(x_ref.shape[0] // tm):
        rows = slice(st * tm, (st + 1) * tm)
        u = _rms_rows(x_ref[rows, :], g_ref[...]).astype(BF16)

        def fm(row0, nrows):
            return lax.dot_general(wfm_ref[row0:row0 + nrows, :], u, _NT, preferred_element_type=F32)

        def store_qa(h, val):
            for t in range(per_tile):
                qa_ref[0, st * per_tile + t, h * HEAD_DIM:(h + 1) * HEAD_DIM, :] = (
                    val[:, t * SWA_BLOCK:(t + 1) * SWA_BLOCK])

        def store_qd(h, val):
            qd_ref[0, st, h * HEAD_DIM:(h + 1) * HEAD_DIM, :] = val

        r = 0
        headnorm_fm(fm(r, n_qa), gqa_ref[...], store_qa)
        r += n_qa
        zva = fm(r, n_va).astype(BF16)
        for t in range(per_tile):
            va_ref[0, st * per_tile + t] = zva[:, t * SWA_BLOCK:(t + 1) * SWA_BLOCK]
        r += n_va
        headnorm_fm(fm(r, n_qd), gqd_ref[...], store_qd)
        r += n_qd
        vd_ref[0, st] = fm(r, n_vd).astype(BF16)

        zk = jnp.dot(u, wtm_ref[...], preferred_element_type=F32)
        ka_ref[0, rows, :] = headnorm_tm(zk[:, :n_ka], gka_ref[...])
        cw = bd_ref.shape[0]
        for c in range(n_kd // cw):
            kd_ref[0, rows, c * cw:(c + 1) * cw] = headnorm_tm(
                zk[:, n_ka + c * cw:n_ka + (c + 1) * cw], gkd_ref[...])


def _inproj(x, g_attn, wfm, wtm, qn_swa, kn_swa, qn_diff, kn_diff, dims):
    b, s, d = x.shape
    n_qa, n_ka, n_va, n_qd, n_kd, n_vd = dims
    tm = TOKEN_TILE
    nt = s // tm
    tps = min(INPROJ_TILES_PER_STEP, nt)
    assert nt % tps == 0
    blk = tps * tm
    scale = LOG2E / math.sqrt(HEAD_DIM)
    gqa = jnp.broadcast_to((qn_swa * scale)[:, None], (HEAD_DIM, tm)).astype(F32)
    gqd = jnp.broadcast_to((qn_diff * scale)[:, None], (HEAD_DIM, tm)).astype(F32)
    gka = jnp.tile(kn_swa, n_ka // HEAD_DIM)[None, :].astype(F32)
    gkd = jnp.tile(kn_diff, 256 // HEAD_DIM)[None, :].astype(F32)
    idx = np.arange(256) // HEAD_DIM
    bd = jnp.asarray(idx[:, None] == idx[None, :], dtype=BF16)
    const = lambda shape: _resident(shape, lambda bi, ti: (0,) * len(shape))
    kern = functools.partial(_inproj_kernel, n_qa=n_qa, n_va=n_va, n_qd=n_qd, n_vd=n_vd, n_ka=n_ka, n_kd=n_kd)
    return pl.pallas_call(
        kern,
        grid=(b, nt // tps),
        in_specs=[
            pl.BlockSpec((None, blk, d), lambda bi, ti: (bi, ti, 0)),
            const((1, d)), const(wfm.shape), const(wtm.shape),
            const((HEAD_DIM, tm)), const((HEAD_DIM, tm)), const((1, n_ka)), const((1, 256)), const((256, 256)),
        ],
        out_specs=[
            pl.BlockSpec((1, blk // SWA_BLOCK, n_qa, SWA_BLOCK), lambda bi, ti: (bi, ti, 0, 0)),
            pl.BlockSpec((1, blk // SWA_BLOCK, n_va, SWA_BLOCK), lambda bi, ti: (bi, ti, 0, 0)),
            pl.BlockSpec((1, tps, n_qd, tm), lambda bi, ti: (bi, ti, 0, 0)),
            pl.BlockSpec((1, tps, n_vd, tm), lambda bi, ti: (bi, ti, 0, 0)),
            pl.BlockSpec((1, blk, n_ka), lambda bi, ti: (bi, ti, 0)),
            pl.BlockSpec((1, blk, n_kd), lambda bi, ti: (bi, ti, 0)),
        ],
        out_shape=[
            jax.ShapeDtypeStruct((b, s // SWA_BLOCK, n_qa, SWA_BLOCK), BF16),
            jax.ShapeDtypeStruct((b, s // SWA_BLOCK, n_va, SWA_BLOCK), BF16),
            jax.ShapeDtypeStruct((b, nt, n_qd, tm), BF16),
            jax.ShapeDtypeStruct((b, nt, n_vd, tm), BF16),
            jax.ShapeDtypeStruct((b, s, n_ka), BF16),
            jax.ShapeDtypeStruct((b, s, n_kd), BF16),
        ],
        compiler_params=pltpu.CompilerParams(
            dimension_semantics=("parallel", "parallel"), vmem_limit_bytes=VMEM_LIMIT_BYTES),
        name="inproj",
    )(x, g_attn[None, :], wfm, wtm, gqa, gqd, gka, gkd, bd)


def _staged_pipeline(n_steps, stages):
    depth = len(stages)

    def iteration(i, parity):
        for k in reversed(range(depth)):
            t = i - k
            if isinstance(i, int) and not 0 <= t < n_steps:
                continue
            stages[k](t, (parity - k) % 2)

    fill_end = min(depth - 1, n_steps)
    for i in range(fill_end):
        iteration(i, i % 2)
    n_pairs = (n_steps - fill_end) // 2

    def pair(m, carry):
        i = fill_end + 2 * m
        iteration(i, fill_end % 2)
        iteration(i + 1, (fill_end + 1) % 2)
        return carry

    lax.fori_loop(0, n_pairs, pair, 0)
    for i in range(fill_end + 2 * n_pairs, n_steps + depth - 1):
        iteration(i, i % 2)


def _swa_kernel(bounded_ref, q_ref, k_ref, v_ref, bias_ref, sink_ref, o_ref, s_ref, p_ref, d_ref):
    hk = pl.program_id(1)
    nblk = q_ref.shape[1]
    w = SWA_BLOCK
    sink = sink_ref[0]

    def window_start(n):
        return jnp.maximum(n - 1, 0)

    def scores(n, slot):
        qblk = q_ref[0, n]
        qg = jnp.concatenate([qblk[g * HEAD_DIM:(g + 1) * HEAD_DIM, :] for g in range(SWA_GROUP)], axis=1)
        zero = jnp.zeros_like(qg)
        qpad = jnp.concatenate([jnp.where(hk == 0, qg, zero), jnp.where(hk == 1, qg, zero)], axis=0)
        kwin = k_ref[0, pl.ds(pl.multiple_of(window_start(n) * w, w), 2 * w), :]
        s_ref[slot] = jnp.dot(kwin, qpad, preferred_element_type=F32)

    def probs(n, slot, *, bounded):
        t = s_ref[slot] + bias_ref[jnp.minimum(n, 1)]
        if bounded:
            e = jnp.exp2(t)
            d_ref[slot] = jnp.sum(e, axis=0, keepdims=True) + jnp.exp2(sink)
        else:
            m = jnp.maximum(jnp.max(t, axis=0, keepdims=True), sink)
            e = jnp.exp2(t - m)
            d_ref[slot] = jnp.sum(e, axis=0, keepdims=True) + jnp.exp2(sink - m)
        p_ref[slot] = e.astype(BF16)

    def weighted_values(n, slot):
        first = window_start(n)
        vwin = jnp.concatenate([v_ref[0, first], v_ref[0, first + 1]], axis=1)
        o = jnp.dot(vwin, p_ref[slot], preferred_element_type=F32) * (1.0 / d_ref[slot])
        for gp in range(SWA_GROUP // 2):
            pair = jnp.concatenate([o[:, (2 * gp) * w:(2 * gp + 1) * w],
                                    o[:, (2 * gp + 1) * w:(2 * gp + 2) * w]], axis=0)
            o_ref[0, pl.ds(pl.multiple_of(n * w, w), w), gp * 2 * HEAD_DIM:(gp + 1) * 2 * HEAD_DIM] = (
                pair.T.astype(BF16))

    @pl.when(bounded_ref[0] != 0)
    def _():
        _staged_pipeline(nblk, [scores, functools.partial(probs, bounded=True), weighted_values])

    @pl.when(bounded_ref[0] == 0)
    def _():
        def block(n, carry):
            scores(n, 0)
            probs(n, 0, bounded=False)
            weighted_values(n, 0)
            return carry

        lax.fori_loop(0, nblk, block, 0)


def _swa_bias(slopes):
    w = SWA_BLOCK
    kj = np.arange(2 * w)[:, None]
    qi = np.arange(w)[None, :]
    out = np.empty((2, SWA_KV_HEADS, 2 * w, SWA_GROUP * w), np.float32)
    for first, dist in ((0, qi - kj), (1, qi - kj + w)):
        valid = (dist >= 0) & (dist < w)
        for hk in range(SWA_KV_HEADS):
            for g in range(SWA_GROUP):
                sl = slopes[hk * SWA_GROUP + g]
                out[first, hk, :, g * w:(g + 1) * w] = np.where(valid, -sl * dist, -np.inf)
    return out


def _swa_attention(qa_t, ka, va_t, sinks, slopes, logit_bound):
    b, nblk, n_qa, w = qa_t.shape
    s = nblk * w
    gw = SWA_GROUP * w
    bias = jnp.asarray(_swa_bias(slopes) * LOG2E)
    sinks = sinks.astype(F32)
    sink_cols = jnp.repeat((sinks * LOG2E).reshape(SWA_KV_HEADS, 1, SWA_GROUP), w, axis=-1)
    bounded = jnp.asarray(jnp.maximum(logit_bound, jnp.max(jnp.abs(sinks))) <= MAX_UNSHIFTED_LOGIT, jnp.int32)
    return pl.pallas_call(
        _swa_kernel,
        grid=(b, SWA_KV_HEADS),
        in_specs=[
            pl.BlockSpec(memory_space=pltpu.SMEM),
            pl.BlockSpec((1, nblk, SWA_GROUP * HEAD_DIM, w), lambda bi, hk: (bi, 0, hk, 0)),
            pl.BlockSpec((1, s, SWA_KV_HEADS * HEAD_DIM), lambda bi, hk: (bi, 0, 0)),
            pl.BlockSpec((1, nblk, HEAD_DIM, w), lambda bi, hk: (bi, 0, hk, 0)),
            pl.BlockSpec((2, None, 2 * w, gw), lambda bi, hk: (0, hk, 0, 0)),
            pl.BlockSpec((1, 1, gw), lambda bi, hk: (hk, 0, 0)),
        ],
        out_specs=pl.BlockSpec((1, s, SWA_GROUP * HEAD_DIM), lambda bi, hk: (bi, 0, hk)),
        out_shape=jax.ShapeDtypeStruct((b, s, n_qa), BF16),
        scratch_shapes=[pltpu.VMEM((2, 2 * w, gw), F32), pltpu.VMEM((2, 2 * w, gw), BF16),
                        pltpu.VMEM((2, 1, gw), F32)],
        compiler_params=pltpu.CompilerParams(
            dimension_semantics=("parallel", "parallel"), vmem_limit_bytes=VMEM_LIMIT_BYTES),
        name="swa_attn",
    )(bounded.reshape(1), qa_t, ka, va_t, bias, sink_cols)


def _diff_kernel(bounded_ref, slope_ref, iq_tbl_ref, j_tbl_ref, q_ref, k_ref, v_ref, bias_ref, lam_ref, gsub_ref,
                 o_ref, acc_ref, l_ref, p_ref, *, lam_init):
    hg = pl.program_id(1)
    nt, t_q = q_ref.shape[1], q_ref.shape[3]
    hpb = acc_ref.shape[0]
    n_steps = nt * (nt + 1) // 2
    dv = DIFF_V_DIM
    lp = lam_ref[...]
    lam = (jnp.exp(jnp.sum(lp[0:1] * lp[1:2], axis=-1, keepdims=True))
           - jnp.exp(jnp.sum(lp[2:3] * lp[3:4], axis=-1, keepdims=True)) + lam_init)
    zero = jnp.zeros((HEAD_DIM, t_q), BF16)

    def padded_q(iq, hh):
        q = q_ref[0, iq, hh * dv:(hh + 1) * dv, :]
        return jnp.concatenate([jnp.concatenate([q[:HEAD_DIM], zero], axis=0),
                                jnp.concatenate([zero, q[HEAD_DIM:]], axis=0)], axis=1)

    def logits(iq, j, hh, qb):
        kblk = k_ref[0, pl.ds(pl.multiple_of(j * t_q, t_q), t_q), hh * dv:(hh + 1) * dv]
        diag = jnp.where(j == iq, 1, 0)
        cj = slope_ref[hg * hpb + hh] * ((j - iq) * t_q).astype(F32)
        return jnp.dot(kblk, qb, preferred_element_type=F32) + bias_ref[hh, diag], cj

    def finalize(iq, hh, acc, l):
        acc = acc * (1.0 / l)
        od = acc[:, :t_q] - lam * acc[:, t_q:]
        ms = jnp.mean(od * od, axis=0, keepdims=True)
        y = od * lax.rsqrt(ms + EPS) * gsub_ref[...]
        o_ref[0, pl.ds(pl.multiple_of(iq * t_q, t_q), t_q), hh * dv:(hh + 1) * dv] = y.T.astype(BF16)

    def probs(t, slot):
        iq, j = iq_tbl_ref[t], j_tbl_ref[t]
        for hh in range(hpb):
            s, cj = logits(iq, j, hh, padded_q(iq, hh))
            p = jnp.exp2(s + cj)
            l_ref[hh, iq] += jnp.sum(p.reshape(t_q // 8, 8, 2 * t_q), axis=0)
            p_ref[slot, hh] = p.astype(BF16)

    def pv(t, slot):
        iq, j = iq_tbl_ref[t], j_tbl_ref[t]
        for hh in range(hpb):
            acc_ref[hh, iq] += jnp.dot(v_ref[0, j, hh * dv:(hh + 1) * dv, :], p_ref[slot, hh],
                                       preferred_element_type=F32)

    @pl.when(bounded_ref[0] != 0)
    def _():
        acc_ref[...] = jnp.zeros_like(acc_ref)
        l_ref[...] = jnp.zeros_like(l_ref)
        _staged_pipeline(n_steps, [probs, pv])

        def fin(iq, carry):
            for hh in range(hpb):
                finalize(iq, hh, acc_ref[hh, iq], jnp.sum(l_ref[hh, iq], axis=0, keepdims=True))
            return carry

        lax.fori_loop(0, nt, fin, 0)

    @pl.when(bounded_ref[0] == 0)
    def _():
        def q_block(iq, carry):
            qbs = [padded_q(iq, hh) for hh in range(hpb)]
            acc_ref[:, 0] = jnp.zeros((hpb, dv, 2 * t_q), F32)

            def step(j, stats):
                out = []
                for hh in range(hpb):
                    m, l = stats[2 * hh], stats[2 * hh + 1]
                    s, cj = logits(iq, j, hh, qbs[hh])
                    mnew = jnp.maximum(m, jnp.max(s, axis=0, keepdims=True) + cj)
                    alpha = jnp.exp2(m - mnew)
                    p = jnp.exp2(s + (cj - mnew))
                    out += [mnew, alpha * l + jnp.sum(p, axis=0, keepdims=True)]
                    acc_ref[hh, 0] = alpha * acc_ref[hh, 0] + jnp.dot(
                        v_ref[0, j, hh * dv:(hh + 1) * dv, :], p.astype(BF16), preferred_element_type=F32)
                return tuple(out)

            init = (jnp.full((1, 2 * t_q), NEG_INF, F32), jnp.zeros((1, 2 * t_q), F32)) * hpb
            stats = lax.fori_loop(0, iq + 1, step, init)
            for hh in range(hpb):
                finalize(iq, hh, acc_ref[hh, 0], stats[2 * hh + 1])
            return carry

        lax.fori_loop(0, nt, q_block, 0)


def _diff_bias(slopes, t):
    kj = np.arange(t)[:, None]
    qi = np.arange(2 * t)[None, :] % t
    rel = (kj - qi).astype(np.float32)
    out = np.empty((len(slopes), 2, t, 2 * t), np.float32)
    for h, sl in enumerate(slopes):
        out[h, 0] = sl * rel
        out[h, 1] = np.where(kj <= qi, sl * rel, -np.inf)
    return out


def _diff_attention(qd_t, kd, vd_t, lam_params, g_sub, slopes, lam_init, logit_bound, hpb=DIFF_HEADS_PER_STEP):
    b, nt, n_qd, t = qd_t.shape
    s = nt * t
    nh = n_qd // DIFF_V_DIM
    bias = jnp.asarray(_diff_bias(slopes, t) * LOG2E)
    gsub = jnp.broadcast_to((g_sub * (1.0 - lam_init))[:, None], (DIFF_V_DIM, t)).astype(F32)
    bounded = jnp.asarray(logit_bound <= MAX_UNSHIFTED_LOGIT, jnp.int32)
    pairs = [(iq, j) for iq in range(nt) for j in range(iq + 1)]
    iq_tbl = jnp.asarray([pq[0] for pq in pairs], jnp.int32)
    j_tbl = jnp.asarray([pq[1] for pq in pairs], jnp.int32)
    kern = functools.partial(_diff_kernel, lam_init=lam_init)
    return pl.pallas_call(
        kern,
        grid=(b, nh // hpb),
        in_specs=[
            pl.BlockSpec(memory_space=pltpu.SMEM),
            pl.BlockSpec(memory_space=pltpu.SMEM),
            pl.BlockSpec(memory_space=pltpu.SMEM),
            pl.BlockSpec(memory_space=pltpu.SMEM),
            pl.BlockSpec((1, nt, hpb * DIFF_V_DIM, t), lambda bi, hg: (bi, 0, hg, 0)),
            pl.BlockSpec((1, s, hpb * DIFF_V_DIM), lambda bi, hg: (bi, 0, hg)),
            pl.BlockSpec((1, nt, hpb * DIFF_V_DIM, t), lambda bi, hg: (bi, 0, hg, 0)),
            pl.BlockSpec((hpb, 2, t, 2 * t), lambda bi, hg: (hg, 0, 0, 0)),
            pl.BlockSpec((4, HEAD_DIM), lambda bi, hg: (0, 0)),
            pl.BlockSpec((DIFF_V_DIM, t), lambda bi, hg: (0, 0)),
        ],
        out_specs=pl.BlockSpec((1, s, hpb * DIFF_V_DIM), lambda bi, hg: (bi, 0, hg)),
        out_shape=jax.ShapeDtypeStruct((b, s, n_qd), BF16),
        scratch_shapes=[pltpu.VMEM((hpb, nt, DIFF_V_DIM, 2 * t), F32),
                        pltpu.VMEM((hpb, nt, 8, 2 * t), F32),
                        pltpu.VMEM((2, hpb, t, 2 * t), BF16)],
        compiler_params=pltpu.CompilerParams(
            dimension_semantics=("parallel", "parallel"), vmem_limit_bytes=VMEM_LIMIT_BYTES),
        name="diff_attn",
    )(bounded.reshape(1), jnp.asarray(np.asarray(slopes) * LOG2E, F32), iq_tbl, j_tbl,
      qd_t, kd, vd_t, bias, lam_params, gsub)


def _outproj_kernel(x_ref, ya_ref, yd_ref, wa_ref, wd_ref, o_ref):
    o_ref[...] = (x_ref[...]
                  + jnp.dot(ya_ref[...], wa_ref[...], preferred_element_type=F32)
                  + jnp.dot(yd_ref[...], wd_ref[...], preferred_element_type=F32))


def _outproj(x2, ya2, yd2, wa, wd, tm=512):
    n, d = x2.shape
    tm = min(tm, n)
    assert n % tm == 0
    const = lambda shape: _resident(shape, lambda i: (0, 0))
    return pl.pallas_call(
        _outproj_kernel,
        grid=(n // tm,),
        in_specs=[
            pl.BlockSpec((tm, d), lambda i: (i, 0)),
            pl.BlockSpec((tm, ya2.shape[1]), lambda i: (i, 0)),
            pl.BlockSpec((tm, yd2.shape[1]), lambda i: (i, 0)),
            const(wa.shape), const(wd.shape),
        ],
        out_specs=pl.BlockSpec((tm, d), lambda i: (i, 0)),
        out_shape=jax.ShapeDtypeStruct((n, d), F32),
        compiler_params=pltpu.CompilerParams(
            dimension_semantics=("parallel",), vmem_limit_bytes=VMEM_LIMIT_BYTES),
        name="outproj",
    )(x2, ya2, yd2, wa, wd)


def _ffn_kernel(h_ref, g_ref, wg_ref, wu_ref, wd_ref, o_ref, u_ref):
    f = pl.program_id(1)

    @pl.when(f == 0)
    def _():
        h = h_ref[...]
        u_ref[...] = _rms_rows(h, g_ref[...]).astype(BF16)
        o_ref[...] = h

    u = u_ref[...]
    gate = jnp.dot(u, wg_ref[...], preferred_element_type=F32)
    up = jnp.dot(u, wu_ref[...], preferred_element_type=F32)
    act = (gate * (1.0 / (1.0 + jnp.exp(-gate))) * up).astype(BF16)
    o_ref[...] += jnp.dot(act, wd_ref[...], preferred_element_type=F32)


def _ffn(h2, g_ffn, wg, wu, wd, tm=1024, tf=512):
    n, d = h2.shape
    tm = min(tm, n)
    assert n % tm == 0
    dff = wg.shape[1]
    return pl.pallas_call(
        _ffn_kernel,
        grid=(n // tm, dff // tf),
        in_specs=[
            pl.BlockSpec((tm, d), lambda i, f: (i, 0)),
            _resident((1, d), lambda i, f: (0, 0)),
            pl.BlockSpec((d, tf), lambda i, f: (0, f)),
            pl.BlockSpec((d, tf), lambda i, f: (0, f)),
            pl.BlockSpec((tf, d), lambda i, f: (f, 0)),
        ],
        out_specs=pl.BlockSpec((tm, d), lambda i, f: (i, 0)),
        out_shape=jax.ShapeDtypeStruct((n, d), F32),
        scratch_shapes=[pltpu.VMEM((tm, d), BF16)],
        compiler_params=pltpu.CompilerParams(
            dimension_semantics=("parallel", "arbitrary"), vmem_limit_bytes=VMEM_LIMIT_BYTES),
        name="ffn",
    )(h2, g_ffn[None, :], wg, wu, wd)


def _ple_kernel(h_ref, p_ref, g_ref, wg_ref, wp_ref, go_ref, o_ref, *, sub, ncol):
    d = h_ref.shape[1]
    for st in range(h_ref.shape[0] // sub):
        rows = slice(st * sub, (st + 1) * sub)
        u = _rms_rows(h_ref[rows, :], g_ref[...]).astype(BF16)
        pp = jnp.dot(p_ref[rows, :].astype(BF16), wp_ref[...], preferred_element_type=F32)
        ppn = _rms_rows(pp, go_ref[...])
        for c in range(d // ncol):
            cols = slice(c * ncol, (c + 1) * ncol)
            z = jnp.dot(u, wg_ref[:, cols], preferred_element_type=F32)
            gate = 1.0 / (1.0 + jnp.exp(-z))
            o_ref[rows, cols] = h_ref[rows, cols] + gate * ppn[:, cols]


def _ple(h2, p2, g_ple, wg, wp, g_out, tm=1024, sub=512, ncol=512):
    n, d = h2.shape
    tm = min(tm, n)
    assert n % tm == 0
    sub = min(sub, tm)
    assert tm % sub == 0
    const = lambda shape: _resident(shape, lambda i: (0, 0))
    return pl.pallas_call(
        functools.partial(_ple_kernel, sub=sub, ncol=min(ncol, d)),
        grid=(n // tm,),
        in_specs=[
            pl.BlockSpec((tm, d), lambda i: (i, 0)),
            pl.BlockSpec((tm, p2.shape[1]), lambda i: (i, 0)),
            const((1, d)), const(wg.shape), const(wp.shape), const((1, d)),
        ],
        out_specs=pl.BlockSpec((tm, d), lambda i: (i, 0)),
        out_shape=jax.ShapeDtypeStruct((n, d), F32),
        compiler_params=pltpu.CompilerParams(
            dimension_semantics=("parallel",), vmem_limit_bytes=VMEM_LIMIT_BYTES),
        name="ple",
    )(h2, p2, g_ple[None, :], wg, wp, g_out[None, :])


def _alibi_slopes(n):
    return [2.0 ** (-8.0 * (h + 1) / n) for h in range(n)]


def _logit_bound(q_gain, k_gain):
    return 1.02 * math.sqrt(HEAD_DIM) * jnp.max(jnp.abs(q_gain)) * jnp.max(jnp.abs(k_gain))


def kernel(x, p, g_attn, w_in, qn_swa, kn_swa, sinks, qn_diff, kn_diff, lambda_q1, lambda_k1, lambda_q2,
           lambda_k2, g_sub, w_out, g_ffn, w_gate, w_up, w_down, g_ple, w_ple_gate, w_ple_proj, g_ple_out):
    b, s, d = x.shape
    depth = p.shape[0]
    n_qa = d // 2
    n_ka = n_va = SWA_KV_HEADS * HEAD_DIM
    n_qd = n_kd = n_vd = d // 2
    diff_heads = n_vd // DIFF_V_DIM
    swa_heads = n_qa // HEAD_DIM
    assert swa_heads == SWA_KV_HEADS * SWA_GROUP and s % TOKEN_TILE == 0
    c = np.cumsum([0, n_qa, n_ka, n_va, n_qd, n_kd, n_vd])
    h = x
    for i in range(depth):
        lam_init = 0.8 - 0.6 * math.exp(-0.3 * i)
        w = w_in[i]
        col = lambda k: w[:, c[k]:c[k + 1]]
        wfm = jnp.concatenate([col(0), col(2), col(3), col(5)], axis=1).T.astype(BF16)
        wtm = jnp.concatenate([col(1), col(4)], axis=1).astype(BF16)
        qa_t, va_t, qd_t, vd_t, ka, kd = _inproj(
            h, g_attn[i], wfm, wtm, qn_swa[i], kn_swa[i], qn_diff[i], kn_diff[i],
            (n_qa, n_ka, n_va, n_qd, n_kd, n_vd))
        ya = _swa_attention(qa_t, ka, va_t, sinks[i], _alibi_slopes(swa_heads),
                            _logit_bound(qn_swa[i], kn_swa[i]))
        lam_params = jnp.stack([lambda_q1[i], lambda_k1[i], lambda_q2[i], lambda_k2[i]]).astype(F32)
        yd = _diff_attention(qd_t, kd, vd_t, lam_params, g_sub[i], _alibi_slopes(diff_heads), lam_init,
                             _logit_bound(qn_diff[i], kn_diff[i]))
        wo = w_out[i].astype(BF16)
        h2 = _outproj(h.reshape(b * s, d), ya.reshape(b * s, n_qa), yd.reshape(b * s, n_vd),
                      wo[:n_qa], wo[n_qa:])
        h2 = _ffn(h2, g_ffn[i], w_gate[i].astype(BF16), w_up[i].astype(BF16), w_down[i].astype(BF16))
        h2 = _ple(h2, p[i].reshape(b * s, -1), g_ple[i], w_ple_gate[i].astype(BF16),
                  w_ple_proj[i].astype(BF16), g_ple_out[i])
        h = h2.reshape(b, s, d)
    return h
```

```python
import functools
import math

import jax
import jax.numpy as jnp
import numpy as np
from jax import lax
from jax.experimental import pallas as pl
from jax.experimental.pallas import tpu as pltpu

F32 = jnp.float32
BF16 = jnp.bfloat16

HEAD_DIM = 64
SWA_BLOCK = 128
SWA_KV_HEADS = 2
SWA_GROUP = 8
DIFF_V_DIM = 2 * HEAD_DIM
EPS = 1e-6
NEG_INF = float("-inf")
LOG2E = math.log2(math.e)
MAX_UNSHIFTED_LOGIT = 60.0

TOKEN_TILE = 256
INPROJ_TILES_PER_STEP = 4
DIFF_HEADS_PER_STEP = 4
VMEM_LIMIT_BYTES = 56 * 1024 * 1024

_NT = (((1,), (1,)), ((), ()))


def _resident(shape, index_map):
    return pl.BlockSpec(shape, index_map, pipeline_mode=pl.Buffered(1))


def _rms_rows(x, gain):
    ms = jnp.mean(x * x, axis=-1, keepdims=True)
    return x * lax.rsqrt(ms + EPS) * gain


def _inproj_kernel(x_ref, g_ref, wfm_ref, wtm_ref, gqa_ref, gqd_ref, gka_ref, gkd_ref, bd_ref,
                   qa_ref, va_ref, qd_ref, vd_ref, ka_ref, kd_ref, *, n_qa, n_va, n_qd, n_vd, n_ka, n_kd):
    tm = TOKEN_TILE
    per_tile = tm // SWA_BLOCK

    def headnorm_fm(z, gain, store):
        for h in range(z.shape[0] // HEAD_DIM):
            zh = z[h * HEAD_DIM:(h + 1) * HEAD_DIM, :]
            ms = jnp.mean(zh * zh, axis=0, keepdims=True)
            store(h, (zh * lax.rsqrt(ms + EPS) * gain).astype(BF16))

    def headnorm_tm(z, gain):
        ncols = z.shape[1]
        ssq = jnp.dot((z * z).astype(BF16), bd_ref[0:ncols, 0:ncols], preferred_element_type=F32)
        return (z * lax.rsqrt(ssq * (1.0 / HEAD_DIM) + EPS) * gain).astype(BF16)

    for st in range(x_ref.shape[0] // tm):
        rows = slice(st * tm, (st + 1) * tm)
        u = _rms_rows(x_ref[rows, :], g_ref[...]).astype(BF16)

        def fm(row0, nrows):
            return lax.dot_general(wfm_ref[row0:row0 + nrows, :], u, _NT, preferred_element_type=F32)

        def store_qa(h, val):
            for t in range(per_tile):
                qa_ref[0, st * per_tile + t, h * HEAD_DIM:(h + 1) * HEAD_DIM, :] = (
                    val[:, t * SWA_BLOCK:(t + 1) * SWA_BLOCK])

        def store_qd(h, val):
            qd_ref[0, st, h * HEAD_DIM:(h + 1) * HEAD_DIM, :] = val

        r = 0
        headnorm_fm(fm(r, n_qa), gqa_ref[...], store_qa)
        r += n_qa
        zva = fm(r, n_va).astype(BF16)
        for t in range(per_tile):
            va_ref[0, st * per_tile + t] = zva[:, t * SWA_BLOCK:(t + 1) * SWA_BLOCK]
        r += n_va
        headnorm_fm(fm(r, n_qd), gqd_ref[...], store_qd)
        r += n_qd
        vd_ref[0, st] = fm(r, n_vd).astype(BF16)

        zk = jnp.dot(u, wtm_ref[...], preferred_element_type=F32)
        ka_ref[0, rows, :] = headnorm_tm(zk[:, :n_ka], gka_ref[...])
        cw = bd_ref.shape[0]
        for c in range(n_kd // cw):
            kd_ref[0, rows, c * cw:(c + 1) * cw] = headnorm_tm(
                zk[:, n_ka + c * cw:n_ka + (c + 1) * cw], gkd_ref[...])


def _inproj(x, g_attn, wfm, wtm, qn_swa, kn_swa, qn_diff, kn_diff, dims):
    b, s, d = x.shape
    n_qa, n_ka, n_va, n_qd, n_kd, n_vd = dims
    tm = TOKEN_TILE
    nt = s // tm
    tps = min(INPROJ_TILES_PER_STEP, nt)
    assert nt % tps == 0
    blk = tps * tm
    scale = LOG2E / math.sqrt(HEAD_DIM)
    gqa = jnp.broadcast_to((qn_swa * scale)[:, None], (HEAD_DIM, tm)).astype(F32)
    gqd = jnp.broadcast_to((qn_diff * scale)[:, None], (HEAD_DIM, tm)).astype(F32)
    gka = jnp.tile(kn_swa, n_ka // HEAD_DIM)[None, :].astype(F32)
    gkd = jnp.tile(kn_diff, 256 // HEAD_DIM)[None, :].astype(F32)
    idx = np.arange(256) // HEAD_DIM
    bd = jnp.asarray(idx[:, None] == idx[None, :], dtype=BF16)
    const = lambda shape: _resident(shape, lambda bi, ti: (0,) * len(shape))
    kern = functools.partial(_inproj_kernel, n_qa=n_qa, n_va=n_va, n_qd=n_qd, n_vd=n_vd, n_ka=n_ka, n_kd=n_kd)
    return pl.pallas_call(
        kern,
        grid=(b, nt // tps),
        in_specs=[
            pl.BlockSpec((None, blk, d), lambda bi, ti: (bi, ti, 0)),
            const((1, d)), const(wfm.shape), const(wtm.shape),
            const((HEAD_DIM, tm)), const((HEAD_DIM, tm)), const((1, n_ka)), const((1, 256)), const((256, 256)),
        ],
        out_specs=[
            pl.BlockSpec((1, blk // SWA_BLOCK, n_qa, SWA_BLOCK), lambda bi, ti: (bi, ti, 0, 0)),
            pl.BlockSpec((1, blk // SWA_BLOCK, n_va, SWA_BLOCK), lambda bi, ti: (bi, ti, 0, 0)),
            pl.BlockSpec((1, tps, n_qd, tm), lambda bi, ti: (bi, ti, 0, 0)),
            pl.BlockSpec((1, tps, n_vd, tm), lambda bi, ti: (bi, ti, 0, 0)),
            pl.BlockSpec((1, blk, n_ka), lambda bi, ti: (bi, ti, 0)),
            pl.BlockSpec((1, blk, n_kd), lambda bi, ti: (bi, ti, 0)),
        ],
        out_shape=[
            jax.ShapeDtypeStruct((b, s // SWA_BLOCK, n_qa, SWA_BLOCK), BF16),
            jax.ShapeDtypeStruct((b, s // SWA_BLOCK, n_va, SWA_BLOCK), BF16),
            jax.ShapeDtypeStruct((b, nt, n_qd, tm), BF16),
            jax.ShapeDtypeStruct((b, nt, n_vd, tm), BF16),
            jax.ShapeDtypeStruct((b, s, n_ka), BF16),
            jax.ShapeDtypeStruct((b, s, n_kd), BF16),
        ],
        compiler_params=pltpu.CompilerParams(
            dimension_semantics=("parallel", "parallel"), vmem_limit_bytes=VMEM_LIMIT_BYTES),
        name="inproj",
    )(x, g_attn[None, :], wfm, wtm, gqa, gqd, gka, gkd, bd)


def _staged_pipeline(n_steps, stages):
    depth = len(stages)

    def iteration(i, parity):
        for k in reversed(range(depth)):
            t = i - k
            if isinstance(i, int) and not 0 <= t < n_steps:
                continue
            stages[k](t, (parity - k) % 2)

    fill_end = min(depth - 1, n_steps)
    for i in range(fill_end):
        iteration(i, i % 2)
    n_pairs = (n_steps - fill_end) // 2

    def pair(m, carry):
        i = fill_end + 2 * m
        iteration(i, fill_end % 2)
        iteration(i + 1, (fill_end + 1) % 2)
        return carry

    lax.fori_loop(0, n_pairs, pair, 0)
    for i in range(fill_end + 2 * n_pairs, n_steps + depth - 1):
        iteration(i, i % 2)


def _swa_kernel(bounded_ref, q_ref, k_ref, v_ref, bias_ref, sink_ref, o_ref, s_ref, p_ref, d_ref):
    hk = pl.program_id(0)
    nblk = q_ref.shape[1]
    w = SWA_BLOCK
    sink = sink_ref[0]

    def window_start(n):
        return jnp.maximum(n - 1, 0)

    def scores(n, slot):
        qblk = q_ref[0, n]
        qg = jnp.concatenate([qblk[g * HEAD_DIM:(g + 1) * HEAD_DIM, :] for g in range(SWA_GROUP)], axis=1)
        zero = jnp.zeros_like(qg)
        qpad = jnp.concatenate([jnp.where(hk == 0, qg, zero), jnp.where(hk == 1, qg, zero)], axis=0)
        kwin = k_ref[0, pl.ds(pl.multiple_of(window_start(n) * w, w), 2 * w), :]
        s_ref[slot] = jnp.dot(kwin, qpad, preferred_element_type=F32)

    def probs(n, slot, *, bounded):
        t = s_ref[slot] + bias_ref[jnp.minimum(n, 1)]
        if bounded:
            e = jnp.exp2(t)
            d_ref[slot] = jnp.sum(e, axis=0, keepdims=True) + jnp.exp2(sink)
        else:
            m = jnp.maximum(jnp.max(t, axis=0, keepdims=True), sink)
            e = jnp.exp2(t - m)
            d_ref[slot] = jnp.sum(e, axis=0, keepdims=True) + jnp.exp2(sink - m)
        p_ref[slot] = e.astype(BF16)

    def weighted_values(n, slot):
        first = window_start(n)
        vwin = jnp.concatenate([v_ref[0, first], v_ref[0, first + 1]], axis=1)
        o = jnp.dot(vwin, p_ref[slot], preferred_element_type=F32) * (1.0 / d_ref[slot])
        for gp in range(SWA_GROUP // 2):
            pair = jnp.concatenate([o[:, (2 * gp) * w:(2 * gp + 1) * w],
                                    o[:, (2 * gp + 1) * w:(2 * gp + 2) * w]], axis=0)
            o_ref[0, pl.ds(pl.multiple_of(n * w, w), w), gp * 2 * HEAD_DIM:(gp + 1) * 2 * HEAD_DIM] = (
                pair.T.astype(BF16))

    @pl.when(bounded_ref[0] != 0)
    def _():
        _staged_pipeline(nblk, [scores, functools.partial(probs, bounded=True), weighted_values])

    @pl.when(bounded_ref[0] == 0)
    def _():
        def block(n, carry):
            scores(n, 0)
            probs(n, 0, bounded=False)
            weighted_values(n, 0)
            return carry

        lax.fori_loop(0, nblk, block, 0)


def _swa_bias(slopes):
    w = SWA_BLOCK
    kj = np.arange(2 * w)[:, None]
    qi = np.arange(w)[None, :]
    out = np.empty((2, SWA_KV_HEADS, 2 * w, SWA_GROUP * w), np.float32)
    for first, dist in ((0, qi - kj), (1, qi - kj + w)):
        valid = (dist >= 0) & (dist < w)
        for hk in range(SWA_KV_HEADS):
            for g in range(SWA_GROUP):
                sl = slopes[hk * SWA_GROUP + g]
                out[first, hk, :, g * w:(g + 1) * w] = np.where(valid, -sl * dist, -np.inf)
    return out


def _swa_attention(qa_t, ka, va_t, sinks, slopes, logit_bound):
    b, nblk, n_qa, w = qa_t.shape
    s = nblk * w
    gw = SWA_GROUP * w
    bias = jnp.asarray(_swa_bias(slopes) * LOG2E)
    sinks = sinks.astype(F32)
    sink_cols = jnp.repeat((sinks * LOG2E).reshape(SWA_KV_HEADS, 1, SWA_GROUP), w, axis=-1)
    bounded = jnp.asarray(jnp.maximum(logit_bound, jnp.max(jnp.abs(sinks))) <= MAX_UNSHIFTED_LOGIT, jnp.int32)
    return pl.pallas_call(
        _swa_kernel,
        grid=(SWA_KV_HEADS, b),
        in_specs=[
            pl.BlockSpec(memory_space=pltpu.SMEM),
            pl.BlockSpec((1, nblk, SWA_GROUP * HEAD_DIM, w), lambda hk, bi: (bi, 0, hk, 0)),
            pl.BlockSpec((1, s, SWA_KV_HEADS * HEAD_DIM), lambda hk, bi: (bi, 0, 0)),
            pl.BlockSpec((1, nblk, HEAD_DIM, w), lambda hk, bi: (bi, 0, hk, 0)),
            pl.BlockSpec((2, None, 2 * w, gw), lambda hk, bi: (0, hk, 0, 0)),
            pl.BlockSpec((1, 1, gw), lambda hk, bi: (hk, 0, 0)),
        ],
        out_specs=pl.BlockSpec((1, s, SWA_GROUP * HEAD_DIM), lambda hk, bi: (bi, 0, hk)),
        out_shape=jax.ShapeDtypeStruct((b, s, n_qa), BF16),
        scratch_shapes=[pltpu.VMEM((2, 2 * w, gw), F32), pltpu.VMEM((2, 2 * w, gw), BF16),
                        pltpu.VMEM((2, 1, gw), F32)],
        compiler_params=pltpu.CompilerParams(
            dimension_semantics=("parallel", "parallel"), vmem_limit_bytes=VMEM_LIMIT_BYTES),
        name="swa_attn",
    )(bounded.reshape(1), qa_t, ka, va_t, bias, sink_cols)


def _diff_kernel(bounded_ref, slope_ref, iq_tbl_ref, j_tbl_ref, q_ref, k_ref, v_ref, bias_ref, lam_ref, gsub_ref,
                 o_ref, acc_ref, l_ref, p_ref, *, lam_init):
    hg = pl.program_id(0)
    nt, t_q = q_ref.shape[1], q_ref.shape[3]
    hpb = acc_ref.shape[0]
    n_steps = nt * (nt + 1) // 2
    dv = DIFF_V_DIM
    lp = lam_ref[...]
    lam = (jnp.exp(jnp.sum(lp[0:1] * lp[1:2], axis=-1, keepdims=True))
           - jnp.exp(jnp.sum(lp[2:3] * lp[3:4], axis=-1, keepdims=True)) + lam_init)
    zero = jnp.zeros((HEAD_DIM, t_q), BF16)

    def padded_q(iq, hh):
        q = q_ref[0, iq, hh * dv:(hh + 1) * dv, :]
        return jnp.concatenate([jnp.concatenate([q[:HEAD_DIM], zero], axis=0),
                                jnp.concatenate([zero, q[HEAD_DIM:]], axis=0)], axis=1)

    def logits(iq, j, hh, qb):
        kblk = k_ref[0, pl.ds(pl.multiple_of(j * t_q, t_q), t_q), hh * dv:(hh + 1) * dv]
        diag = jnp.where(j == iq, 1, 0)
        cj = slope_ref[hg * hpb + hh] * jnp.asarray((j - iq) * t_q, F32)
        return jnp.dot(kblk, qb, preferred_element_type=F32) + bias_ref[hh, diag], cj

    def finalize(iq, hh, acc, l):
        acc = acc * (1.0 / l)
        od = acc[:, :t_q] - lam * acc[:, t_q:]
        ms = jnp.mean(od * od, axis=0, keepdims=True)
        y = od * lax.rsqrt(ms + EPS) * gsub_ref[...]
        o_ref[0, pl.ds(pl.multiple_of(iq * t_q, t_q), t_q), hh * dv:(hh + 1) * dv] = y.T.astype(BF16)

    def probs(t, slot):
        iq, j = iq_tbl_ref[t], j_tbl_ref[t]
        for hh in range(hpb):
            s, cj = logits(iq, j, hh, padded_q(iq, hh))
            p = jnp.exp2(s + cj)
            l_ref[hh, iq] += jnp.sum(p.reshape(t_q // 8, 8, 2 * t_q), axis=0)
            p_ref[slot, hh] = p.astype(BF16)

    def pv(t, slot):
        iq, j = iq_tbl_ref[t], j_tbl_ref[t]
        for hh in range(hpb):
            acc_ref[hh, iq] += jnp.dot(v_ref[0, j, hh * dv:(hh + 1) * dv, :], p_ref[slot, hh],
                                       preferred_element_type=F32)

    @pl.when(bounded_ref[0] != 0)
    def _():
        acc_ref[...] = jnp.zeros_like(acc_ref)
        l_ref[...] = jnp.zeros_like(l_ref)
        _staged_pipeline(n_steps, [probs, pv])

        def fin(iq, carry):
            for hh in range(hpb):
                finalize(iq, hh, acc_ref[hh, iq], jnp.sum(l_ref[hh, iq], axis=0, keepdims=True))
            return carry

        lax.fori_loop(0, nt, fin, 0)

    @pl.when(bounded_ref[0] == 0)
    def _():
        def q_block(iq, carry):
            qbs = [padded_q(iq, hh) for hh in range(hpb)]
            acc_ref[:, 0] = jnp.zeros((hpb, dv, 2 * t_q), F32)

            def step(j, stats):
                out = []
                for hh in range(hpb):
                    m, l = stats[2 * hh], stats[2 * hh + 1]
                    s, cj = logits(iq, j, hh, qbs[hh])
                    mnew = jnp.maximum(m, jnp.max(s, axis=0, keepdims=True) + cj)
                    alpha = jnp.exp2(m - mnew)
                    p = jnp.exp2(s + (cj - mnew))
                    out += [mnew, alpha * l + jnp.sum(p, axis=0, keepdims=True)]
                    acc_ref[hh, 0] = alpha * acc_ref[hh, 0] + jnp.dot(
                        v_ref[0, j, hh * dv:(hh + 1) * dv, :], p.astype(BF16), preferred_element_type=F32)
                return tuple(out)

            init = (jnp.full((1, 2 * t_q), NEG_INF, F32), jnp.zeros((1, 2 * t_q), F32)) * hpb
            stats = lax.fori_loop(0, iq + 1, step, init)
            for hh in range(hpb):
                finalize(iq, hh, acc_ref[hh, 0], stats[2 * hh + 1])
            return carry

        lax.fori_loop(0, nt, q_block, 0)


def _diff_bias(slopes, t):
    kj = np.arange(t)[:, None]
    qi = np.arange(2 * t)[None, :] % t
    rel = (kj - qi).astype(np.float32)
    out = np.empty((len(slopes), 2, t, 2 * t), np.float32)
    for h, sl in enumerate(slopes):
        out[h, 0] = sl * rel
        out[h, 1] = np.where(kj <= qi, sl * rel, -np.inf)
    return out


def _diff_attention(qd_t, kd, vd_t, lam_params, g_sub, slopes, lam_init, logit_bound, hpb=DIFF_HEADS_PER_STEP):
    b, nt, n_qd, t = qd_t.shape
    s = nt * t
    nh = n_qd // DIFF_V_DIM
    bias = jnp.asarray(_diff_bias(slopes, t) * LOG2E)
    gsub = jnp.broadcast_to((g_sub * (1.0 - lam_init))[:, None], (DIFF_V_DIM, t)).astype(F32)
    bounded = jnp.asarray(logit_bound <= MAX_UNSHIFTED_LOGIT, jnp.int32)
    pairs = [(iq, j) for iq in range(nt) for j in range(iq + 1)]
    iq_tbl = jnp.asarray([pq[0] for pq in pairs], jnp.int32)
    j_tbl = jnp.asarray([pq[1] for pq in pairs], jnp.int32)
    kern = functools.partial(_diff_kernel, lam_init=lam_init)
    return pl.pallas_call(
        kern,
        grid=(nh // hpb, b),
        in_specs=[
            pl.BlockSpec(memory_space=pltpu.SMEM),
            pl.BlockSpec(memory_space=pltpu.SMEM),
            pl.BlockSpec(memory_space=pltpu.SMEM),
            pl.BlockSpec(memory_space=pltpu.SMEM),
            pl.BlockSpec((1, nt, hpb * DIFF_V_DIM, t), lambda hg, bi: (bi, 0, hg, 0)),
            pl.BlockSpec((1, s, hpb * DIFF_V_DIM), lambda hg, bi: (bi, 0, hg)),
            pl.BlockSpec((1, nt, hpb * DIFF_V_DIM, t), lambda hg, bi: (bi, 0, hg, 0)),
            pl.BlockSpec((hpb, 2, t, 2 * t), lambda hg, bi: (hg, 0, 0, 0)),
            pl.BlockSpec((4, HEAD_DIM), lambda hg, bi: (0, 0)),
            pl.BlockSpec((DIFF_V_DIM, t), lambda hg, bi: (0, 0)),
        ],
        out_specs=pl.BlockSpec((1, s, hpb * DIFF_V_DIM), lambda hg, bi: (bi, 0, hg)),
        out_shape=jax.ShapeDtypeStruct((b, s, n_qd), BF16),
        scratch_shapes=[pltpu.VMEM((hpb, nt, DIFF_V_DIM, 2 * t), F32),
                        pltpu.VMEM((hpb, nt, 8, 2 * t), F32),
                        pltpu.VMEM((2, hpb, t, 2 * t), BF16)],
        compiler_params=pltpu.CompilerParams(
            dimension_semantics=("parallel", "parallel"), vmem_limit_bytes=VMEM_LIMIT_BYTES),
        name="diff_attn",
    )(bounded.reshape(1), jnp.asarray(np.asarray(slopes) * LOG2E, F32), iq_tbl, j_tbl,
      qd_t, kd, vd_t, bias, lam_params, gsub)


def _outproj_kernel(x_ref, ya_ref, yd_ref, wa_ref, wd_ref, o_ref):
    o_ref[...] = (x_ref[...]
                  + jnp.dot(ya_ref[...], wa_ref[...], preferred_element_type=F32)
                  + jnp.dot(yd_ref[...], wd_ref[...], preferred_element_type=F32))


def _outproj(x2, ya2, yd2, wa, wd, tm=512):
    n, d = x2.shape
    tm = min(tm, n)
    assert n % tm == 0
    const = lambda shape: _resident(shape, lambda i: (0, 0))
    return pl.pallas_call(
        _outproj_kernel,
        grid=(n // tm,),
        in_specs=[
            pl.BlockSpec((tm, d), lambda i: (i, 0)),
            pl.BlockSpec((tm, ya2.shape[1]), lambda i: (i, 0)),
            pl.BlockSpec((tm, yd2.shape[1]), lambda i: (i, 0)),
            const(wa.shape), const(wd.shape),
        ],
        out_specs=pl.BlockSpec((tm, d), lambda i: (i, 0)),
        out_shape=jax.ShapeDtypeStruct((n, d), F32),
        compiler_params=pltpu.CompilerParams(
            dimension_semantics=("parallel",), vmem_limit_bytes=VMEM_LIMIT_BYTES),
        name="outproj",
    )(x2, ya2, yd2, wa, wd)


def _ffn_kernel(h_ref, g_ref, wg_ref, wu_ref, wd_ref, o_ref, u_ref):
    f = pl.program_id(1)

    @pl.when(f == 0)
    def _():
        h = h_ref[...]
        u_ref[...] = _rms_rows(h, g_ref[...]).astype(BF16)
        o_ref[...] = h

    u = u_ref[...]
    gate = jnp.dot(u, wg_ref[...], preferred_element_type=F32)
    up = jnp.dot(u, wu_ref[...], preferred_element_type=F32)
    act = (gate * (1.0 / (1.0 + jnp.exp(-gate))) * up).astype(BF16)
    o_ref[...] += jnp.dot(act, wd_ref[...], preferred_element_type=F32)


def _ffn(h2, g_ffn, wg, wu, wd, tm=1024, tf=512):
    n, d = h2.shape
    tm = min(tm, n)
    assert n % tm == 0
    dff = wg.shape[1]
    return pl.pallas_call(
        _ffn_kernel,
        grid=(n // tm, dff // tf),
        in_specs=[
            pl.BlockSpec((tm, d), lambda i, f: (i, 0)),
            _resident((1, d), lambda i, f: (0, 0)),
            pl.BlockSpec((d, tf), lambda i, f: (0, f)),
            pl.BlockSpec((d, tf), lambda i, f: (0, f)),
            pl.BlockSpec((tf, d), lambda i, f: (f, 0)),
        ],
        out_specs=pl.BlockSpec((tm, d), lambda i, f: (i, 0)),
        out_shape=jax.ShapeDtypeStruct((n, d), F32),
        scratch_shapes=[pltpu.VMEM((tm, d), BF16)],
        compiler_params=pltpu.CompilerParams(
            dimension_semantics=("parallel", "arbitrary"), vmem_limit_bytes=VMEM_LIMIT_BYTES),
        name="ffn",
    )(h2, g_ffn[None, :], wg, wu, wd)


def _ple_kernel(h_ref, p_ref, g_ref, wg_ref, wp_ref, go_ref, o_ref, *, sub, ncol):
    d = h_ref.shape[1]
    for st in range(h_ref.shape[0] // sub):
        rows = slice(st * sub, (st + 1) * sub)
        u = _rms_rows(h_ref[rows, :], g_ref[...]).astype(BF16)
        pp = jnp.dot(p_ref[rows, :].astype(BF16), wp_ref[...], preferred_element_type=F32)
        ppn = _rms_rows(pp, go_ref[...])
        for c in range(d // ncol):
            cols = slice(c * ncol, (c + 1) * ncol)
            z = jnp.dot(u, wg_ref[:, cols], preferred_element_type=F32)
            gate = 1.0 / (1.0 + jnp.exp(-z))
            o_ref[rows, cols] = h_ref[rows, cols] + gate * ppn[:, cols]


def _ple(h2, p2, g_ple, wg, wp, g_out, tm=1024, sub=512, ncol=512):
    n, d = h2.shape
    tm = min(tm, n)
    assert n % tm == 0
    sub = min(sub, tm)
    assert tm % sub == 0
    const = lambda shape: _resident(shape, lambda i: (0, 0))
    return pl.pallas_call(
        functools.partial(_ple_kernel, sub=sub, ncol=min(ncol, d)),
        grid=(n // tm,),
        in_specs=[
            pl.BlockSpec((tm, d), lambda i: (i, 0)),
            pl.BlockSpec((tm, p2.shape[1]), lambda i: (i, 0)),
            const((1, d)), const(wg.shape), const(wp.shape), const((1, d)),
        ],
        out_specs=pl.BlockSpec((tm, d), lambda i: (i, 0)),
        out_shape=jax.ShapeDtypeStruct((n, d), F32),
        compiler_params=pltpu.CompilerParams(
            dimension_semantics=("parallel",), vmem_limit_bytes=VMEM_LIMIT_BYTES),
        name="ple",
    )(h2, p2, g_ple[None, :], wg, wp, g_out[None, :])


def _alibi_slopes(n):
    return [2.0 ** (-8.0 * (h + 1) / n) for h in range(n)]


def _logit_bound(q_gain, k_gain):
    return 1.02 * math.sqrt(HEAD_DIM) * jnp.max(jnp.abs(q_gain)) * jnp.max(jnp.abs(k_gain))


def kernel(x, p, g_attn, w_in, qn_swa, kn_swa, sinks, qn_diff, kn_diff, lambda_q1, lambda_k1, lambda_q2,
           lambda_k2, g_sub, w_out, g_ffn, w_gate, w_up, w_down, g_ple, w_ple_gate, w_ple_proj, g_ple_out):
    b, s, d = x.shape
    depth = p.shape[0]
    n_qa = d // 2
    n_ka = n_va = SWA_KV_HEADS * HEAD_DIM
    n_qd = n_kd = n_vd = d // 2
    diff_heads = n_vd // DIFF_V_DIM
    swa_heads = n_qa // HEAD_DIM
    assert swa_heads == SWA_KV_HEADS * SWA_GROUP and s % TOKEN_TILE == 0
    c = np.cumsum([0, n_qa, n_ka, n_va, n_qd, n_kd, n_vd])
    h = x
    for i in range(depth):
        lam_init = 0.8 - 0.6 * math.exp(-0.3 * i)
        w = w_in[i]
        col = lambda k: w[:, c[k]:c[k + 1]]
        wfm = jnp.concatenate([col(0), col(2), col(3), col(5)], axis=1).T.astype(BF16)
        wtm = jnp.concatenate([col(1), col(4)], axis=1).astype(BF16)
        qa_t, va_t, qd_t, vd_t, ka, kd = _inproj(
            h, g_attn[i], wfm, wtm, qn_swa[i], kn_swa[i], qn_diff[i], kn_diff[i],
            (n_qa, n_ka, n_va, n_qd, n_kd, n_vd))
        ya = _swa_attention(qa_t, ka, va_t, sinks[i], _alibi_slopes(swa_heads),
                            _logit_bound(qn_swa[i], kn_swa[i]))
        lam_params = jnp.stack([lambda_q1[i], lambda_k1[i], lambda_q2[i], lambda_k2[i]]).astype(F32)
        yd = _diff_attention(qd_t, kd, vd_t, lam_params, g_sub[i], _alibi_slopes(diff_heads), lam_init,
                             _logit_bound(qn_diff[i], kn_diff[i]))
        wo = w_out[i].astype(BF16)
        h2 = _outproj(h.reshape(b * s, d), ya.reshape(b * s, n_qa), yd.reshape(b * s, n_vd),
                      wo[:n_qa], wo[n_qa:])
        h2 = _ffn(h2, g_ffn[i], w_gate[i].astype(BF16), w_up[i].astype(BF16), w_down[i].astype(BF16))
        h2 = _ple(h2, p[i].reshape(b * s, -1), g_ple[i], w_ple_gate[i].astype(BF16),
                  w_ple_proj[i].astype(BF16), g_ple_out[i])
        h = h2.reshape(b, s, d)
    return h
```

```python
import functools
import math

import jax
import jax.numpy as jnp
import numpy as np
from jax import lax
from jax.experimental import pallas as pl
from jax.experimental.pallas import tpu as pltpu

F32 = jnp.float32
BF16 = jnp.bfloat16

HEAD_DIM = 64
SWA_BLOCK = 128
SWA_KV_HEADS = 2
SWA_GROUP = 8
DIFF_V_DIM = 2 * HEAD_DIM
EPS = 1e-6
NEG_INF = float("-inf")
LOG2E = math.log2(math.e)
MAX_UNSHIFTED_LOGIT = 60.0

TOKEN_TILE = 256
INPROJ_TILES_PER_STEP = 4
DIFF_HEADS_PER_STEP = 4
VMEM_LIMIT_BYTES = 56 * 1024 * 1024

_NT = (((1,), (1,)), ((), ()))


def _resident(shape, index_map):
    return pl.BlockSpec(shape, index_map, pipeline_mode=pl.Buffered(1))


def _rms_rows(x, gain):
    ms = jnp.mean(x * x, axis=-1, keepdims=True)
    return x * lax.rsqrt(ms + EPS) * gain


def _transpose_cast_kernel(blk_ref, w_ref, o_ref):
    del blk_ref
    o_ref[...] = w_ref[...].T.astype(BF16)


def _transpose_cast_columns(w, col_ranges, cb=128):
    k, _ = w.shape
    blocks = [c0 // cb + i for c0, c1 in col_ranges for i in range((c1 - c0) // cb)]
    assert all(c0 % cb == 0 and c1 % cb == 0 for c0, c1 in col_ranges)
    return pl.pallas_call(
        _transpose_cast_kernel,
        grid_spec=pltpu.PrefetchScalarGridSpec(
            num_scalar_prefetch=1,
            grid=(len(blocks),),
            in_specs=[pl.BlockSpec((k, cb), lambda i, blk: (0, blk[i]))],
            out_specs=pl.BlockSpec((cb, k), lambda i, blk: (i, 0)),
        ),
        out_shape=jax.ShapeDtypeStruct((len(blocks) * cb, k), BF16),
        compiler_params=pltpu.CompilerParams(dimension_semantics=("parallel",)),
        name="weight_layout",
    )(jnp.asarray(blocks, jnp.int32), w)


def _inproj_kernel(x_ref, g_ref, wfm_ref, wtm_ref, gqa_ref, gqd_ref, gka_ref, gkd_ref, bd_ref,
                   qa_ref, va_ref, qd_ref, vd_ref, ka_ref, kd_ref, *, n_qa, n_va, n_qd, n_vd, n_ka, n_kd):
    tm = TOKEN_TILE
    per_tile = tm // SWA_BLOCK

    def headnorm_fm(z, gain, store):
        for h in range(z.shape[0] // HEAD_DIM):
            zh = z[h * HEAD_DIM:(h + 1) * HEAD_DIM, :]
            ms = jnp.mean(zh * zh, axis=0, keepdims=True)
            store(h, (zh * lax.rsqrt(ms + EPS) * gain).astype(BF16))

    def headnorm_tm(z, gain):
        ncols = z.shape[1]
        ssq = jnp.dot((z * z).astype(BF16), bd_ref[0:ncols, 0:ncols], preferred_element_type=F32)
        return (z * lax.rsqrt(ssq * (1.0 / HEAD_DIM) + EPS) * gain).astype(BF16)

    for st in range(x_ref.shape[0] // tm):
        rows = slice(st * tm, (st + 1) * tm)
        u = _rms_rows(x_ref[rows, :], g_ref[...]).astype(BF16)

        def fm(row0, nrows):
            return lax.dot_general(wfm_ref[row0:row0 + nrows, :], u, _NT, preferred_element_type=F32)

        def store_qa(h, val):
            for t in range(per_tile):
                qa_ref[0, st * per_tile + t, h * HEAD_DIM:(h + 1) * HEAD_DIM, :] = (
                    val[:, t * SWA_BLOCK:(t + 1) * SWA_BLOCK])

        def store_qd(h, val):
            qd_ref[0, st, h * HEAD_DIM:(h + 1) * HEAD_DIM, :] = val

        r = 0
        headnorm_fm(fm(r, n_qa), gqa_ref[...], store_qa)
        r += n_qa
        zva = fm(r, n_va).astype(BF16)
        for t in range(per_tile):
            va_ref[0, st * per_tile + t] = zva[:, t * SWA_BLOCK:(t + 1) * SWA_BLOCK]
        r += n_va
        headnorm_fm(fm(r, n_qd), gqd_ref[...], store_qd)
        r += n_qd
        vd_ref[0, st] = fm(r, n_vd).astype(BF16)

        zk = jnp.dot(u, wtm_ref[...], preferred_element_type=F32)
        ka_ref[0, rows, :] = headnorm_tm(zk[:, :n_ka], gka_ref[...])
        cw = bd_ref.shape[0]
        for c in range(n_kd // cw):
            kd_ref[0, rows, c * cw:(c + 1) * cw] = headnorm_tm(
                zk[:, n_ka + c * cw:n_ka + (c + 1) * cw], gkd_ref[...])


def _inproj(x, g_attn, wfm, wtm, qn_swa, kn_swa, qn_diff, kn_diff, dims):
    b, s, d = x.shape
    n_qa, n_ka, n_va, n_qd, n_kd, n_vd = dims
    tm = TOKEN_TILE
    nt = s // tm
    tps = min(INPROJ_TILES_PER_STEP, nt)
    assert nt % tps == 0
    blk = tps * tm
    scale = LOG2E / math.sqrt(HEAD_DIM)
    gqa = jnp.broadcast_to((qn_swa * scale)[:, None], (HEAD_DIM, tm)).astype(F32)
    gqd = jnp.broadcast_to((qn_diff * scale)[:, None], (HEAD_DIM, tm)).astype(F32)
    gka = jnp.tile(kn_swa, n_ka // HEAD_DIM)[None, :].astype(F32)
    gkd = jnp.tile(kn_diff, 256 // HEAD_DIM)[None, :].astype(F32)
    idx = np.arange(256) // HEAD_DIM
    bd = jnp.asarray(idx[:, None] == idx[None, :], dtype=BF16)
    const = lambda shape: _resident(shape, lambda bi, ti: (0,) * len(shape))
    kern = functools.partial(_inproj_kernel, n_qa=n_qa, n_va=n_va, n_qd=n_qd, n_vd=n_vd, n_ka=n_ka, n_kd=n_kd)
    return pl.pallas_call(
        kern,
        grid=(b, nt // tps),
        in_specs=[
            pl.BlockSpec((None, blk, d), lambda bi, ti: (bi, ti, 0)),
            const((1, d)), const(wfm.shape), const(wtm.shape),
            const((HEAD_DIM, tm)), const((HEAD_DIM, tm)), const((1, n_ka)), const((1, 256)), const((256, 256)),
        ],
        out_specs=[
            pl.BlockSpec((1, blk // SWA_BLOCK, n_qa, SWA_BLOCK), lambda bi, ti: (bi, ti, 0, 0)),
            pl.BlockSpec((1, blk // SWA_BLOCK, n_va, SWA_BLOCK), lambda bi, ti: (bi, ti, 0, 0)),
            pl.BlockSpec((1, tps, n_qd, tm), lambda bi, ti: (bi, ti, 0, 0)),
            pl.BlockSpec((1, tps, n_vd, tm), lambda bi, ti: (bi, ti, 0, 0)),
            pl.BlockSpec((1, blk, n_ka), lambda bi, ti: (bi, ti, 0)),
            pl.BlockSpec((1, blk, n_kd), lambda bi, ti: (bi, ti, 0)),
        ],
        out_shape=[
            jax.ShapeDtypeStruct((b, s // SWA_BLOCK, n_qa, SWA_BLOCK), BF16),
            jax.ShapeDtypeStruct((b, s // SWA_BLOCK, n_va, SWA_BLOCK), BF16),
            jax.ShapeDtypeStruct((b, nt, n_qd, tm), BF16),
            jax.ShapeDtypeStruct((b, nt, n_vd, tm), BF16),
            jax.ShapeDtypeStruct((b, s, n_ka), BF16),
            jax.ShapeDtypeStruct((b, s, n_kd), BF16),
        ],
        compiler_params=pltpu.CompilerParams(
            dimension_semantics=("parallel", "parallel"), vmem_limit_bytes=VMEM_LIMIT_BYTES),
        name="inproj",
    )(x, g_attn[None, :], wfm, wtm, gqa, gqd, gka, gkd, bd)


def _staged_pipeline(n_steps, stages):
    depth = len(stages)

    def iteration(i, parity):
        for k in reversed(range(depth)):
            t = i - k
            if isinstance(i, int) and not 0 <= t < n_steps:
                continue
            stages[k](t, (parity - k) % 2)

    fill_end = min(depth - 1, n_steps)
    for i in range(fill_end):
        iteration(i, i % 2)
    n_pairs = (n_steps - fill_end) // 2

    def pair(m, carry):
        i = fill_end + 2 * m
        iteration(i, fill_end % 2)
        iteration(i + 1, (fill_end + 1) % 2)
        return carry

    lax.fori_loop(0, n_pairs, pair, 0)
    for i in range(fill_end + 2 * n_pairs, n_steps + depth - 1):
        iteration(i, i % 2)


def _swa_kernel(bounded_ref, q_ref, k_ref, v_ref, bias_ref, sink_ref, o_ref, s_ref, p_ref, d_ref):
    hk = pl.program_id(0)
    nblk = q_ref.shape[1]
    w = SWA_BLOCK
    sink = sink_ref[0]

    def window_start(n):
        return jnp.maximum(n - 1, 0)

    def scores(n, slot):
        qblk = q_ref[0, n]
        qg = jnp.concatenate([qblk[g * HEAD_DIM:(g + 1) * HEAD_DIM, :] for g in range(SWA_GROUP)], axis=1)
        zero = jnp.zeros_like(qg)
        qpad = jnp.concatenate([jnp.where(hk == 0, qg, zero), jnp.where(hk == 1, qg, zero)], axis=0)
        kwin = k_ref[0, pl.ds(pl.multiple_of(window_start(n) * w, w), 2 * w), :]
        s_ref[slot] = jnp.dot(kwin, qpad, preferred_element_type=F32)

    def probs(n, slot, *, bounded):
        t = s_ref[slot] + bias_ref[jnp.minimum(n, 1)]
        if bounded:
            e = jnp.exp2(t)
            d_ref[slot] = jnp.sum(e, axis=0, keepdims=True) + jnp.exp2(sink)
        else:
            m = jnp.maximum(jnp.max(t, axis=0, keepdims=True), sink)
            e = jnp.exp2(t - m)
            d_ref[slot] = jnp.sum(e, axis=0, keepdims=True) + jnp.exp2(sink - m)
        p_ref[slot] = e.astype(BF16)

    def weighted_values(n, slot):
        first = window_start(n)
        vwin = jnp.concatenate([v_ref[0, first], v_ref[0, first + 1]], axis=1)
        o = jnp.dot(vwin, p_ref[slot], preferred_element_type=F32) * (1.0 / d_ref[slot])
        for gp in range(SWA_GROUP // 2):
            pair = jnp.concatenate([o[:, (2 * gp) * w:(2 * gp + 1) * w],
                                    o[:, (2 * gp + 1) * w:(2 * gp + 2) * w]], axis=0)
            o_ref[0, pl.ds(pl.multiple_of(n * w, w), w), gp * 2 * HEAD_DIM:(gp + 1) * 2 * HEAD_DIM] = (
                pair.T.astype(BF16))

    @pl.when(bounded_ref[0] != 0)
    def _():
        _staged_pipeline(nblk, [scores, functools.partial(probs, bounded=True), weighted_values])

    @pl.when(bounded_ref[0] == 0)
    def _():
        def block(n, carry):
            scores(n, 0)
            probs(n, 0, bounded=False)
            weighted_values(n, 0)
            return carry

        lax.fori_loop(0, nblk, block, 0)


def _swa_bias(slopes):
    w = SWA_BLOCK
    kj = np.arange(2 * w)[:, None]
    qi = np.arange(w)[None, :]
    out = np.empty((2, SWA_KV_HEADS, 2 * w, SWA_GROUP * w), np.float32)
    for first, dist in ((0, qi - kj), (1, qi - kj + w)):
        valid = (dist >= 0) & (dist < w)
        for hk in range(SWA_KV_HEADS):
            for g in range(SWA_GROUP):
                sl = slopes[hk * SWA_GROUP + g]
                out[first, hk, :, g * w:(g + 1) * w] = np.where(valid, -sl * dist, -np.inf)
    return out


def _swa_attention(qa_t, ka, va_t, sinks, slopes, logit_bound):
    b, nblk, n_qa, w = qa_t.shape
    s = nblk * w
    gw = SWA_GROUP * w
    bias = jnp.asarray(_swa_bias(slopes) * LOG2E)
    sinks = sinks.astype(F32)
    sink_cols = jnp.repeat((sinks * LOG2E).reshape(SWA_KV_HEADS, 1, SWA_GROUP), w, axis=-1)
    bounded = jnp.asarray(jnp.maximum(logit_bound, jnp.max(jnp.abs(sinks))) <= MAX_UNSHIFTED_LOGIT, jnp.int32)
    return pl.pallas_call(
        _swa_kernel,
        grid=(SWA_KV_HEADS, b),
        in_specs=[
            pl.BlockSpec(memory_space=pltpu.SMEM),
            pl.BlockSpec((1, nblk, SWA_GROUP * HEAD_DIM, w), lambda hk, bi: (bi, 0, hk, 0)),
            pl.BlockSpec((1, s, SWA_KV_HEADS * HEAD_DIM), lambda hk, bi: (bi, 0, 0)),
            pl.BlockSpec((1, nblk, HEAD_DIM, w), lambda hk, bi: (bi, 0, hk, 0)),
            pl.BlockSpec((2, None, 2 * w, gw), lambda hk, bi: (0, hk, 0, 0)),
            pl.BlockSpec((1, 1, gw), lambda hk, bi: (hk, 0, 0)),
        ],
        out_specs=pl.BlockSpec((1, s, SWA_GROUP * HEAD_DIM), lambda hk, bi: (bi, 0, hk)),
        out_shape=jax.ShapeDtypeStruct((b, s, n_qa), BF16),
        scratch_shapes=[pltpu.VMEM((2, 2 * w, gw), F32), pltpu.VMEM((2, 2 * w, gw), BF16),
                        pltpu.VMEM((2, 1, gw), F32)],
        compiler_params=pltpu.CompilerParams(
            dimension_semantics=("parallel", "parallel"), vmem_limit_bytes=VMEM_LIMIT_BYTES),
        name="swa_attn",
    )(bounded.reshape(1), qa_t, ka, va_t, bias, sink_cols)


def _diff_kernel(bounded_ref, slope_ref, iq_tbl_ref, j_tbl_ref, q_ref, k_ref, v_ref, bias_ref, lam_ref, gsub_ref,
                 o_ref, acc_ref, l_ref, p_ref, *, lam_init):
    hg = pl.program_id(0)
    nt, t_q = q_ref.shape[1], q_ref.shape[3]
    hpb = acc_ref.shape[0]
    n_steps = nt * (nt + 1) // 2
    dv = DIFF_V_DIM
    lp = lam_ref[...]
    lam = (jnp.exp(jnp.sum(lp[0:1] * lp[1:2], axis=-1, keepdims=True))
           - jnp.exp(jnp.sum(lp[2:3] * lp[3:4], axis=-1, keepdims=True)) + lam_init)
    zero = jnp.zeros((HEAD_DIM, t_q), BF16)

    def padded_q(iq, hh):
        q = q_ref[0, iq, hh * dv:(hh + 1) * dv, :]
        return jnp.concatenate([jnp.concatenate([q[:HEAD_DIM], zero], axis=0),
                                jnp.concatenate([zero, q[HEAD_DIM:]], axis=0)], axis=1)

    def logits(iq, j, hh, qb):
        kblk = k_ref[0, pl.ds(pl.multiple_of(j * t_q, t_q), t_q), hh * dv:(hh + 1) * dv]
        diag = jnp.where(j == iq, 1, 0)
        cj = slope_ref[hg * hpb + hh] * jnp.asarray((j - iq) * t_q, F32)
        return jnp.dot(kblk, qb, preferred_element_type=F32) + bias_ref[hh, diag], cj

    def finalize(iq, hh, acc, l):
        acc = acc * (1.0 / l)
        od = acc[:, :t_q] - lam * acc[:, t_q:]
        ms = jnp.mean(od * od, axis=0, keepdims=True)
        y = od * lax.rsqrt(ms + EPS) * gsub_ref[...]
        o_ref[0, pl.ds(pl.multiple_of(iq * t_q, t_q), t_q), hh * dv:(hh + 1) * dv] = y.T.astype(BF16)

    def probs(t, slot):
        iq, j = iq_tbl_ref[t], j_tbl_ref[t]
        for hh in range(hpb):
            s, cj = logits(iq, j, hh, padded_q(iq, hh))
            p = jnp.exp2(s + cj)
            l_ref[hh, iq] += jnp.sum(p.reshape(t_q // 8, 8, 2 * t_q), axis=0)
            p_ref[slot, hh] = p.astype(BF16)

    def pv(t, slot):
        iq, j = iq_tbl_ref[t], j_tbl_ref[t]
        for hh in range(hpb):
            acc_ref[hh, iq] += jnp.dot(v_ref[0, j, hh * dv:(hh + 1) * dv, :], p_ref[slot, hh],
                                       preferred_element_type=F32)

    @pl.when(bounded_ref[0] != 0)
    def _():
        acc_ref[...] = jnp.zeros_like(acc_ref)
        l_ref[...] = jnp.zeros_like(l_ref)
        _staged_pipeline(n_steps, [probs, pv])

        def fin(iq, carry):
            for hh in range(hpb):
                finalize(iq, hh, acc_ref[hh, iq], jnp.sum(l_ref[hh, iq], axis=0, keepdims=True))
            return carry

        lax.fori_loop(0, nt, fin, 0)

    @pl.when(bounded_ref[0] == 0)
    def _():
        def q_block(iq, carry):
            qbs = [padded_q(iq, hh) for hh in range(hpb)]
            acc_ref[:, 0] = jnp.zeros((hpb, dv, 2 * t_q), F32)

            def step(j, stats):
                out = []
                for hh in range(hpb):
                    m, l = stats[2 * hh], stats[2 * hh + 1]
                    s, cj = logits(iq, j, hh, qbs[hh])
                    mnew = jnp.maximum(m, jnp.max(s, axis=0, keepdims=True) + cj)
                    alpha = jnp.exp2(m - mnew)
                    p = jnp.exp2(s + (cj - mnew))
                    out += [mnew, alpha * l + jnp.sum(p, axis=0, keepdims=True)]
                    acc_ref[hh, 0] = alpha * acc_ref[hh, 0] + jnp.dot(
                        v_ref[0, j, hh * dv:(hh + 1) * dv, :], p.astype(BF16), preferred_element_type=F32)
                return tuple(out)

            init = (jnp.full((1, 2 * t_q), NEG_INF, F32), jnp.zeros((1, 2 * t_q), F32)) * hpb
            stats = lax.fori_loop(0, iq + 1, step, init)
            for hh in range(hpb):
                finalize(iq, hh, acc_ref[hh, 0], stats[2 * hh + 1])
            return carry

        lax.fori_loop(0, nt, q_block, 0)


def _diff_bias(slopes, t):
    kj = np.arange(t)[:, None]
    qi = np.arange(2 * t)[None, :] % t
    rel = (kj - qi).astype(np.float32)
    out = np.empty((len(slopes), 2, t, 2 * t), np.float32)
    for h, sl in enumerate(slopes):
        out[h, 0] = sl * rel
        out[h, 1] = np.where(kj <= qi, sl * rel, -np.inf)
    return out


def _diff_attention(qd_t, kd, vd_t, lam_params, g_sub, slopes, lam_init, logit_bound, hpb=DIFF_HEADS_PER_STEP):
    b, nt, n_qd, t = qd_t.shape
    s = nt * t
    nh = n_qd // DIFF_V_DIM
    bias = jnp.asarray(_diff_bias(slopes, t) * LOG2E)
    gsub = jnp.broadcast_to((g_sub * (1.0 - lam_init))[:, None], (DIFF_V_DIM, t)).astype(F32)
    bounded = jnp.asarray(logit_bound <= MAX_UNSHIFTED_LOGIT, jnp.int32)
    pairs = [(iq, j) for iq in range(nt) for j in range(iq + 1)]
    iq_tbl = jnp.asarray([pq[0] for pq in pairs], jnp.int32)
    j_tbl = jnp.asarray([pq[1] for pq in pairs], jnp.int32)
    kern = functools.partial(_diff_kernel, lam_init=lam_init)
    return pl.pallas_call(
        kern,
        grid=(nh // hpb, b),
        in_specs=[
            pl.BlockSpec(memory_space=pltpu.SMEM),
            pl.BlockSpec(memory_space=pltpu.SMEM),
            pl.BlockSpec(memory_space=pltpu.SMEM),
            pl.BlockSpec(memory_space=pltpu.SMEM),
            pl.BlockSpec((1, nt, hpb * DIFF_V_DIM, t), lambda hg, bi: (bi, 0, hg, 0)),
            pl.BlockSpec((1, s, hpb * DIFF_V_DIM), lambda hg, bi: (bi, 0, hg)),
            pl.BlockSpec((1, nt, hpb * DIFF_V_DIM, t), lambda hg, bi: (bi, 0, hg, 0)),
            pl.BlockSpec((hpb, 2, t, 2 * t), lambda hg, bi: (hg, 0, 0, 0)),
            pl.BlockSpec((4, HEAD_DIM), lambda hg, bi: (0, 0)),
            pl.BlockSpec((DIFF_V_DIM, t), lambda hg, bi: (0, 0)),
        ],
        out_specs=pl.BlockSpec((1, s, hpb * DIFF_V_DIM), lambda hg, bi: (bi, 0, hg)),
        out_shape=jax.ShapeDtypeStruct((b, s, n_qd), BF16),
        scratch_shapes=[pltpu.VMEM((hpb, nt, DIFF_V_DIM, 2 * t), F32),
                        pltpu.VMEM((hpb, nt, 8, 2 * t), F32),
                        pltpu.VMEM((2, hpb, t, 2 * t), BF16)],
        compiler_params=pltpu.CompilerParams(
            dimension_semantics=("parallel", "parallel"), vmem_limit_bytes=VMEM_LIMIT_BYTES),
        name="diff_attn",
    )(bounded.reshape(1), jnp.asarray(np.asarray(slopes) * LOG2E, F32), iq_tbl, j_tbl,
      qd_t, kd, vd_t, bias, lam_params, gsub)


def _outproj_kernel(x_ref, ya_ref, yd_ref, wa_ref, wd_ref, o_ref):
    o_ref[...] = (x_ref[...]
                  + jnp.dot(ya_ref[...], wa_ref[...], preferred_element_type=F32)
                  + jnp.dot(yd_ref[...], wd_ref[...], preferred_element_type=F32))


def _outproj(x2, ya2, yd2, wa, wd, tm=512):
    n, d = x2.shape
    tm = min(tm, n)
    assert n % tm == 0
    const = lambda shape: _resident(shape, lambda i: (0, 0))
    return pl.pallas_call(
        _outproj_kernel,
        grid=(n // tm,),
        in_specs=[
            pl.BlockSpec((tm, d), lambda i: (i, 0)),
            pl.BlockSpec((tm, ya2.shape[1]), lambda i: (i, 0)),
            pl.BlockSpec((tm, yd2.shape[1]), lambda i: (i, 0)),
            const(wa.shape), const(wd.shape),
        ],
        out_specs=pl.BlockSpec((tm, d), lambda i: (i, 0)),
        out_shape=jax.ShapeDtypeStruct((n, d), F32),
        compiler_params=pltpu.CompilerParams(
            dimension_semantics=("parallel",), vmem_limit_bytes=VMEM_LIMIT_BYTES),
        name="outproj",
    )(x2, ya2, yd2, wa, wd)


def _ffn_kernel(h_ref, g_ref, wg_ref, wu_ref, wd_ref, o_ref, u_ref):
    f = pl.program_id(1)

    @pl.when(f == 0)
    def _():
        h = h_ref[...]
        u_ref[...] = _rms_rows(h, g_ref[...]).astype(BF16)
        o_ref[...] = h

    u = u_ref[...]
    gate = jnp.dot(u, wg_ref[...], preferred_element_type=F32)
    up = jnp.dot(u, wu_ref[...], preferred_element_type=F32)
    act = (gate * (1.0 / (1.0 + jnp.exp(-gate))) * up).astype(BF16)
    o_ref[...] += jnp.dot(act, wd_ref[...], preferred_element_type=F32)


def _ffn(h2, g_ffn, wg, wu, wd, tm=1024, tf=512):
    n, d = h2.shape
    tm = min(tm, n)
    assert n % tm == 0
    dff = wg.shape[1]
    return pl.pallas_call(
        _ffn_kernel,
        grid=(n // tm, dff // tf),
        in_specs=[
            pl.BlockSpec((tm, d), lambda i, f: (i, 0)),
            _resident((1, d), lambda i, f: (0, 0)),
            pl.BlockSpec((d, tf), lambda i, f: (0, f)),
            pl.BlockSpec((d, tf), lambda i, f: (0, f)),
            pl.BlockSpec((tf, d), lambda i, f: (f, 0)),
        ],
        out_specs=pl.BlockSpec((tm, d), lambda i, f: (i, 0)),
        out_shape=jax.ShapeDtypeStruct((n, d), F32),
        scratch_shapes=[pltpu.VMEM((tm, d), BF16)],
        compiler_params=pltpu.CompilerParams(
            dimension_semantics=("parallel", "arbitrary"), vmem_limit_bytes=VMEM_LIMIT_BYTES),
        name="ffn",
    )(h2, g_ffn[None, :], wg, wu, wd)


def _ple_kernel(h_ref, p_ref, g_ref, wg_ref, wp_ref, go_ref, o_ref, *, sub, ncol):
    d = h_ref.shape[1]
    for st in range(h_ref.shape[0] // sub):
        rows = slice(st * sub, (st + 1) * sub)
        u = _rms_rows(h_ref[rows, :], g_ref[...]).astype(BF16)
        pp = jnp.dot(p_ref[rows, :].astype(BF16), wp_ref[...], preferred_element_type=F32)
        ppn = _rms_rows(pp, go_ref[...])
        for c in range(d // ncol):
            cols = slice(c * ncol, (c + 1) * ncol)
            z = jnp.dot(u, wg_ref[:, cols], preferred_element_type=F32)
            gate = 1.0 / (1.0 + jnp.exp(-z))
            o_ref[rows, cols] = h_ref[rows, cols] + gate * ppn[:, cols]


def _ple(h2, p2, g_ple, wg, wp, g_out, tm=1024, sub=512, ncol=512):
    n, d = h2.shape
    tm = min(tm, n)
    assert n % tm == 0
    sub = min(sub, tm)
    assert tm % sub == 0
    const = lambda shape: _resident(shape, lambda i: (0, 0))
    return pl.pallas_call(
        functools.partial(_ple_kernel, sub=sub, ncol=min(ncol, d)),
        grid=(n // tm,),
        in_specs=[
            pl.BlockSpec((tm, d), lambda i: (i, 0)),
            pl.BlockSpec((tm, p2.shape[1]), lambda i: (i, 0)),
            const((1, d)), const(wg.shape), const(wp.shape), const((1, d)),
        ],
        out_specs=pl.BlockSpec((tm, d), lambda i: (i, 0)),
        out_shape=jax.ShapeDtypeStruct((n, d), F32),
        compiler_params=pltpu.CompilerParams(
            dimension_semantics=("parallel",), vmem_limit_bytes=VMEM_LIMIT_BYTES),
        name="ple",
    )(h2, p2, g_ple[None, :], wg, wp, g_out[None, :])


def _alibi_slopes(n):
    return [2.0 ** (-8.0 * (h + 1) / n) for h in range(n)]


def _logit_bound(q_gain, k_gain):
    return 1.02 * math.sqrt(HEAD_DIM) * jnp.max(jnp.abs(q_gain)) * jnp.max(jnp.abs(k_gain))


def kernel(x, p, g_attn, w_in, qn_swa, kn_swa, sinks, qn_diff, kn_diff, lambda_q1, lambda_k1, lambda_q2,
           lambda_k2, g_sub, w_out, g_ffn, w_gate, w_up, w_down, g_ple, w_ple_gate, w_ple_proj, g_ple_out):
    b, s, d = x.shape
    depth = p.shape[0]
    n_qa = d // 2
    n_ka = n_va = SWA_KV_HEADS * HEAD_DIM
    n_qd = n_kd = n_vd = d // 2
    diff_heads = n_vd // DIFF_V_DIM
    swa_heads = n_qa // HEAD_DIM
    assert swa_heads == SWA_KV_HEADS * SWA_GROUP and s % TOKEN_TILE == 0
    c = np.cumsum([0, n_qa, n_ka, n_va, n_qd, n_kd, n_vd])
    h = x
    for i in range(depth):
        lam_init = 0.8 - 0.6 * math.exp(-0.3 * i)
        w = w_in[i]
        col = lambda k: w[:, c[k]:c[k + 1]]
        wfm = _transpose_cast_columns(w, [(c[k], c[k + 1]) for k in (0, 2, 3, 5)])
        wtm = jnp.concatenate([col(1), col(4)], axis=1).astype(BF16)
        qa_t, va_t, qd_t, vd_t, ka, kd = _inproj(
            h, g_attn[i], wfm, wtm, qn_swa[i], kn_swa[i], qn_diff[i], kn_diff[i],
            (n_qa, n_ka, n_va, n_qd, n_kd, n_vd))
        ya = _swa_attention(qa_t, ka, va_t, sinks[i], _alibi_slopes(swa_heads),
                            _logit_bound(qn_swa[i], kn_swa[i]))
        lam_params = jnp.stack([lambda_q1[i], lambda_k1[i], lambda_q2[i], lambda_k2[i]]).astype(F32)
        yd = _diff_attention(qd_t, kd, vd_t, lam_params, g_sub[i], _alibi_slopes(diff_heads), lam_init,
                             _logit_bound(qn_diff[i], kn_diff[i]))
        wo = w_out[i].astype(BF16)
        h2 = _outproj(h.reshape(b * s, d), ya.reshape(b * s, n_qa), yd.reshape(b * s, n_vd),
                      wo[:n_qa], wo[n_qa:])
        h2 = _ffn(h2, g_ffn[i], w_gate[i].astype(BF16), w_up[i].astype(BF16), w_down[i].astype(BF16))
        h2 = _ple(h2, p[i].reshape(b * s, -1), g_ple[i], w_ple_gate[i].astype(BF16),
                  w_ple_proj[i].astype(BF16), g_ple_out[i])
        h = h2.reshape(b, s, d)
    return h
```

```python
import functools
import math

import jax
import jax.numpy as jnp
import numpy as np
from jax import lax
from jax.experimental import pallas as pl
from jax.experimental.pallas import tpu as pltpu

F32 = jnp.float32
BF16 = jnp.bfloat16

HEAD_DIM = 64
SWA_BLOCK = 128
SWA_KV_HEADS = 2
SWA_GROUP = 8
DIFF_V_DIM = 2 * HEAD_DIM
EPS = 1e-6
NEG_INF = float("-inf")
LOG2E = math.log2(math.e)
MAX_UNSHIFTED_LOGIT = 60.0

TOKEN_TILE = 256
INPROJ_TILES_PER_STEP = 4
DIFF_HEADS_PER_STEP = 4
VMEM_LIMIT_BYTES = 56 * 1024 * 1024

_NT = (((1,), (1,)), ((), ()))


def _resident(shape, index_map):
    return pl.BlockSpec(shape, index_map, pipeline_mode=pl.Buffered(1))


def _rms_rows(x, gain):
    ms = jnp.mean(x * x, axis=-1, keepdims=True)
    return x * lax.rsqrt(ms + EPS) * gain


def _transpose_cast_kernel(blk_ref, w_ref, o_ref):
    del blk_ref
    o_ref[...] = w_ref[...].T.astype(BF16)


def _transpose_cast_columns(w, col_ranges, cb=128):
    k, _ = w.shape
    blocks = [c0 // cb + i for c0, c1 in col_ranges for i in range((c1 - c0) // cb)]
    assert all(c0 % cb == 0 and c1 % cb == 0 for c0, c1 in col_ranges)
    return pl.pallas_call(
        _transpose_cast_kernel,
        grid_spec=pltpu.PrefetchScalarGridSpec(
            num_scalar_prefetch=1,
            grid=(len(blocks),),
            in_specs=[pl.BlockSpec((k, cb), lambda i, blk: (0, blk[i]))],
            out_specs=pl.BlockSpec((cb, k), lambda i, blk: (i, 0)),
        ),
        out_shape=jax.ShapeDtypeStruct((len(blocks) * cb, k), BF16),
        compiler_params=pltpu.CompilerParams(dimension_semantics=("parallel",)),
        name="weight_layout",
    )(jnp.asarray(blocks, jnp.int32), w)


def _inproj_kernel(x_ref, g_ref, wfm_ref, wtm_ref, gqa_ref, gqd_ref, gka_ref, gkd_ref, bd_ref,
                   qa_ref, va_ref, qd_ref, vd_ref, ka_ref, kd_ref, *, n_qa, n_va, n_qd, n_vd, n_ka, n_kd):
    tm = TOKEN_TILE
    per_tile = tm // SWA_BLOCK

    def headnorm_fm(z, gain, store):
        for h in range(z.shape[0] // HEAD_DIM):
            zh = z[h * HEAD_DIM:(h + 1) * HEAD_DIM, :]
            ms = jnp.mean(zh * zh, axis=0, keepdims=True)
            store(h, (zh * lax.rsqrt(ms + EPS) * gain).astype(BF16))

    def headnorm_tm(z, gain):
        ncols = z.shape[1]
        ssq = jnp.dot((z * z).astype(BF16), bd_ref[0:ncols, 0:ncols], preferred_element_type=F32)
        return (z * lax.rsqrt(ssq * (1.0 / HEAD_DIM) + EPS) * gain).astype(BF16)

    for st in range(x_ref.shape[0] // tm):
        rows = slice(st * tm, (st + 1) * tm)
        u = _rms_rows(x_ref[rows, :], g_ref[...]).astype(BF16)

        def fm(row0, nrows):
            return lax.dot_general(wfm_ref[row0:row0 + nrows, :], u, _NT, preferred_element_type=F32)

        def store_qa(h, val):
            for t in range(per_tile):
                qa_ref[0, st * per_tile + t, h * HEAD_DIM:(h + 1) * HEAD_DIM, :] = (
                    val[:, t * SWA_BLOCK:(t + 1) * SWA_BLOCK])

        def store_qd(h, val):
            qd_ref[0, st, h * HEAD_DIM:(h + 1) * HEAD_DIM, :] = val

        r = 0
        headnorm_fm(fm(r, n_qa), gqa_ref[...], store_qa)
        r += n_qa
        zva = fm(r, n_va).astype(BF16)
        for t in range(per_tile):
            va_ref[0, st * per_tile + t] = zva[:, t * SWA_BLOCK:(t + 1) * SWA_BLOCK]
        r += n_va
        headnorm_fm(fm(r, n_qd), gqd_ref[...], store_qd)
        r += n_qd
        vd_ref[0, st] = fm(r, n_vd).astype(BF16)

        zk = jnp.dot(u, wtm_ref[...], preferred_element_type=F32)
        ka_ref[0, rows, :] = headnorm_tm(zk[:, :n_ka], gka_ref[...])
        cw = bd_ref.shape[0]
        for c in range(n_kd // cw):
            kd_ref[0, rows, c * cw:(c + 1) * cw] = headnorm_tm(
                zk[:, n_ka + c * cw:n_ka + (c + 1) * cw], gkd_ref[...])


def _inproj(x, g_attn, wfm, wtm, qn_swa, kn_swa, qn_diff, kn_diff, dims):
    b, s, d = x.shape
    n_qa, n_ka, n_va, n_qd, n_kd, n_vd = dims
    tm = TOKEN_TILE
    nt = s // tm
    tps = min(INPROJ_TILES_PER_STEP, nt)
    assert nt % tps == 0
    blk = tps * tm
    scale = LOG2E / math.sqrt(HEAD_DIM)
    gqa = jnp.broadcast_to((qn_swa * scale)[:, None], (HEAD_DIM, tm)).astype(F32)
    gqd = jnp.broadcast_to((qn_diff * scale)[:, None], (HEAD_DIM, tm)).astype(F32)
    gka = jnp.tile(kn_swa, n_ka // HEAD_DIM)[None, :].astype(F32)
    gkd = jnp.tile(kn_diff, 256 // HEAD_DIM)[None, :].astype(F32)
    idx = np.arange(256) // HEAD_DIM
    bd = jnp.asarray(idx[:, None] == idx[None, :], dtype=BF16)
    const = lambda shape: _resident(shape, lambda bi, ti: (0,) * len(shape))
    kern = functools.partial(_inproj_kernel, n_qa=n_qa, n_va=n_va, n_qd=n_qd, n_vd=n_vd, n_ka=n_ka, n_kd=n_kd)
    return pl.pallas_call(
        kern,
        grid=(b, nt // tps),
        in_specs=[
            pl.BlockSpec((None, blk, d), lambda bi, ti: (bi, ti, 0)),
            const((1, d)), const(wfm.shape), const(wtm.shape),
            const((HEAD_DIM, tm)), const((HEAD_DIM, tm)), const((1, n_ka)), const((1, 256)), const((256, 256)),
        ],
        out_specs=[
            pl.BlockSpec((1, blk // SWA_BLOCK, n_qa, SWA_BLOCK), lambda bi, ti: (bi, ti, 0, 0)),
            pl.BlockSpec((1, blk // SWA_BLOCK, n_va, SWA_BLOCK), lambda bi, ti: (bi, ti, 0, 0)),
            pl.BlockSpec((1, tps, n_qd, tm), lambda bi, ti: (bi, ti, 0, 0)),
            pl.BlockSpec((1, tps, n_vd, tm), lambda bi, ti: (bi, ti, 0, 0)),
            pl.BlockSpec((1, blk, n_ka), lambda bi, ti: (bi, ti, 0)),
            pl.BlockSpec((1, blk, n_kd), lambda bi, ti: (bi, ti, 0)),
        ],
        out_shape=[
            jax.ShapeDtypeStruct((b, s // SWA_BLOCK, n_qa, SWA_BLOCK), BF16),
            jax.ShapeDtypeStruct((b, s // SWA_BLOCK, n_va, SWA_BLOCK), BF16),
            jax.ShapeDtypeStruct((b, nt, n_qd, tm), BF16),
            jax.ShapeDtypeStruct((b, nt, n_vd, tm), BF16),
            jax.ShapeDtypeStruct((b, s, n_ka), BF16),
            jax.ShapeDtypeStruct((b, s, n_kd), BF16),
        ],
        compiler_params=pltpu.CompilerParams(
            dimension_semantics=("parallel", "parallel"), vmem_limit_bytes=VMEM_LIMIT_BYTES),
        name="inproj",
    )(x, g_attn[None, :], wfm, wtm, gqa, gqd, gka, gkd, bd)


def _staged_pipeline(n_steps, stages):
    depth = len(stages)

    def iteration(i, parity):
        for k in reversed(range(depth)):
            t = i - k
            if isinstance(i, int) and not 0 <= t < n_steps:
                continue
            stages[k](t, (parity - k) % 2)

    fill_end = min(depth - 1, n_steps)
    for i in range(fill_end):
        iteration(i, i % 2)
    n_pairs = (n_steps - fill_end) // 2

    def pair(m, carry):
        i = fill_end + 2 * m
        iteration(i, fill_end % 2)
        iteration(i + 1, (fill_end + 1) % 2)
        return carry

    lax.fori_loop(0, n_pairs, pair, 0)
    for i in range(fill_end + 2 * n_pairs, n_steps + depth - 1):
        iteration(i, i % 2)


def _swa_kernel(bounded_ref, q_ref, k_ref, v_ref, bias_ref, sink_ref, o_ref, s_ref, p_ref, d_ref):
    hk = pl.program_id(0)
    nblk = q_ref.shape[1]
    w = SWA_BLOCK
    sink = sink_ref[0]

    def window_start(n):
        return jnp.maximum(n - 1, 0)

    def scores(n, slot):
        qblk = q_ref[0, n]
        qg = jnp.concatenate([qblk[g * HEAD_DIM:(g + 1) * HEAD_DIM, :] for g in range(SWA_GROUP)], axis=1)
        zero = jnp.zeros_like(qg)
        qpad = jnp.concatenate([jnp.where(hk == 0, qg, zero), jnp.where(hk == 1, qg, zero)], axis=0)
        kwin = k_ref[0, pl.ds(pl.multiple_of(window_start(n) * w, w), 2 * w), :]
        s_ref[slot] = jnp.dot(kwin, qpad, preferred_element_type=F32)

    def probs(n, slot, *, bounded):
        t = s_ref[slot] + bias_ref[jnp.minimum(n, 1)]
        if bounded:
            e = jnp.exp2(t)
            d_ref[slot] = jnp.sum(e, axis=0, keepdims=True) + jnp.exp2(sink)
        else:
            m = jnp.maximum(jnp.max(t, axis=0, keepdims=True), sink)
            e = jnp.exp2(t - m)
            d_ref[slot] = jnp.sum(e, axis=0, keepdims=True) + jnp.exp2(sink - m)
        p_ref[slot] = e.astype(BF16)

    def weighted_values(n, slot):
        first = window_start(n)
        vwin = jnp.concatenate([v_ref[0, first], v_ref[0, first + 1]], axis=1)
        o = jnp.dot(vwin, p_ref[slot], preferred_element_type=F32) * (1.0 / d_ref[slot])
        for gp in range(SWA_GROUP // 2):
            pair = jnp.concatenate([o[:, (2 * gp) * w:(2 * gp + 1) * w],
                                    o[:, (2 * gp + 1) * w:(2 * gp + 2) * w]], axis=0)
            o_ref[0, pl.ds(pl.multiple_of(n * w, w), w), gp * 2 * HEAD_DIM:(gp + 1) * 2 * HEAD_DIM] = (
                pair.T.astype(BF16))

    @pl.when(bounded_ref[0] != 0)
    def _():
        _staged_pipeline(nblk, [scores, functools.partial(probs, bounded=True), weighted_values])

    @pl.when(bounded_ref[0] == 0)
    def _():
        def block(n, carry):
            scores(n, 0)
            probs(n, 0, bounded=False)
            weighted_values(n, 0)
            return carry

        lax.fori_loop(0, nblk, block, 0)


def _swa_bias(slopes):
    w = SWA_BLOCK
    kj = np.arange(2 * w)[:, None]
    qi = np.arange(w)[None, :]
    out = np.empty((2, SWA_KV_HEADS, 2 * w, SWA_GROUP * w), np.float32)
    for first, dist in ((0, qi - kj), (1, qi - kj + w)):
        valid = (dist >= 0) & (dist < w)
        for hk in range(SWA_KV_HEADS):
            for g in range(SWA_GROUP):
                sl = slopes[hk * SWA_GROUP + g]
                out[first, hk, :, g * w:(g + 1) * w] = np.where(valid, -sl * dist, -np.inf)
    return out


def _swa_attention(qa_t, ka, va_t, sinks, slopes, logit_bound):
    b, nblk, n_qa, w = qa_t.shape
    s = nblk * w
    gw = SWA_GROUP * w
    bias = jnp.asarray(_swa_bias(slopes) * LOG2E)
    sinks = sinks.astype(F32)
    sink_cols = jnp.repeat((sinks * LOG2E).reshape(SWA_KV_HEADS, 1, SWA_GROUP), w, axis=-1)
    bounded = jnp.asarray(jnp.maximum(logit_bound, jnp.max(jnp.abs(sinks))) <= MAX_UNSHIFTED_LOGIT, jnp.int32)
    return pl.pallas_call(
        _swa_kernel,
        grid=(SWA_KV_HEADS, b),
        in_specs=[
            pl.BlockSpec(memory_space=pltpu.SMEM),
            pl.BlockSpec((1, nblk, SWA_GROUP * HEAD_DIM, w), lambda hk, bi: (bi, 0, hk, 0)),
            pl.BlockSpec((1, s, SWA_KV_HEADS * HEAD_DIM), lambda hk, bi: (bi, 0, 0)),
            pl.BlockSpec((1, nblk, HEAD_DIM, w), lambda hk, bi: (bi, 0, hk, 0)),
            pl.BlockSpec((2, None, 2 * w, gw), lambda hk, bi: (0, hk, 0, 0)),
            pl.BlockSpec((1, 1, gw), lambda hk, bi: (hk, 0, 0)),
        ],
        out_specs=pl.BlockSpec((1, s, SWA_GROUP * HEAD_DIM), lambda hk, bi: (bi, 0, hk)),
        out_shape=jax.ShapeDtypeStruct((b, s, n_qa), BF16),
        scratch_shapes=[pltpu.VMEM((2, 2 * w, gw), F32), pltpu.VMEM((2, 2 * w, gw), BF16),
                        pltpu.VMEM((2, 1, gw), F32)],
        compiler_params=pltpu.CompilerParams(
            dimension_semantics=("parallel", "parallel"), vmem_limit_bytes=VMEM_LIMIT_BYTES),
        name="swa_attn",
    )(bounded.reshape(1), qa_t, ka, va_t, bias, sink_cols)


def _diff_kernel(bounded_ref, iq_tbl_ref, j_tbl_ref, q_ref, qpos_ref, k_ref, kpos_ref, v_ref, mask_ref, lam_ref,
                 gsub_ref, o_ref, acc_ref, l_ref, p_ref, *, lam_init):
    nt, t_q = q_ref.shape[1], q_ref.shape[3]
    hpb = acc_ref.shape[0]
    n_plain = nt * (nt - 1) // 2
    dv = DIFF_V_DIM
    lp = lam_ref[...]
    lam = (jnp.exp(jnp.sum(lp[0:1] * lp[1:2], axis=-1, keepdims=True))
           - jnp.exp(jnp.sum(lp[2:3] * lp[3:4], axis=-1, keepdims=True)) + lam_init)
    zero = jnp.zeros((HEAD_DIM, t_q), BF16)

    def padded_q(iq, hh):
        q = q_ref[0, iq, hh * dv:(hh + 1) * dv, :]
        return jnp.concatenate([jnp.concatenate([q[:HEAD_DIM], zero], axis=0),
                                jnp.concatenate([zero, q[HEAD_DIM:]], axis=0)], axis=1)

    def logits(iq, j, hh, masked):
        rows = pl.ds(pl.multiple_of(j * t_q, t_q), t_q)
        lhs = jnp.concatenate([k_ref[0, rows, hh * dv:(hh + 1) * dv], kpos_ref[rows, :]], axis=1)
        rhs = jnp.concatenate([padded_q(iq, hh), qpos_ref[iq, hh * dv:(hh + 1) * dv, :]], axis=0)
        s = jnp.dot(lhs, rhs, preferred_element_type=F32)
        return s + mask_ref[...] if masked else s

    def finalize(iq, hh, acc, l):
        acc = acc * (1.0 / l)
        od = acc[:, :t_q] - lam * acc[:, t_q:]
        ms = jnp.mean(od * od, axis=0, keepdims=True)
        y = od * lax.rsqrt(ms + EPS) * gsub_ref[...]
        o_ref[0, pl.ds(pl.multiple_of(iq * t_q, t_q), t_q), hh * dv:(hh + 1) * dv] = y.T.astype(BF16)

    def probs(t, slot, *, base, masked):
        iq, j = iq_tbl_ref[base + t], j_tbl_ref[base + t]
        for hh in range(hpb):
            p = jnp.exp2(logits(iq, j, hh, masked))
            l_ref[hh, iq] += jnp.sum(p.reshape(t_q // 8, 8, 2 * t_q), axis=0)
            p_ref[slot, hh] = p.astype(BF16)

    def pv(t, slot, *, base):
        iq, j = iq_tbl_ref[base + t], j_tbl_ref[base + t]
        for hh in range(hpb):
            acc_ref[hh, iq] += jnp.dot(v_ref[0, j, hh * dv:(hh + 1) * dv, :], p_ref[slot, hh],
                                       preferred_element_type=F32)

    @pl.when(bounded_ref[0] != 0)
    def _():
        acc_ref[...] = jnp.zeros_like(acc_ref)
        l_ref[...] = jnp.zeros_like(l_ref)
        _staged_pipeline(n_plain, [functools.partial(probs, base=0, masked=False),
                                   functools.partial(pv, base=0)])
        _staged_pipeline(nt, [functools.partial(probs, base=n_plain, masked=True),
                              functools.partial(pv, base=n_plain)])

        def fin(iq, carry):
            for hh in range(hpb):
                finalize(iq, hh, acc_ref[hh, iq], jnp.sum(l_ref[hh, iq], axis=0, keepdims=True))
            return carry

        lax.fori_loop(0, nt, fin, 0)

    @pl.when(bounded_ref[0] == 0)
    def _():
        def q_block(iq, carry):
            acc_ref[:, 0] = jnp.zeros((hpb, dv, 2 * t_q), F32)

            def step(j, stats, masked):
                out = []
                for hh in range(hpb):
                    m, l = stats[2 * hh], stats[2 * hh + 1]
                    s = logits(iq, j, hh, masked)
                    mnew = jnp.maximum(m, jnp.max(s, axis=0, keepdims=True))
                    alpha = jnp.exp2(m - mnew)
                    p = jnp.exp2(s - mnew)
                    out += [mnew, alpha * l + jnp.sum(p, axis=0, keepdims=True)]
                    acc_ref[hh, 0] = alpha * acc_ref[hh, 0] + jnp.dot(
                        v_ref[0, j, hh * dv:(hh + 1) * dv, :], p.astype(BF16), preferred_element_type=F32)
                return tuple(out)

            init = (jnp.full((1, 2 * t_q), NEG_INF, F32), jnp.zeros((1, 2 * t_q), F32)) * hpb
            stats = lax.fori_loop(0, iq, lambda j, c: step(j, c, False), init)
            stats = step(iq, stats, True)
            for hh in range(hpb):
                finalize(iq, hh, acc_ref[hh, 0], stats[2 * hh + 1])
            return carry

        lax.fori_loop(0, nt, q_block, 0)


def _split_bf16(x, parts=3):
    out, rest = [], np.asarray(x, np.float64)
    for _ in range(parts):
        piece = rest.astype(BF16).astype(np.float64)
        out.append(piece)
        rest = rest - piece
    return out


def _alibi_features(slopes, s, t):
    ROW = DIFF_V_DIM
    kpos = np.arange(s)
    kfeat = np.zeros((s, ROW), np.float64)
    qfeat = np.zeros((s // t, len(slopes) * ROW, 2 * t), np.float64)
    qpos = (np.arange(s // t)[:, None] * t + np.arange(2 * t)[None, :] % t).astype(np.float64)
    for h, slope in enumerate(slopes):
        for i, piece in enumerate(_split_bf16(slope * LOG2E)):
            c = 5 * i
            kfeat[:, c], kfeat[:, c + 1], kfeat[:, c + 2:c + 5] = kpos // SWA_BLOCK, kpos % SWA_BLOCK, 1.0
            r = h * ROW + c
            qfeat[:, r], qfeat[:, r + 1] = float(piece) * SWA_BLOCK, float(piece)
            for n, part in enumerate(_split_bf16(float(piece) * qpos)):
                qfeat[:, r + 2 + n] = -part
    return jnp.asarray(kfeat, BF16), jnp.asarray(qfeat, BF16)


def _diff_attention(qd_t, kd, vd_t, lam_params, g_sub, slopes, lam_init, logit_bound, hpb=DIFF_HEADS_PER_STEP):
    b, nt, n_qd, t = qd_t.shape
    s = nt * t
    nh = n_qd // DIFF_V_DIM
    kfeat, qfeat = _alibi_features(slopes, s, t)
    causal = np.arange(t)[:, None] <= np.arange(2 * t)[None, :] % t
    mask = jnp.asarray(np.where(causal, 0.0, -np.inf), F32)
    gsub = jnp.broadcast_to((g_sub * (1.0 - lam_init))[:, None], (DIFF_V_DIM, t)).astype(F32)
    bounded = jnp.asarray(logit_bound <= MAX_UNSHIFTED_LOGIT, jnp.int32)
    pairs = [(iq, j) for iq in range(nt) for j in range(iq)] + [(iq, iq) for iq in range(nt)]
    iq_tbl = jnp.asarray([pq[0] for pq in pairs], jnp.int32)
    j_tbl = jnp.asarray([pq[1] for pq in pairs], jnp.int32)
    kern = functools.partial(_diff_kernel, lam_init=lam_init)
    return pl.pallas_call(
        kern,
        grid=(nh // hpb, b),
        in_specs=[
            pl.BlockSpec(memory_space=pltpu.SMEM),
            pl.BlockSpec(memory_space=pltpu.SMEM),
            pl.BlockSpec(memory_space=pltpu.SMEM),
            pl.BlockSpec((1, nt, hpb * DIFF_V_DIM, t), lambda hg, bi: (bi, 0, hg, 0)),
            pl.BlockSpec((nt, hpb * DIFF_V_DIM, 2 * t), lambda hg, bi: (0, hg, 0)),
            pl.BlockSpec((1, s, hpb * DIFF_V_DIM), lambda hg, bi: (bi, 0, hg)),
            pl.BlockSpec((s, DIFF_V_DIM), lambda hg, bi: (0, 0)),
            pl.BlockSpec((1, nt, hpb * DIFF_V_DIM, t), lambda hg, bi: (bi, 0, hg, 0)),
            pl.BlockSpec((t, 2 * t), lambda hg, bi: (0, 0)),
            pl.BlockSpec((4, HEAD_DIM), lambda hg, bi: (0, 0)),
            pl.BlockSpec((DIFF_V_DIM, t), lambda hg, bi: (0, 0)),
        ],
        out_specs=pl.BlockSpec((1, s, hpb * DIFF_V_DIM), lambda hg, bi: (bi, 0, hg)),
        out_shape=jax.ShapeDtypeStruct((b, s, n_qd), BF16),
        scratch_shapes=[pltpu.VMEM((hpb, nt, DIFF_V_DIM, 2 * t), F32),
                        pltpu.VMEM((hpb, nt, 8, 2 * t), F32),
                        pltpu.VMEM((2, hpb, t, 2 * t), BF16)],
        compiler_params=pltpu.CompilerParams(
            dimension_semantics=("parallel", "parallel"), vmem_limit_bytes=VMEM_LIMIT_BYTES),
        name="diff_attn",
    )(bounded.reshape(1), iq_tbl, j_tbl, qd_t, qfeat, kd, kfeat, vd_t, mask, lam_params, gsub)


def _outproj_kernel(x_ref, ya_ref, yd_ref, wa_ref, wd_ref, o_ref):
    o_ref[...] = (x_ref[...]
                  + jnp.dot(ya_ref[...], wa_ref[...], preferred_element_type=F32)
                  + jnp.dot(yd_ref[...], wd_ref[...], preferred_element_type=F32))


def _outproj(x2, ya2, yd2, wa, wd, tm=512):
    n, d = x2.shape
    tm = min(tm, n)
    assert n % tm == 0
    const = lambda shape: _resident(shape, lambda i: (0, 0))
    return pl.pallas_call(
        _outproj_kernel,
        grid=(n // tm,),
        in_specs=[
            pl.BlockSpec((tm, d), lambda i: (i, 0)),
            pl.BlockSpec((tm, ya2.shape[1]), lambda i: (i, 0)),
            pl.BlockSpec((tm, yd2.shape[1]), lambda i: (i, 0)),
            const(wa.shape), const(wd.shape),
        ],
        out_specs=pl.BlockSpec((tm, d), lambda i: (i, 0)),
        out_shape=jax.ShapeDtypeStruct((n, d), F32),
        compiler_params=pltpu.CompilerParams(
            dimension_semantics=("parallel",), vmem_limit_bytes=VMEM_LIMIT_BYTES),
        name="outproj",
    )(x2, ya2, yd2, wa, wd)


def _ffn_kernel(h_ref, g_ref, wg_ref, wu_ref, wd_ref, o_ref, u_ref):
    f = pl.program_id(1)

    @pl.when(f == 0)
    def _():
        h = h_ref[...]
        u_ref[...] = _rms_rows(h, g_ref[...]).astype(BF16)
        o_ref[...] = h

    u = u_ref[...]
    gate = jnp.dot(u, wg_ref[...], preferred_element_type=F32)
    up = jnp.dot(u, wu_ref[...], preferred_element_type=F32)
    act = (gate * (1.0 / (1.0 + jnp.exp(-gate))) * up).astype(BF16)
    o_ref[...] += jnp.dot(act, wd_ref[...], preferred_element_type=F32)


def _ffn(h2, g_ffn, wg, wu, wd, tm=1024, tf=512):
    n, d = h2.shape
    tm = min(tm, n)
    assert n % tm == 0
    dff = wg.shape[1]
    return pl.pallas_call(
        _ffn_kernel,
        grid=(n // tm, dff // tf),
        in_specs=[
            pl.BlockSpec((tm, d), lambda i, f: (i, 0)),
            _resident((1, d), lambda i, f: (0, 0)),
            pl.BlockSpec((d, tf), lambda i, f: (0, f)),
            pl.BlockSpec((d, tf), lambda i, f: (0, f)),
            pl.BlockSpec((tf, d), lambda i, f: (f, 0)),
        ],
        out_specs=pl.BlockSpec((tm, d), lambda i, f: (i, 0)),
        out_shape=jax.ShapeDtypeStruct((n, d), F32),
        scratch_shapes=[pltpu.VMEM((tm, d), BF16)],
        compiler_params=pltpu.CompilerParams(
            dimension_semantics=("parallel", "arbitrary"), vmem_limit_bytes=VMEM_LIMIT_BYTES),
        name="ffn",
    )(h2, g_ffn[None, :], wg, wu, wd)


def _ple_kernel(h_ref, p_ref, g_ref, wg_ref, wp_ref, go_ref, o_ref, *, sub, ncol):
    d = h_ref.shape[1]
    for st in range(h_ref.shape[0] // sub):
        rows = slice(st * sub, (st + 1) * sub)
        u = _rms_rows(h_ref[rows, :], g_ref[...]).astype(BF16)
        pp = jnp.dot(p_ref[rows, :].astype(BF16), wp_ref[...], preferred_element_type=F32)
        ppn = _rms_rows(pp, go_ref[...])
        for c in range(d // ncol):
            cols = slice(c * ncol, (c + 1) * ncol)
            z = jnp.dot(u, wg_ref[:, cols], preferred_element_type=F32)
            gate = 1.0 / (1.0 + jnp.exp(-z))
            o_ref[rows, cols] = h_ref[rows, cols] + gate * ppn[:, cols]


def _ple(h2, p2, g_ple, wg, wp, g_out, tm=1024, sub=512, ncol=512):
    n, d = h2.shape
    tm = min(tm, n)
    assert n % tm == 0
    sub = min(sub, tm)
    assert tm % sub == 0
    const = lambda shape: _resident(shape, lambda i: (0, 0))
    return pl.pallas_call(
        functools.partial(_ple_kernel, sub=sub, ncol=min(ncol, d)),
        grid=(n // tm,),
        in_specs=[
            pl.BlockSpec((tm, d), lambda i: (i, 0)),
            pl.BlockSpec((tm, p2.shape[1]), lambda i: (i, 0)),
            const((1, d)), const(wg.shape), const(wp.shape), const((1, d)),
        ],
        out_specs=pl.BlockSpec((tm, d), lambda i: (i, 0)),
        out_shape=jax.ShapeDtypeStruct((n, d), F32),
        compiler_params=pltpu.CompilerParams(
            dimension_semantics=("parallel",), vmem_limit_bytes=VMEM_LIMIT_BYTES),
        name="ple",
    )(h2, p2, g_ple[None, :], wg, wp, g_out[None, :])


def _alibi_slopes(n):
    return [2.0 ** (-8.0 * (h + 1) / n) for h in range(n)]


def _logit_bound(q_gain, k_gain):
    return 1.02 * math.sqrt(HEAD_DIM) * jnp.max(jnp.abs(q_gain)) * jnp.max(jnp.abs(k_gain))


def kernel(x, p, g_attn, w_in, qn_swa, kn_swa, sinks, qn_diff, kn_diff, lambda_q1, lambda_k1, lambda_q2,
           lambda_k2, g_sub, w_out, g_ffn, w_gate, w_up, w_down, g_ple, w_ple_gate, w_ple_proj, g_ple_out):
    b, s, d = x.shape
    depth = p.shape[0]
    n_qa = d // 2
    n_ka = n_va = SWA_KV_HEADS * HEAD_DIM
    n_qd = n_kd = n_vd = d // 2
    diff_heads = n_vd // DIFF_V_DIM
    swa_heads = n_qa // HEAD_DIM
    assert swa_heads == SWA_KV_HEADS * SWA_GROUP and s % TOKEN_TILE == 0
    c = np.cumsum([0, n_qa, n_ka, n_va, n_qd, n_kd, n_vd])
    h = x
    for i in range(depth):
        lam_init = 0.8 - 0.6 * math.exp(-0.3 * i)
        w = w_in[i]
        col = lambda k: w[:, c[k]:c[k + 1]]
        wfm = _transpose_cast_columns(w, [(c[k], c[k + 1]) for k in (0, 2, 3, 5)])
        wtm = jnp.concatenate([col(1), col(4)], axis=1).astype(BF16)
        qa_t, va_t, qd_t, vd_t, ka, kd = _inproj(
            h, g_attn[i], wfm, wtm, qn_swa[i], kn_swa[i], qn_diff[i], kn_diff[i],
            (n_qa, n_ka, n_va, n_qd, n_kd, n_vd))
        ya = _swa_attention(qa_t, ka, va_t, sinks[i], _alibi_slopes(swa_heads),
                            _logit_bound(qn_swa[i], kn_swa[i]))
        lam_params = jnp.stack([lambda_q1[i], lambda_k1[i], lambda_q2[i], lambda_k2[i]]).astype(F32)
        yd = _diff_attention(qd_t, kd, vd_t, lam_params, g_sub[i], _alibi_slopes(diff_heads), lam_init,
                             _logit_bound(qn_diff[i], kn_diff[i]))
        wo = w_out[i].astype(BF16)
        h2 = _outproj(h.reshape(b * s, d), ya.reshape(b * s, n_qa), yd.reshape(b * s, n_vd),
                      wo[:n_qa], wo[n_qa:])
        h2 = _ffn(h2, g_ffn[i], w_gate[i].astype(BF16), w_up[i].astype(BF16), w_down[i].astype(BF16))
        h2 = _ple(h2, p[i].reshape(b * s, -1), g_ple[i], w_ple_gate[i].astype(BF16),
                  w_ple_proj[i].astype(BF16), g_ple_out[i])
        h = h2.reshape(b, s, d)
    return h
```

```python
import functools
import math

import jax
import jax.numpy as jnp
import numpy as np
from jax import lax
from jax.experimental import pallas as pl
from jax.experimental.pallas import tpu as pltpu

F32 = jnp.float32
BF16 = jnp.bfloat16

HEAD_DIM = 64
SWA_BLOCK = 128
SWA_KV_HEADS = 2
SWA_GROUP = 8
DIFF_V_DIM = 2 * HEAD_DIM
EPS = 1e-6
NEG_INF = float("-inf")
LOG2E = math.log2(math.e)
MAX_UNSHIFTED_LOGIT = 60.0

TOKEN_TILE = 256
INPROJ_TILES_PER_STEP = 4
DIFF_HEADS_PER_STEP = 4
VMEM_LIMIT_BYTES = 56 * 1024 * 1024

_NT = (((1,), (1,)), ((), ()))


def _resident(shape, index_map):
    return pl.BlockSpec(shape, index_map, pipeline_mode=pl.Buffered(1))


def _rms_rows(x, gain):
    ms = jnp.mean(x * x, axis=-1, keepdims=True)
    return x * lax.rsqrt(ms + EPS) * gain


def _transpose_cast_kernel(blk_ref, w_ref, o_ref):
    del blk_ref
    o_ref[...] = w_ref[...].T.astype(BF16)


def _transpose_cast_columns(w, col_ranges, cb=128):
    k, _ = w.shape
    blocks = [c0 // cb + i for c0, c1 in col_ranges for i in range((c1 - c0) // cb)]
    assert all(c0 % cb == 0 and c1 % cb == 0 for c0, c1 in col_ranges)
    return pl.pallas_call(
        _transpose_cast_kernel,
        grid_spec=pltpu.PrefetchScalarGridSpec(
            num_scalar_prefetch=1,
            grid=(len(blocks),),
            in_specs=[pl.BlockSpec((k, cb), lambda i, blk: (0, blk[i]))],
            out_specs=pl.BlockSpec((cb, k), lambda i, blk: (i, 0)),
        ),
        out_shape=jax.ShapeDtypeStruct((len(blocks) * cb, k), BF16),
        compiler_params=pltpu.CompilerParams(dimension_semantics=("parallel",)),
        name="weight_layout",
    )(jnp.asarray(blocks, jnp.int32), w)


def _inproj_kernel(x_ref, g_ref, wfm_ref, wtm_ref, gqa_ref, gqd_ref, gka_ref, gkd_ref, bd_ref,
                   qa_ref, va_ref, qd_ref, vd_ref, ka_ref, kd_ref, *, n_qa, n_va, n_qd, n_vd, n_ka, n_kd):
    tm = TOKEN_TILE
    per_tile = tm // SWA_BLOCK

    def headnorm_fm(z, gain, store):
        for h in range(z.shape[0] // HEAD_DIM):
            zh = z[h * HEAD_DIM:(h + 1) * HEAD_DIM, :]
            ms = jnp.mean(zh * zh, axis=0, keepdims=True)
            store(h, (zh * lax.rsqrt(ms + EPS) * gain).astype(BF16))

    def headnorm_tm(z, gain):
        ncols = z.shape[1]
        ssq = jnp.dot((z * z).astype(BF16), bd_ref[0:ncols, 0:ncols], preferred_element_type=F32)
        return (z * lax.rsqrt(ssq * (1.0 / HEAD_DIM) + EPS) * gain).astype(BF16)

    for st in range(x_ref.shape[0] // tm):
        rows = slice(st * tm, (st + 1) * tm)
        u = _rms_rows(x_ref[rows, :], g_ref[...]).astype(BF16)

        def fm(row0, nrows):
            return lax.dot_general(wfm_ref[row0:row0 + nrows, :], u, _NT, preferred_element_type=F32)

        def store_qa(h, val):
            for t in range(per_tile):
                qa_ref[0, st * per_tile + t, h * HEAD_DIM:(h + 1) * HEAD_DIM, :] = (
                    val[:, t * SWA_BLOCK:(t + 1) * SWA_BLOCK])

        def store_qd(h, val):
            qd_ref[0, st, h * HEAD_DIM:(h + 1) * HEAD_DIM, :] = val

        r = 0
        headnorm_fm(fm(r, n_qa), gqa_ref[...], store_qa)
        r += n_qa
        zva = fm(r, n_va).astype(BF16)
        for t in range(per_tile):
            va_ref[0, st * per_tile + t] = zva[:, t * SWA_BLOCK:(t + 1) * SWA_BLOCK]
        r += n_va
        headnorm_fm(fm(r, n_qd), gqd_ref[...], store_qd)
        r += n_qd
        vd_ref[0, st] = fm(r, n_vd).astype(BF16)

        zk = jnp.dot(u, wtm_ref[...], preferred_element_type=F32)
        ka_ref[0, rows, :] = headnorm_tm(zk[:, :n_ka], gka_ref[...])
        cw = bd_ref.shape[0]
        for c in range(n_kd // cw):
            kd_ref[0, rows, c * cw:(c + 1) * cw] = headnorm_tm(
                zk[:, n_ka + c * cw:n_ka + (c + 1) * cw], gkd_ref[...])


def _inproj(x, g_attn, wfm, wtm, qn_swa, kn_swa, qn_diff, kn_diff, dims):
    b, s, d = x.shape
    n_qa, n_ka, n_va, n_qd, n_kd, n_vd = dims
    tm = TOKEN_TILE
    nt = s // tm
    tps = min(INPROJ_TILES_PER_STEP, nt)
    assert nt % tps == 0
    blk = tps * tm
    scale = LOG2E / math.sqrt(HEAD_DIM)
    gqa = jnp.broadcast_to((qn_swa * scale)[:, None], (HEAD_DIM, tm)).astype(F32)
    gqd = jnp.broadcast_to((qn_diff * scale)[:, None], (HEAD_DIM, tm)).astype(F32)
    gka = jnp.tile(kn_swa, n_ka // HEAD_DIM)[None, :].astype(F32)
    gkd = jnp.tile(kn_diff, 256 // HEAD_DIM)[None, :].astype(F32)
    idx = np.arange(256) // HEAD_DIM
    bd = jnp.asarray(idx[:, None] == idx[None, :], dtype=BF16)
    const = lambda shape: _resident(shape, lambda bi, ti: (0,) * len(shape))
    kern = functools.partial(_inproj_kernel, n_qa=n_qa, n_va=n_va, n_qd=n_qd, n_vd=n_vd, n_ka=n_ka, n_kd=n_kd)
    return pl.pallas_call(
        kern,
        grid=(b, nt // tps),
        in_specs=[
            pl.BlockSpec((None, blk, d), lambda bi, ti: (bi, ti, 0)),
            const((1, d)), const(wfm.shape), const(wtm.shape),
            const((HEAD_DIM, tm)), const((HEAD_DIM, tm)), const((1, n_ka)), const((1, 256)), const((256, 256)),
        ],
        out_specs=[
            pl.BlockSpec((1, blk // SWA_BLOCK, n_qa, SWA_BLOCK), lambda bi, ti: (bi, ti, 0, 0)),
            pl.BlockSpec((1, blk // SWA_BLOCK, n_va, SWA_BLOCK), lambda bi, ti: (bi, ti, 0, 0)),
            pl.BlockSpec((1, tps, n_qd, tm), lambda bi, ti: (bi, ti, 0, 0)),
            pl.BlockSpec((1, tps, n_vd, tm), lambda bi, ti: (bi, ti, 0, 0)),
            pl.BlockSpec((1, blk, n_ka), lambda bi, ti: (bi, ti, 0)),
            pl.BlockSpec((1, blk, n_kd), lambda bi, ti: (bi, ti, 0)),
        ],
        out_shape=[
            jax.ShapeDtypeStruct((b, s // SWA_BLOCK, n_qa, SWA_BLOCK), BF16),
            jax.ShapeDtypeStruct((b, s // SWA_BLOCK, n_va, SWA_BLOCK), BF16),
            jax.ShapeDtypeStruct((b, nt, n_qd, tm), BF16),
            jax.ShapeDtypeStruct((b, nt, n_vd, tm), BF16),
            jax.ShapeDtypeStruct((b, s, n_ka), BF16),
            jax.ShapeDtypeStruct((b, s, n_kd), BF16),
        ],
        compiler_params=pltpu.CompilerParams(
            dimension_semantics=("parallel", "parallel"), vmem_limit_bytes=VMEM_LIMIT_BYTES),
        name="inproj",
    )(x, g_attn[None, :], wfm, wtm, gqa, gqd, gka, gkd, bd)


def _staged_pipeline(n_steps, stages):
    depth = len(stages)

    def iteration(i, parity):
        for k in reversed(range(depth)):
            t = i - k
            if isinstance(i, int) and not 0 <= t < n_steps:
                continue
            stages[k](t, (parity - k) % 2)

    fill_end = min(depth - 1, n_steps)
    for i in range(fill_end):
        iteration(i, i % 2)
    n_pairs = (n_steps - fill_end) // 2

    def pair(m, carry):
        i = fill_end + 2 * m
        iteration(i, fill_end % 2)
        iteration(i + 1, (fill_end + 1) % 2)
        return carry

    lax.fori_loop(0, n_pairs, pair, 0)
    for i in range(fill_end + 2 * n_pairs, n_steps + depth - 1):
        iteration(i, i % 2)


def _swa_kernel(bounded_ref, q_ref, k_ref, v_ref, bias_ref, sink_ref, *refs, n_cast):
    cast_in, o_ref, cast_out = refs[:n_cast], refs[n_cast], refs[n_cast + 1:2 * n_cast + 1]
    s_ref, p_ref, d_ref = refs[2 * n_cast + 1:]
    for src, dst in zip(cast_in, cast_out):
        dst[...] = src[...].astype(BF16)

    hk = pl.program_id(0)
    nblk = q_ref.shape[1]
    w = SWA_BLOCK
    sink = sink_ref[0]

    def window_start(n):
        return jnp.maximum(n - 1, 0)

    def scores(n, slot):
        qblk = q_ref[0, n]
        qg = jnp.concatenate([qblk[g * HEAD_DIM:(g + 1) * HEAD_DIM, :] for g in range(SWA_GROUP)], axis=1)
        zero = jnp.zeros_like(qg)
        qpad = jnp.concatenate([jnp.where(hk == 0, qg, zero), jnp.where(hk == 1, qg, zero)], axis=0)
        kwin = k_ref[0, pl.ds(pl.multiple_of(window_start(n) * w, w), 2 * w), :]
        s_ref[slot] = jnp.dot(kwin, qpad, preferred_element_type=F32)

    def probs(n, slot, *, bounded):
        t = s_ref[slot] + bias_ref[jnp.minimum(n, 1)]
        if bounded:
            e = jnp.exp2(t)
            d_ref[slot] = jnp.sum(e, axis=0, keepdims=True) + jnp.exp2(sink)
        else:
            m = jnp.maximum(jnp.max(t, axis=0, keepdims=True), sink)
            e = jnp.exp2(t - m)
            d_ref[slot] = jnp.sum(e, axis=0, keepdims=True) + jnp.exp2(sink - m)
        p_ref[slot] = e.astype(BF16)

    def weighted_values(n, slot):
        first = window_start(n)
        vwin = jnp.concatenate([v_ref[0, first], v_ref[0, first + 1]], axis=1)
        o = jnp.dot(vwin, p_ref[slot], preferred_element_type=F32) * (1.0 / d_ref[slot])
        for gp in range(SWA_GROUP // 2):
            pair = jnp.concatenate([o[:, (2 * gp) * w:(2 * gp + 1) * w],
                                    o[:, (2 * gp + 1) * w:(2 * gp + 2) * w]], axis=0)
            o_ref[0, pl.ds(pl.multiple_of(n * w, w), w), gp * 2 * HEAD_DIM:(gp + 1) * 2 * HEAD_DIM] = (
                pair.T.astype(BF16))

    @pl.when(bounded_ref[0] != 0)
    def _():
        _staged_pipeline(nblk, [scores, functools.partial(probs, bounded=True), weighted_values])

    @pl.when(bounded_ref[0] == 0)
    def _():
        def block(n, carry):
            scores(n, 0)
            probs(n, 0, bounded=False)
            weighted_values(n, 0)
            return carry

        lax.fori_loop(0, nblk, block, 0)


def _swa_bias(slopes):
    w = SWA_BLOCK
    kj = np.arange(2 * w)[:, None]
    qi = np.arange(w)[None, :]
    out = np.empty((2, SWA_KV_HEADS, 2 * w, SWA_GROUP * w), np.float32)
    for first, dist in ((0, qi - kj), (1, qi - kj + w)):
        valid = (dist >= 0) & (dist < w)
        for hk in range(SWA_KV_HEADS):
            for g in range(SWA_GROUP):
                sl = slopes[hk * SWA_GROUP + g]
                out[first, hk, :, g * w:(g + 1) * w] = np.where(valid, -sl * dist, -np.inf)
    return out


def _swa_attention(qa_t, ka, va_t, sinks, slopes, logit_bound, f32_weights=()):
    b, nblk, n_qa, w = qa_t.shape
    s = nblk * w
    gw = SWA_GROUP * w
    bias = jnp.asarray(_swa_bias(slopes) * LOG2E)
    sinks = sinks.astype(F32)
    sink_cols = jnp.repeat((sinks * LOG2E).reshape(SWA_KV_HEADS, 1, SWA_GROUP), w, axis=-1)
    bounded = jnp.asarray(jnp.maximum(logit_bound, jnp.max(jnp.abs(sinks))) <= MAX_UNSHIFTED_LOGIT, jnp.int32)
    n_steps = SWA_KV_HEADS * b
    if any(wt.shape[0] % (16 * n_steps) for wt in f32_weights):
        ya, _ = _swa_attention(qa_t, ka, va_t, sinks, slopes, logit_bound)
        return ya, tuple(wt.astype(BF16) for wt in f32_weights)
    slab_specs = [pl.BlockSpec((wt.shape[0] // n_steps, wt.shape[1]), lambda hk, bi: (hk * b + bi, 0))
                  for wt in f32_weights]
    outs = pl.pallas_call(
        functools.partial(_swa_kernel, n_cast=len(f32_weights)),
        grid=(SWA_KV_HEADS, b),
        in_specs=[
            pl.BlockSpec(memory_space=pltpu.SMEM),
            pl.BlockSpec((1, nblk, SWA_GROUP * HEAD_DIM, w), lambda hk, bi: (bi, 0, hk, 0)),
            pl.BlockSpec((1, s, SWA_KV_HEADS * HEAD_DIM), lambda hk, bi: (bi, 0, 0)),
            pl.BlockSpec((1, nblk, HEAD_DIM, w), lambda hk, bi: (bi, 0, hk, 0)),
            pl.BlockSpec((2, None, 2 * w, gw), lambda hk, bi: (0, hk, 0, 0)),
            pl.BlockSpec((1, 1, gw), lambda hk, bi: (hk, 0, 0)),
        ] + slab_specs,
        out_specs=[pl.BlockSpec((1, s, SWA_GROUP * HEAD_DIM), lambda hk, bi: (bi, 0, hk))] + slab_specs,
        out_shape=[jax.ShapeDtypeStruct((b, s, n_qa), BF16)]
                  + [jax.ShapeDtypeStruct(wt.shape, BF16) for wt in f32_weights],
        scratch_shapes=[pltpu.VMEM((2, 2 * w, gw), F32), pltpu.VMEM((2, 2 * w, gw), BF16),
                        pltpu.VMEM((2, 1, gw), F32)],
        compiler_params=pltpu.CompilerParams(
            dimension_semantics=("parallel", "parallel"), vmem_limit_bytes=VMEM_LIMIT_BYTES),
        name="swa_attn",
    )(bounded.reshape(1), qa_t, ka, va_t, bias, sink_cols, *f32_weights)
    return outs[0], outs[1:]


def _diff_kernel(bounded_ref, iq_tbl_ref, j_tbl_ref, q_ref, qpos_ref, k_ref, kpos_ref, v_ref, mask_ref, lam_ref,
                 gsub_ref, o_ref, acc_ref, l_ref, p_ref, *, lam_init):
    nt, t_q = q_ref.shape[1], q_ref.shape[3]
    hpb = acc_ref.shape[0]
    n_plain = nt * (nt - 1) // 2
    dv = DIFF_V_DIM
    lp = lam_ref[...]
    lam = (jnp.exp(jnp.sum(lp[0:1] * lp[1:2], axis=-1, keepdims=True))
           - jnp.exp(jnp.sum(lp[2:3] * lp[3:4], axis=-1, keepdims=True)) + lam_init)
    zero = jnp.zeros((HEAD_DIM, t_q), BF16)

    def padded_q(iq, hh):
        q = q_ref[0, iq, hh * dv:(hh + 1) * dv, :]
        return jnp.concatenate([jnp.concatenate([q[:HEAD_DIM], zero], axis=0),
                                jnp.concatenate([zero, q[HEAD_DIM:]], axis=0)], axis=1)

    def logits(iq, j, hh, masked):
        rows = pl.ds(pl.multiple_of(j * t_q, t_q), t_q)
        lhs = jnp.concatenate([k_ref[0, rows, hh * dv:(hh + 1) * dv], kpos_ref[rows, :]], axis=1)
        rhs = jnp.concatenate([padded_q(iq, hh), qpos_ref[iq, hh * dv:(hh + 1) * dv, :]], axis=0)
        s = jnp.dot(lhs, rhs, preferred_element_type=F32)
        return s + mask_ref[...] if masked else s

    def finalize(iq, hh, acc, l):
        acc = acc * (1.0 / l)
        od = acc[:, :t_q] - lam * acc[:, t_q:]
        ms = jnp.mean(od * od, axis=0, keepdims=True)
        y = od * lax.rsqrt(ms + EPS) * gsub_ref[...]
        o_ref[0, pl.ds(pl.multiple_of(iq * t_q, t_q), t_q), hh * dv:(hh + 1) * dv] = y.T.astype(BF16)

    def probs(t, slot, *, base, masked):
        iq, j = iq_tbl_ref[base + t], j_tbl_ref[base + t]
        for hh in range(hpb):
            p = jnp.exp2(logits(iq, j, hh, masked))
            l_ref[hh, iq] += jnp.sum(p.reshape(t_q // 8, 8, 2 * t_q), axis=0)
            p_ref[slot, hh] = p.astype(BF16)

    def pv(t, slot, *, base):
        iq, j = iq_tbl_ref[base + t], j_tbl_ref[base + t]
        for hh in range(hpb):
            acc_ref[hh, iq] += jnp.dot(v_ref[0, j, hh * dv:(hh + 1) * dv, :], p_ref[slot, hh],
                                       preferred_element_type=F32)

    @pl.when(bounded_ref[0] != 0)
    def _():
        acc_ref[...] = jnp.zeros_like(acc_ref)
        l_ref[...] = jnp.zeros_like(l_ref)
        _staged_pipeline(n_plain, [functools.partial(probs, base=0, masked=False),
                                   functools.partial(pv, base=0)])
        _staged_pipeline(nt, [functools.partial(probs, base=n_plain, masked=True),
                              functools.partial(pv, base=n_plain)])

        def fin(iq, carry):
            for hh in range(hpb):
                finalize(iq, hh, acc_ref[hh, iq], jnp.sum(l_ref[hh, iq], axis=0, keepdims=True))
            return carry

        lax.fori_loop(0, nt, fin, 0)

    @pl.when(bounded_ref[0] == 0)
    def _():
        def q_block(iq, carry):
            acc_ref[:, 0] = jnp.zeros((hpb, dv, 2 * t_q), F32)

            def step(j, stats, masked):
                out = []
                for hh in range(hpb):
                    m, l = stats[2 * hh], stats[2 * hh + 1]
                    s = logits(iq, j, hh, masked)
                    mnew = jnp.maximum(m, jnp.max(s, axis=0, keepdims=True))
                    alpha = jnp.exp2(m - mnew)
                    p = jnp.exp2(s - mnew)
                    out += [mnew, alpha * l + jnp.sum(p, axis=0, keepdims=True)]
                    acc_ref[hh, 0] = alpha * acc_ref[hh, 0] + jnp.dot(
                        v_ref[0, j, hh * dv:(hh + 1) * dv, :], p.astype(BF16), preferred_element_type=F32)
                return tuple(out)

            init = (jnp.full((1, 2 * t_q), NEG_INF, F32), jnp.zeros((1, 2 * t_q), F32)) * hpb
            stats = lax.fori_loop(0, iq, lambda j, c: step(j, c, False), init)
            stats = step(iq, stats, True)
            for hh in range(hpb):
                finalize(iq, hh, acc_ref[hh, 0], stats[2 * hh + 1])
            return carry

        lax.fori_loop(0, nt, q_block, 0)


def _split_bf16(x, parts=3):
    out, rest = [], np.asarray(x, np.float64)
    for _ in range(parts):
        piece = rest.astype(BF16).astype(np.float64)
        out.append(piece)
        rest = rest - piece
    return out


def _alibi_features(slopes, s, t):
    ROW = DIFF_V_DIM
    kpos = np.arange(s)
    kfeat = np.zeros((s, ROW), np.float64)
    qfeat = np.zeros((s // t, len(slopes) * ROW, 2 * t), np.float64)
    qpos = (np.arange(s // t)[:, None] * t + np.arange(2 * t)[None, :] % t).astype(np.float64)
    for h, slope in enumerate(slopes):
        for i, piece in enumerate(_split_bf16(slope * LOG2E)):
            c = 5 * i
            kfeat[:, c], kfeat[:, c + 1], kfeat[:, c + 2:c + 5] = kpos // SWA_BLOCK, kpos % SWA_BLOCK, 1.0
            r = h * ROW + c
            qfeat[:, r], qfeat[:, r + 1] = float(piece) * SWA_BLOCK, float(piece)
            for n, part in enumerate(_split_bf16(float(piece) * qpos)):
                qfeat[:, r + 2 + n] = -part
    return jnp.asarray(kfeat, BF16), jnp.asarray(qfeat, BF16)


def _diff_attention(qd_t, kd, vd_t, lam_params, g_sub, slopes, lam_init, logit_bound, hpb=DIFF_HEADS_PER_STEP):
    b, nt, n_qd, t = qd_t.shape
    s = nt * t
    nh = n_qd // DIFF_V_DIM
    kfeat, qfeat = _alibi_features(slopes, s, t)
    causal = np.arange(t)[:, None] <= np.arange(2 * t)[None, :] % t
    mask = jnp.asarray(np.where(causal, 0.0, -np.inf), F32)
    gsub = jnp.broadcast_to((g_sub * (1.0 - lam_init))[:, None], (DIFF_V_DIM, t)).astype(F32)
    bounded = jnp.asarray(logit_bound <= MAX_UNSHIFTED_LOGIT, jnp.int32)
    pairs = [(iq, j) for iq in range(nt) for j in range(iq)] + [(iq, iq) for iq in range(nt)]
    iq_tbl = jnp.asarray([pq[0] for pq in pairs], jnp.int32)
    j_tbl = jnp.asarray([pq[1] for pq in pairs], jnp.int32)
    kern = functools.partial(_diff_kernel, lam_init=lam_init)
    return pl.pallas_call(
        kern,
        grid=(nh // hpb, b),
        in_specs=[
            pl.BlockSpec(memory_space=pltpu.SMEM),
            pl.BlockSpec(memory_space=pltpu.SMEM),
            pl.BlockSpec(memory_space=pltpu.SMEM),
            pl.BlockSpec((1, nt, hpb * DIFF_V_DIM, t), lambda hg, bi: (bi, 0, hg, 0)),
            pl.BlockSpec((nt, hpb * DIFF_V_DIM, 2 * t), lambda hg, bi: (0, hg, 0)),
            pl.BlockSpec((1, s, hpb * DIFF_V_DIM), lambda hg, bi: (bi, 0, hg)),
            pl.BlockSpec((s, DIFF_V_DIM), lambda hg, bi: (0, 0)),
            pl.BlockSpec((1, nt, hpb * DIFF_V_DIM, t), lambda hg, bi: (bi, 0, hg, 0)),
            pl.BlockSpec((t, 2 * t), lambda hg, bi: (0, 0)),
            pl.BlockSpec((4, HEAD_DIM), lambda hg, bi: (0, 0)),
            pl.BlockSpec((DIFF_V_DIM, t), lambda hg, bi: (0, 0)),
        ],
        out_specs=pl.BlockSpec((1, s, hpb * DIFF_V_DIM), lambda hg, bi: (bi, 0, hg)),
        out_shape=jax.ShapeDtypeStruct((b, s, n_qd), BF16),
        scratch_shapes=[pltpu.VMEM((hpb, nt, DIFF_V_DIM, 2 * t), F32),
                        pltpu.VMEM((hpb, nt, 8, 2 * t), F32),
                        pltpu.VMEM((2, hpb, t, 2 * t), BF16)],
        compiler_params=pltpu.CompilerParams(
            dimension_semantics=("parallel", "parallel"), vmem_limit_bytes=VMEM_LIMIT_BYTES),
        name="diff_attn",
    )(bounded.reshape(1), iq_tbl, j_tbl, qd_t, qfeat, kd, kfeat, vd_t, mask, lam_params, gsub)


def _outproj_kernel(x_ref, ya_ref, yd_ref, wa_ref, wd_ref, o_ref):
    o_ref[...] = (x_ref[...]
                  + jnp.dot(ya_ref[...], wa_ref[...], preferred_element_type=F32)
                  + jnp.dot(yd_ref[...], wd_ref[...], preferred_element_type=F32))


def _outproj(x2, ya2, yd2, wa, wd, tm=512):
    n, d = x2.shape
    tm = min(tm, n)
    assert n % tm == 0
    const = lambda shape: _resident(shape, lambda i: (0, 0))
    return pl.pallas_call(
        _outproj_kernel,
        grid=(n // tm,),
        in_specs=[
            pl.BlockSpec((tm, d), lambda i: (i, 0)),
            pl.BlockSpec((tm, ya2.shape[1]), lambda i: (i, 0)),
            pl.BlockSpec((tm, yd2.shape[1]), lambda i: (i, 0)),
            const(wa.shape), const(wd.shape),
        ],
        out_specs=pl.BlockSpec((tm, d), lambda i: (i, 0)),
        out_shape=jax.ShapeDtypeStruct((n, d), F32),
        compiler_params=pltpu.CompilerParams(
            dimension_semantics=("parallel",), vmem_limit_bytes=VMEM_LIMIT_BYTES),
        name="outproj",
    )(x2, ya2, yd2, wa, wd)


def _ffn_kernel(h_ref, g_ref, wg_ref, wu_ref, wd_ref, o_ref, u_ref):
    f = pl.program_id(1)

    @pl.when(f == 0)
    def _():
        h = h_ref[...]
        u_ref[...] = _rms_rows(h, g_ref[...]).astype(BF16)
        o_ref[...] = h

    u = u_ref[...]
    gate = jnp.dot(u, wg_ref[...], preferred_element_type=F32)
    up = jnp.dot(u, wu_ref[...], preferred_element_type=F32)
    act = (gate * (1.0 / (1.0 + jnp.exp(-gate))) * up).astype(BF16)
    o_ref[...] += jnp.dot(act, wd_ref[...], preferred_element_type=F32)


def _ffn(h2, g_ffn, wg, wu, wd, tm=1024, tf=512):
    n, d = h2.shape
    tm = min(tm, n)
    assert n % tm == 0
    dff = wg.shape[1]
    return pl.pallas_call(
        _ffn_kernel,
        grid=(n // tm, dff // tf),
        in_specs=[
            pl.BlockSpec((tm, d), lambda i, f: (i, 0)),
            _resident((1, d), lambda i, f: (0, 0)),
            pl.BlockSpec((d, tf), lambda i, f: (0, f)),
            pl.BlockSpec((d, tf), lambda i, f: (0, f)),
            pl.BlockSpec((tf, d), lambda i, f: (f, 0)),
        ],
        out_specs=pl.BlockSpec((tm, d), lambda i, f: (i, 0)),
        out_shape=jax.ShapeDtypeStruct((n, d), F32),
        scratch_shapes=[pltpu.VMEM((tm, d), BF16)],
        compiler_params=pltpu.CompilerParams(
            dimension_semantics=("parallel", "arbitrary"), vmem_limit_bytes=VMEM_LIMIT_BYTES),
        name="ffn",
    )(h2, g_ffn[None, :], wg, wu, wd)


def _ple_kernel(h_ref, p_ref, g_ref, wg_ref, wp_ref, go_ref, o_ref, *, sub, ncol):
    d = h_ref.shape[1]
    for st in range(h_ref.shape[0] // sub):
        rows = slice(st * sub, (st + 1) * sub)
        u = _rms_rows(h_ref[rows, :], g_ref[...]).astype(BF16)
        pp = jnp.dot(p_ref[rows, :].astype(BF16), wp_ref[...], preferred_element_type=F32)
        ppn = _rms_rows(pp, go_ref[...])
        for c in range(d // ncol):
            cols = slice(c * ncol, (c + 1) * ncol)
            z = jnp.dot(u, wg_ref[:, cols], preferred_element_type=F32)
            gate = 1.0 / (1.0 + jnp.exp(-z))
            o_ref[rows, cols] = h_ref[rows, cols] + gate * ppn[:, cols]


def _ple(h2, p2, g_ple, wg, wp, g_out, tm=1024, sub=512, ncol=512):
    n, d = h2.shape
    tm = min(tm, n)
    assert n % tm == 0
    sub = min(sub, tm)
    assert tm % sub == 0
    const = lambda shape: _resident(shape, lambda i: (0, 0))
    return pl.pallas_call(
        functools.partial(_ple_kernel, sub=sub, ncol=min(ncol, d)),
        grid=(n // tm,),
        in_specs=[
            pl.BlockSpec((tm, d), lambda i: (i, 0)),
            pl.BlockSpec((tm, p2.shape[1]), lambda i: (i, 0)),
            const((1, d)), const(wg.shape), const(wp.shape), const((1, d)),
        ],
        out_specs=pl.BlockSpec((tm, d), lambda i: (i, 0)),
        out_shape=jax.ShapeDtypeStruct((n, d), F32),
        compiler_params=pltpu.CompilerParams(
            dimension_semantics=("parallel",), vmem_limit_bytes=VMEM_LIMIT_BYTES),
        name="ple",
    )(h2, p2, g_ple[None, :], wg, wp, g_out[None, :])


def _alibi_slopes(n):
    return [2.0 ** (-8.0 * (h + 1) / n) for h in range(n)]


def _logit_bound(q_gain, k_gain):
    return 1.02 * math.sqrt(HEAD_DIM) * jnp.max(jnp.abs(q_gain)) * jnp.max(jnp.abs(k_gain))


def kernel(x, p, g_attn, w_in, qn_swa, kn_swa, sinks, qn_diff, kn_diff, lambda_q1, lambda_k1, lambda_q2,
           lambda_k2, g_sub, w_out, g_ffn, w_gate, w_up, w_down, g_ple, w_ple_gate, w_ple_proj, g_ple_out):
    b, s, d = x.shape
    depth = p.shape[0]
    n_qa = d // 2
    n_ka = n_va = SWA_KV_HEADS * HEAD_DIM
    n_qd = n_kd = n_vd = d // 2
    diff_heads = n_vd // DIFF_V_DIM
    swa_heads = n_qa // HEAD_DIM
    assert swa_heads == SWA_KV_HEADS * SWA_GROUP and s % TOKEN_TILE == 0
    c = np.cumsum([0, n_qa, n_ka, n_va, n_qd, n_kd, n_vd])
    h = x
    for i in range(depth):
        lam_init = 0.8 - 0.6 * math.exp(-0.3 * i)
        w = w_in[i]
        col = lambda k: w[:, c[k]:c[k + 1]]
        wfm = _transpose_cast_columns(w, [(c[k], c[k + 1]) for k in (0, 2, 3, 5)])
        wtm = jnp.concatenate([col(1), col(4)], axis=1).astype(BF16)
        qa_t, va_t, qd_t, vd_t, ka, kd = _inproj(
            h, g_attn[i], wfm, wtm, qn_swa[i], kn_swa[i], qn_diff[i], kn_diff[i],
            (n_qa, n_ka, n_va, n_qd, n_kd, n_vd))
        ya, (wg, wu, wd, wo, wpg) = _swa_attention(
            qa_t, ka, va_t, sinks[i], _alibi_slopes(swa_heads), _logit_bound(qn_swa[i], kn_swa[i]),
            f32_weights=(w_gate[i], w_up[i], w_down[i], w_out[i], w_ple_gate[i]))
        lam_params = jnp.stack([lambda_q1[i], lambda_k1[i], lambda_q2[i], lambda_k2[i]]).astype(F32)
        yd = _diff_attention(qd_t, kd, vd_t, lam_params, g_sub[i], _alibi_slopes(diff_heads), lam_init,
                             _logit_bound(qn_diff[i], kn_diff[i]))
        h2 = _outproj(h.reshape(b * s, d), ya.reshape(b * s, n_qa), yd.reshape(b * s, n_vd),
                      wo[:n_qa], wo[n_qa:])
        h2 = _ffn(h2, g_ffn[i], wg, wu, wd)
        h2 = _ple(h2, p[i].reshape(b * s, -1), g_ple[i], wpg, w_ple_proj[i].astype(BF16), g_ple_out[i])
        h = h2.reshape(b, s, d)
    return h
```

```python
import functools
import math

import jax
import jax.numpy as jnp
import numpy as np
from jax import lax
from jax.experimental import pallas as pl
from jax.experimental.pallas import tpu as pltpu

F32 = jnp.float32
BF16 = jnp.bfloat16

HEAD_DIM = 64
SWA_BLOCK = 128
SWA_KV_HEADS = 2
SWA_GROUP = 8
DIFF_V_DIM = 2 * HEAD_DIM
EPS = 1e-6
NEG_INF = float("-inf")
LOG2E = math.log2(math.e)
MAX_UNSHIFTED_LOGIT = 60.0

TOKEN_TILE = 256
INPROJ_TILES_PER_STEP = 4
DIFF_HEADS_PER_STEP = 4
VMEM_LIMIT_BYTES = 56 * 1024 * 1024

_NT = (((1,), (1,)), ((), ()))


def _resident(shape, index_map):
    return pl.BlockSpec(shape, index_map, pipeline_mode=pl.Buffered(1))


def _rms_rows(x, gain):
    ms = jnp.mean(x * x, axis=-1, keepdims=True)
    return x * lax.rsqrt(ms + EPS) * gain


def _cast_columns_kernel(blk_ref, w_ref, o_ref, *, transpose):
    del blk_ref
    w = w_ref[...]
    o_ref[...] = (w.T if transpose else w).astype(BF16)


def _cast_columns(w, col_ranges, transpose, cb=128):
    k, _ = w.shape
    blocks = [c0 // cb + i for c0, c1 in col_ranges for i in range((c1 - c0) // cb)]
    assert all(c0 % cb == 0 and c1 % cb == 0 for c0, c1 in col_ranges)
    n = len(blocks) * cb
    return pl.pallas_call(
        functools.partial(_cast_columns_kernel, transpose=transpose),
        grid_spec=pltpu.PrefetchScalarGridSpec(
            num_scalar_prefetch=1,
            grid=(len(blocks),),
            in_specs=[pl.BlockSpec((k, cb), lambda i, blk: (0, blk[i]))],
            out_specs=pl.BlockSpec((cb, k), lambda i, blk: (i, 0)) if transpose
            else pl.BlockSpec((k, cb), lambda i, blk: (0, i)),
        ),
        out_shape=jax.ShapeDtypeStruct((n, k) if transpose else (k, n), BF16),
        compiler_params=pltpu.CompilerParams(dimension_semantics=("parallel",)),
        name="weight_layout_t" if transpose else "weight_layout",
    )(jnp.asarray(blocks, jnp.int32), w)


def _inproj_kernel(x_ref, g_ref, wfm_ref, wtm_ref, gqa_ref, gqd_ref, gka_ref, gkd_ref, bd_ref,
                   qa_ref, va_ref, qd_ref, vd_ref, ka_ref, kd_ref, *, n_qa, n_va, n_qd, n_vd, n_ka, n_kd, fm_rows):
    tm = TOKEN_TILE
    per_tile = tm // SWA_BLOCK

    def headnorm_fm(z, gain, store):
        for h in range(z.shape[0] // HEAD_DIM):
            zh = z[h * HEAD_DIM:(h + 1) * HEAD_DIM, :]
            ms = jnp.mean(zh * zh, axis=0, keepdims=True)
            store(h, (zh * lax.rsqrt(ms + EPS) * gain).astype(BF16))

    def headnorm_tm(z, gain):
        ncols = z.shape[1]
        ssq = jnp.dot((z * z).astype(BF16), bd_ref[0:ncols, 0:ncols], preferred_element_type=F32)
        return (z * lax.rsqrt(ssq * (1.0 / HEAD_DIM) + EPS) * gain).astype(BF16)

    for st in range(x_ref.shape[0] // tm):
        rows = slice(st * tm, (st + 1) * tm)
        u = _rms_rows(x_ref[rows, :], g_ref[...]).astype(BF16)

        def fm(row0, nrows):
            return lax.dot_general(wfm_ref[row0:row0 + nrows, :], u, _NT, preferred_element_type=F32)

        def store_qa(h, val):
            for t in range(per_tile):
                qa_ref[0, st * per_tile + t, h * HEAD_DIM:(h + 1) * HEAD_DIM, :] = (
                    val[:, t * SWA_BLOCK:(t + 1) * SWA_BLOCK])

        def store_qd(h, val):
            qd_ref[0, st, h * HEAD_DIM:(h + 1) * HEAD_DIM, :] = val

        r_qa, r_va, r_qd, r_vd = fm_rows
        headnorm_fm(fm(r_qa, n_qa), gqa_ref[...], store_qa)
        zva = fm(r_va, n_va).astype(BF16)
        for t in range(per_tile):
            va_ref[0, st * per_tile + t] = zva[:, t * SWA_BLOCK:(t + 1) * SWA_BLOCK]
        headnorm_fm(fm(r_qd, n_qd), gqd_ref[...], store_qd)
        vd_ref[0, st] = fm(r_vd, n_vd).astype(BF16)

        zk = jnp.dot(u, wtm_ref[...], preferred_element_type=F32)
        ka_ref[0, rows, :] = headnorm_tm(zk[:, :n_ka], gka_ref[...])
        cw = bd_ref.shape[0]
        for c in range(n_kd // cw):
            kd_ref[0, rows, c * cw:(c + 1) * cw] = headnorm_tm(
                zk[:, n_ka + c * cw:n_ka + (c + 1) * cw], gkd_ref[...])


def _inproj(x, g_attn, wfm, wtm, qn_swa, kn_swa, qn_diff, kn_diff, dims, fm_rows):
    b, s, d = x.shape
    n_qa, n_ka, n_va, n_qd, n_kd, n_vd = dims
    tm = TOKEN_TILE
    nt = s // tm
    tps = min(INPROJ_TILES_PER_STEP, nt)
    assert nt % tps == 0
    blk = tps * tm
    scale = LOG2E / math.sqrt(HEAD_DIM)
    gqa = jnp.broadcast_to((qn_swa * scale)[:, None], (HEAD_DIM, tm)).astype(F32)
    gqd = jnp.broadcast_to((qn_diff * scale)[:, None], (HEAD_DIM, tm)).astype(F32)
    gka = jnp.tile(kn_swa, n_ka // HEAD_DIM)[None, :].astype(F32)
    gkd = jnp.tile(kn_diff, 256 // HEAD_DIM)[None, :].astype(F32)
    idx = np.arange(256) // HEAD_DIM
    bd = jnp.asarray(idx[:, None] == idx[None, :], dtype=BF16)
    const = lambda shape: _resident(shape, lambda bi, ti: (0,) * len(shape))
    kern = functools.partial(_inproj_kernel, n_qa=n_qa, n_va=n_va, n_qd=n_qd, n_vd=n_vd, n_ka=n_ka, n_kd=n_kd,
                             fm_rows=fm_rows)
    return pl.pallas_call(
        kern,
        grid=(b, nt // tps),
        in_specs=[
            pl.BlockSpec((None, blk, d), lambda bi, ti: (bi, ti, 0)),
            const((1, d)), const(wfm.shape), const(wtm.shape),
            const((HEAD_DIM, tm)), const((HEAD_DIM, tm)), const((1, n_ka)), const((1, 256)), const((256, 256)),
        ],
        out_specs=[
            pl.BlockSpec((1, blk // SWA_BLOCK, n_qa, SWA_BLOCK), lambda bi, ti: (bi, ti, 0, 0)),
            pl.BlockSpec((1, blk // SWA_BLOCK, n_va, SWA_BLOCK), lambda bi, ti: (bi, ti, 0, 0)),
            pl.BlockSpec((1, tps, n_qd, tm), lambda bi, ti: (bi, ti, 0, 0)),
            pl.BlockSpec((1, tps, n_vd, tm), lambda bi, ti: (bi, ti, 0, 0)),
            pl.BlockSpec((1, blk, n_ka), lambda bi, ti: (bi, ti, 0)),
            pl.BlockSpec((1, blk, n_kd), lambda bi, ti: (bi, ti, 0)),
        ],
        out_shape=[
            jax.ShapeDtypeStruct((b, s // SWA_BLOCK, n_qa, SWA_BLOCK), BF16),
            jax.ShapeDtypeStruct((b, s // SWA_BLOCK, n_va, SWA_BLOCK), BF16),
            jax.ShapeDtypeStruct((b, nt, n_qd, tm), BF16),
            jax.ShapeDtypeStruct((b, nt, n_vd, tm), BF16),
            jax.ShapeDtypeStruct((b, s, n_ka), BF16),
            jax.ShapeDtypeStruct((b, s, n_kd), BF16),
        ],
        compiler_params=pltpu.CompilerParams(
            dimension_semantics=("parallel", "parallel"), vmem_limit_bytes=VMEM_LIMIT_BYTES),
        name="inproj",
    )(x, g_attn[None, :], wfm, wtm, gqa, gqd, gka, gkd, bd)


def _staged_pipeline(n_steps, stages):
    depth = len(stages)

    def iteration(i, parity):
        for k in reversed(range(depth)):
            t = i - k
            if isinstance(i, int) and not 0 <= t < n_steps:
                continue
            stages[k](t, (parity - k) % 2)

    fill_end = min(depth - 1, n_steps)
    for i in range(fill_end):
        iteration(i, i % 2)
    n_pairs = (n_steps - fill_end) // 2

    def pair(m, carry):
        i = fill_end + 2 * m
        iteration(i, fill_end % 2)
        iteration(i + 1, (fill_end + 1) % 2)
        return carry

    lax.fori_loop(0, n_pairs, pair, 0)
    for i in range(fill_end + 2 * n_pairs, n_steps + depth - 1):
        iteration(i, i % 2)


def _swa_kernel(bounded_ref, q_ref, k_ref, v_ref, bias_ref, sink_ref, *refs, n_cast):
    cast_in, o_ref, cast_out = refs[:n_cast], refs[n_cast], refs[n_cast + 1:2 * n_cast + 1]
    s_ref, p_ref, d_ref = refs[2 * n_cast + 1:]
    for src, dst in zip(cast_in, cast_out):
        dst[...] = src[...].astype(BF16)

    hk = pl.program_id(0)
    nblk = q_ref.shape[1]
    w = SWA_BLOCK
    sink = sink_ref[0]

    def window_start(n):
        return jnp.maximum(n - 1, 0)

    def scores(n, slot):
        qblk = q_ref[0, n]
        qg = jnp.concatenate([qblk[g * HEAD_DIM:(g + 1) * HEAD_DIM, :] for g in range(SWA_GROUP)], axis=1)
        zero = jnp.zeros_like(qg)
        qpad = jnp.concatenate([jnp.where(hk == 0, qg, zero), jnp.where(hk == 1, qg, zero)], axis=0)
        kwin = k_ref[0, pl.ds(pl.multiple_of(window_start(n) * w, w), 2 * w), :]
        s_ref[slot] = jnp.dot(kwin, qpad, preferred_element_type=F32)

    def probs(n, slot, *, bounded):
        t = s_ref[slot] + bias_ref[jnp.minimum(n, 1)]
        if bounded:
            e = jnp.exp2(t)
            d_ref[slot] = jnp.sum(e, axis=0, keepdims=True) + jnp.exp2(sink)
        else:
            m = jnp.maximum(jnp.max(t, axis=0, keepdims=True), sink)
            e = jnp.exp2(t - m)
            d_ref[slot] = jnp.sum(e, axis=0, keepdims=True) + jnp.exp2(sink - m)
        p_ref[slot] = e.astype(BF16)

    def weighted_values(n, slot):
        first = window_start(n)
        vwin = jnp.concatenate([v_ref[0, first], v_ref[0, first + 1]], axis=1)
        o = jnp.dot(vwin, p_ref[slot], preferred_element_type=F32) * (1.0 / d_ref[slot])
        for gp in range(SWA_GROUP // 2):
            pair = jnp.concatenate([o[:, (2 * gp) * w:(2 * gp + 1) * w],
                                    o[:, (2 * gp + 1) * w:(2 * gp + 2) * w]], axis=0)
            o_ref[0, pl.ds(pl.multiple_of(n * w, w), w), gp * 2 * HEAD_DIM:(gp + 1) * 2 * HEAD_DIM] = (
                pair.T.astype(BF16))

    @pl.when(bounded_ref[0] != 0)
    def _():
        _staged_pipeline(nblk, [scores, functools.partial(probs, bounded=True), weighted_values])

    @pl.when(bounded_ref[0] == 0)
    def _():
        def block(n, carry):
            scores(n, 0)
            probs(n, 0, bounded=False)
            weighted_values(n, 0)
            return carry

        lax.fori_loop(0, nblk, block, 0)


def _swa_bias(slopes):
    w = SWA_BLOCK
    kj = np.arange(2 * w)[:, None]
    qi = np.arange(w)[None, :]
    out = np.empty((2, SWA_KV_HEADS, 2 * w, SWA_GROUP * w), np.float32)
    for first, dist in ((0, qi - kj), (1, qi - kj + w)):
        valid = (dist >= 0) & (dist < w)
        for hk in range(SWA_KV_HEADS):
            for g in range(SWA_GROUP):
                sl = slopes[hk * SWA_GROUP + g]
                out[first, hk, :, g * w:(g + 1) * w] = np.where(valid, -sl * dist, -np.inf)
    return out


def _swa_attention(qa_t, ka, va_t, sinks, slopes, logit_bound, f32_weights=()):
    b, nblk, n_qa, w = qa_t.shape
    s = nblk * w
    gw = SWA_GROUP * w
    bias = jnp.asarray(_swa_bias(slopes) * LOG2E)
    sinks = sinks.astype(F32)
    sink_cols = jnp.repeat((sinks * LOG2E).reshape(SWA_KV_HEADS, 1, SWA_GROUP), w, axis=-1)
    bounded = jnp.asarray(jnp.maximum(logit_bound, jnp.max(jnp.abs(sinks))) <= MAX_UNSHIFTED_LOGIT, jnp.int32)
    n_steps = SWA_KV_HEADS * b
    if any(wt.shape[0] % (16 * n_steps) for wt in f32_weights):
        ya, _ = _swa_attention(qa_t, ka, va_t, sinks, slopes, logit_bound)
        return ya, tuple(wt.astype(BF16) for wt in f32_weights)
    slab_specs = [pl.BlockSpec((wt.shape[0] // n_steps, wt.shape[1]), lambda hk, bi: (hk * b + bi, 0))
                  for wt in f32_weights]
    outs = pl.pallas_call(
        functools.partial(_swa_kernel, n_cast=len(f32_weights)),
        grid=(SWA_KV_HEADS, b),
        in_specs=[
            pl.BlockSpec(memory_space=pltpu.SMEM),
            pl.BlockSpec((1, nblk, SWA_GROUP * HEAD_DIM, w), lambda hk, bi: (bi, 0, hk, 0)),
            pl.BlockSpec((1, s, SWA_KV_HEADS * HEAD_DIM), lambda hk, bi: (bi, 0, 0)),
            pl.BlockSpec((1, nblk, HEAD_DIM, w), lambda hk, bi: (bi, 0, hk, 0)),
            pl.BlockSpec((2, None, 2 * w, gw), lambda hk, bi: (0, hk, 0, 0)),
            pl.BlockSpec((1, 1, gw), lambda hk, bi: (hk, 0, 0)),
        ] + slab_specs,
        out_specs=[pl.BlockSpec((1, s, SWA_GROUP * HEAD_DIM), lambda hk, bi: (bi, 0, hk))] + slab_specs,
        out_shape=[jax.ShapeDtypeStruct((b, s, n_qa), BF16)]
                  + [jax.ShapeDtypeStruct(wt.shape, BF16) for wt in f32_weights],
        scratch_shapes=[pltpu.VMEM((2, 2 * w, gw), F32), pltpu.VMEM((2, 2 * w, gw), BF16),
                        pltpu.VMEM((2, 1, gw), F32)],
        compiler_params=pltpu.CompilerParams(
            dimension_semantics=("parallel", "parallel"), vmem_limit_bytes=VMEM_LIMIT_BYTES),
        name="swa_attn",
    )(bounded.reshape(1), qa_t, ka, va_t, bias, sink_cols, *f32_weights)
    return outs[0], outs[1:]


def _diff_kernel(bounded_ref, iq_tbl_ref, j_tbl_ref, q_ref, qpos_ref, k_ref, kpos_ref, v_ref, mask_ref, lam_ref,
                 gsub_ref, o_ref, acc_ref, l_ref, p_ref, *, lam_init):
    nt, t_q = q_ref.shape[1], q_ref.shape[3]
    hpb = acc_ref.shape[0]
    n_plain = nt * (nt - 1) // 2
    dv = DIFF_V_DIM
    lp = lam_ref[...]
    lam = (jnp.exp(jnp.sum(lp[0:1] * lp[1:2], axis=-1, keepdims=True))
           - jnp.exp(jnp.sum(lp[2:3] * lp[3:4], axis=-1, keepdims=True)) + lam_init)
    zero = jnp.zeros((HEAD_DIM, t_q), BF16)

    def padded_q(iq, hh):
        q = q_ref[0, iq, hh * dv:(hh + 1) * dv, :]
        return jnp.concatenate([jnp.concatenate([q[:HEAD_DIM], zero], axis=0),
                                jnp.concatenate([zero, q[HEAD_DIM:]], axis=0)], axis=1)

    def logits(iq, j, hh, masked):
        rows = pl.ds(pl.multiple_of(j * t_q, t_q), t_q)
        lhs = jnp.concatenate([k_ref[0, rows, hh * dv:(hh + 1) * dv], kpos_ref[rows, :]], axis=1)
        rhs = jnp.concatenate([padded_q(iq, hh), qpos_ref[iq, hh * dv:(hh + 1) * dv, :]], axis=0)
        s = jnp.dot(lhs, rhs, preferred_element_type=F32)
        return s + mask_ref[...] if masked else s

    def finalize(iq, hh, acc, l):
        acc = acc * (1.0 / l)
        od = acc[:, :t_q] - lam * acc[:, t_q:]
        ms = jnp.mean(od * od, axis=0, keepdims=True)
        y = od * lax.rsqrt(ms + EPS) * gsub_ref[...]
        o_ref[0, pl.ds(pl.multiple_of(iq * t_q, t_q), t_q), hh * dv:(hh + 1) * dv] = y.T.astype(BF16)

    def probs(t, slot, *, base, masked):
        iq, j = iq_tbl_ref[base + t], j_tbl_ref[base + t]
        for hh in range(hpb):
            p = jnp.exp2(logits(iq, j, hh, masked))
            l_ref[hh, iq] += jnp.sum(p.reshape(t_q // 8, 8, 2 * t_q), axis=0)
            p_ref[slot, hh] = p.astype(BF16)

    def pv(t, slot, *, base):
        iq, j = iq_tbl_ref[base + t], j_tbl_ref[base + t]
        for hh in range(hpb):
            acc_ref[hh, iq] += jnp.dot(v_ref[0, j, hh * dv:(hh + 1) * dv, :], p_ref[slot, hh],
                                       preferred_element_type=F32)

    @pl.when(bounded_ref[0] != 0)
    def _():
        acc_ref[...] = jnp.zeros_like(acc_ref)
        l_ref[...] = jnp.zeros_like(l_ref)
        _staged_pipeline(n_plain, [functools.partial(probs, base=0, masked=False),
                                   functools.partial(pv, base=0)])
        _staged_pipeline(nt, [functools.partial(probs, base=n_plain, masked=True),
                              functools.partial(pv, base=n_plain)])

        def fin(iq, carry):
            for hh in range(hpb):
                finalize(iq, hh, acc_ref[hh, iq], jnp.sum(l_ref[hh, iq], axis=0, keepdims=True))
            return carry

        lax.fori_loop(0, nt, fin, 0)

    @pl.when(bounded_ref[0] == 0)
    def _():
        def q_block(iq, carry):
            acc_ref[:, 0] = jnp.zeros((hpb, dv, 2 * t_q), F32)

            def step(j, stats, masked):
                out = []
                for hh in range(hpb):
                    m, l = stats[2 * hh], stats[2 * hh + 1]
                    s = logits(iq, j, hh, masked)
                    mnew = jnp.maximum(m, jnp.max(s, axis=0, keepdims=True))
                    alpha = jnp.exp2(m - mnew)
                    p = jnp.exp2(s - mnew)
                    out += [mnew, alpha * l + jnp.sum(p, axis=0, keepdims=True)]
                    acc_ref[hh, 0] = alpha * acc_ref[hh, 0] + jnp.dot(
                        v_ref[0, j, hh * dv:(hh + 1) * dv, :], p.astype(BF16), preferred_element_type=F32)
                return tuple(out)

            init = (jnp.full((1, 2 * t_q), NEG_INF, F32), jnp.zeros((1, 2 * t_q), F32)) * hpb
            stats = lax.fori_loop(0, iq, lambda j, c: step(j, c, False), init)
            stats = step(iq, stats, True)
            for hh in range(hpb):
                finalize(iq, hh, acc_ref[hh, 0], stats[2 * hh + 1])
            return carry

        lax.fori_loop(0, nt, q_block, 0)


def _split_bf16(x, parts=3):
    out, rest = [], np.asarray(x, np.float64)
    for _ in range(parts):
        piece = rest.astype(BF16).astype(np.float64)
        out.append(piece)
        rest = rest - piece
    return out


def _alibi_features(slopes, s, t):
    ROW = DIFF_V_DIM
    RADIX = 128
    assert s <= RADIX * 256
    kpos = np.arange(s)
    kfeat = np.zeros((s, ROW), np.float64)
    qfeat = np.zeros((s // t, len(slopes) * ROW, 2 * t), np.float64)
    qpos = (np.arange(s // t)[:, None] * t + np.arange(2 * t)[None, :] % t).astype(np.float64)
    for h, slope in enumerate(slopes):
        for i, piece in enumerate(_split_bf16(slope * LOG2E)):
            c = 5 * i
            kfeat[:, c], kfeat[:, c + 1], kfeat[:, c + 2:c + 5] = kpos // RADIX, kpos % RADIX, 1.0
            r = h * ROW + c
            qfeat[:, r], qfeat[:, r + 1] = float(piece) * RADIX, float(piece)
            for n, part in enumerate(_split_bf16(float(piece) * qpos)):
                qfeat[:, r + 2 + n] = -part
    return jnp.asarray(kfeat, BF16), jnp.asarray(qfeat, BF16)


def _diff_attention(qd_t, kd, vd_t, lam_params, g_sub, slopes, lam_init, logit_bound, hpb=DIFF_HEADS_PER_STEP):
    b, nt, n_qd, t = qd_t.shape
    s = nt * t
    nh = n_qd // DIFF_V_DIM
    kfeat, qfeat = _alibi_features(slopes, s, t)
    causal = np.arange(t)[:, None] <= np.arange(2 * t)[None, :] % t
    mask = jnp.asarray(np.where(causal, 0.0, -np.inf), F32)
    gsub = jnp.broadcast_to((g_sub * (1.0 - lam_init))[:, None], (DIFF_V_DIM, t)).astype(F32)
    bounded = jnp.asarray(logit_bound <= MAX_UNSHIFTED_LOGIT, jnp.int32)
    pairs = [(iq, j) for iq in range(nt) for j in range(iq)] + [(iq, iq) for iq in range(nt)]
    iq_tbl = jnp.asarray([pq[0] for pq in pairs], jnp.int32)
    j_tbl = jnp.asarray([pq[1] for pq in pairs], jnp.int32)
    kern = functools.partial(_diff_kernel, lam_init=lam_init)
    return pl.pallas_call(
        kern,
        grid=(nh // hpb, b),
        in_specs=[
            pl.BlockSpec(memory_space=pltpu.SMEM),
            pl.BlockSpec(memory_space=pltpu.SMEM),
            pl.BlockSpec(memory_space=pltpu.SMEM),
            pl.BlockSpec((1, nt, hpb * DIFF_V_DIM, t), lambda hg, bi: (bi, 0, hg, 0)),
            pl.BlockSpec((nt, hpb * DIFF_V_DIM, 2 * t), lambda hg, bi: (0, hg, 0)),
            pl.BlockSpec((1, s, hpb * DIFF_V_DIM), lambda hg, bi: (bi, 0, hg)),
            pl.BlockSpec((s, DIFF_V_DIM), lambda hg, bi: (0, 0)),
            pl.BlockSpec((1, nt, hpb * DIFF_V_DIM, t), lambda hg, bi: (bi, 0, hg, 0)),
            pl.BlockSpec((t, 2 * t), lambda hg, bi: (0, 0)),
            pl.BlockSpec((4, HEAD_DIM), lambda hg, bi: (0, 0)),
            pl.BlockSpec((DIFF_V_DIM, t), lambda hg, bi: (0, 0)),
        ],
        out_specs=pl.BlockSpec((1, s, hpb * DIFF_V_DIM), lambda hg, bi: (bi, 0, hg)),
        out_shape=jax.ShapeDtypeStruct((b, s, n_qd), BF16),
        scratch_shapes=[pltpu.VMEM((hpb, nt, DIFF_V_DIM, 2 * t), F32),
                        pltpu.VMEM((hpb, nt, 8, 2 * t), F32),
                        pltpu.VMEM((2, hpb, t, 2 * t), BF16)],
        compiler_params=pltpu.CompilerParams(
            dimension_semantics=("parallel", "parallel"), vmem_limit_bytes=VMEM_LIMIT_BYTES),
        name="diff_attn",
    )(bounded.reshape(1), iq_tbl, j_tbl, qd_t, qfeat, kd, kfeat, vd_t, mask, lam_params, gsub)


def _outproj_kernel(x_ref, ya_ref, yd_ref, wa_ref, wd_ref, o_ref):
    o_ref[...] = (x_ref[...]
                  + jnp.dot(ya_ref[...], wa_ref[...], preferred_element_type=F32)
                  + jnp.dot(yd_ref[...], wd_ref[...], preferred_element_type=F32))


def _outproj(x2, ya2, yd2, wo, tm=512):
    n, d = x2.shape
    tm = min(tm, n)
    n_a, n_d = ya2.shape[1], yd2.shape[1]
    assert n % tm == 0 and n_a == n_d and wo.shape == (n_a + n_d, d)
    return pl.pallas_call(
        _outproj_kernel,
        grid=(n // tm,),
        in_specs=[
            pl.BlockSpec((tm, d), lambda i: (i, 0)),
            pl.BlockSpec((tm, n_a), lambda i: (i, 0)),
            pl.BlockSpec((tm, n_d), lambda i: (i, 0)),
            _resident((n_a, d), lambda i: (0, 0)),
            _resident((n_d, d), lambda i: (1, 0)),
        ],
        out_specs=pl.BlockSpec((tm, d), lambda i: (i, 0)),
        out_shape=jax.ShapeDtypeStruct((n, d), F32),
        compiler_params=pltpu.CompilerParams(
            dimension_semantics=("parallel",), vmem_limit_bytes=VMEM_LIMIT_BYTES),
        name="outproj",
    )(x2, ya2, yd2, wo, wo)


def _ffn_kernel(h_ref, g_ref, wg_ref, wu_ref, wd_ref, o_ref, u_ref):
    f = pl.program_id(1)

    @pl.when(f == 0)
    def _():
        h = h_ref[...]
        u_ref[...] = _rms_rows(h, g_ref[...]).astype(BF16)
        o_ref[...] = h

    u = u_ref[...]
    gate = jnp.dot(u, wg_ref[...], preferred_element_type=F32)
    up = jnp.dot(u, wu_ref[...], preferred_element_type=F32)
    act = (gate * (1.0 / (1.0 + jnp.exp(-gate))) * up).astype(BF16)
    o_ref[...] += jnp.dot(act, wd_ref[...], preferred_element_type=F32)


def _ffn(h2, g_ffn, wg, wu, wd, tm=1024, tf=512):
    n, d = h2.shape
    tm = min(tm, n)
    assert n % tm == 0
    dff = wg.shape[1]
    return pl.pallas_call(
        _ffn_kernel,
        grid=(n // tm, dff // tf),
        in_specs=[
            pl.BlockSpec((tm, d), lambda i, f: (i, 0)),
            _resident((1, d), lambda i, f: (0, 0)),
            pl.BlockSpec((d, tf), lambda i, f: (0, f)),
            pl.BlockSpec((d, tf), lambda i, f: (0, f)),
            pl.BlockSpec((tf, d), lambda i, f: (f, 0)),
        ],
        out_specs=pl.BlockSpec((tm, d), lambda i, f: (i, 0)),
        out_shape=jax.ShapeDtypeStruct((n, d), F32),
        scratch_shapes=[pltpu.VMEM((tm, d), BF16)],
        compiler_params=pltpu.CompilerParams(
            dimension_semantics=("parallel", "arbitrary"), vmem_limit_bytes=VMEM_LIMIT_BYTES),
        name="ffn",
    )(h2, g_ffn[None, :], wg, wu, wd)


def _ple_kernel(h_ref, p_ref, g_ref, wg_ref, wp_ref, go_ref, o_ref, *, sub, ncol):
    d = h_ref.shape[1]
    for st in range(h_ref.shape[0] // sub):
        rows = slice(st * sub, (st + 1) * sub)
        u = _rms_rows(h_ref[rows, :], g_ref[...]).astype(BF16)
        pp = jnp.dot(p_ref[rows, :].astype(BF16), wp_ref[...], preferred_element_type=F32)
        ppn = _rms_rows(pp, go_ref[...])
        for c in range(d // ncol):
            cols = slice(c * ncol, (c + 1) * ncol)
            z = jnp.dot(u, wg_ref[:, cols], preferred_element_type=F32)
            gate = 1.0 / (1.0 + jnp.exp(-z))
            o_ref[rows, cols] = h_ref[rows, cols] + gate * ppn[:, cols]


def _ple(h2, p2, g_ple, wg, wp, g_out, tm=1024, sub=512, ncol=512):
    n, d = h2.shape
    tm = min(tm, n)
    assert n % tm == 0
    sub = min(sub, tm)
    assert tm % sub == 0
    const = lambda shape: _resident(shape, lambda i: (0, 0))
    return pl.pallas_call(
        functools.partial(_ple_kernel, sub=sub, ncol=min(ncol, d)),
        grid=(n // tm,),
        in_specs=[
            pl.BlockSpec((tm, d), lambda i: (i, 0)),
            pl.BlockSpec((tm, p2.shape[1]), lambda i: (i, 0)),
            const((1, d)), const(wg.shape), const(wp.shape), const((1, d)),
        ],
        out_specs=pl.BlockSpec((tm, d), lambda i: (i, 0)),
        out_shape=jax.ShapeDtypeStruct((n, d), F32),
        compiler_params=pltpu.CompilerParams(
            dimension_semantics=("parallel",), vmem_limit_bytes=VMEM_LIMIT_BYTES),
        name="ple",
    )(h2, p2, g_ple[None, :], wg, wp, g_out[None, :])


def _alibi_slopes(n):
    return [2.0 ** (-8.0 * (h + 1) / n) for h in range(n)]


def _logit_bound(q_gain, k_gain):
    return 1.02 * math.sqrt(HEAD_DIM) * jnp.max(jnp.abs(q_gain)) * jnp.max(jnp.abs(k_gain))


def kernel(x, p, g_attn, w_in, qn_swa, kn_swa, sinks, qn_diff, kn_diff, lambda_q1, lambda_k1, lambda_q2,
           lambda_k2, g_sub, w_out, g_ffn, w_gate, w_up, w_down, g_ple, w_ple_gate, w_ple_proj, g_ple_out):
    b, s, d = x.shape
    depth = p.shape[0]
    n_qa = d // 2
    n_ka = n_va = SWA_KV_HEADS * HEAD_DIM
    n_qd = n_kd = n_vd = d // 2
    diff_heads = n_vd // DIFF_V_DIM
    swa_heads = n_qa // HEAD_DIM
    assert swa_heads == SWA_KV_HEADS * SWA_GROUP and s % TOKEN_TILE == 0
    c = np.cumsum([0, n_qa, n_ka, n_va, n_qd, n_kd, n_vd])
    h = x
    for i in range(depth):
        lam_init = 0.8 - 0.6 * math.exp(-0.3 * i)
        wfm = _cast_columns(w_in[i], [(c[0], c[1]), (c[1], c[3]), (c[3], c[4]), (c[5], c[6])], transpose=True,
                            cb=2 * SWA_BLOCK)
        fm_rows = (0, n_qa + n_ka, n_qa + n_ka + n_va, n_qa + n_ka + n_va + n_qd)
        wtm = _cast_columns(w_in[i], [(c[1], c[2]), (c[4], c[5])], transpose=False)
        qa_t, va_t, qd_t, vd_t, ka, kd = _inproj(
            h, g_attn[i], wfm, wtm, qn_swa[i], kn_swa[i], qn_diff[i], kn_diff[i],
            (n_qa, n_ka, n_va, n_qd, n_kd, n_vd), fm_rows)
        ya, (wg, wu, wd, wo, wpg) = _swa_attention(
            qa_t, ka, va_t, sinks[i], _alibi_slopes(swa_heads), _logit_bound(qn_swa[i], kn_swa[i]),
            f32_weights=(w_gate[i], w_up[i], w_down[i], w_out[i], w_ple_gate[i]))
        lam_params = jnp.stack([lambda_q1[i], lambda_k1[i], lambda_q2[i], lambda_k2[i]]).astype(F32)
        yd = _diff_attention(qd_t, kd, vd_t, lam_params, g_sub[i], _alibi_slopes(diff_heads), lam_init,
                             _logit_bound(qn_diff[i], kn_diff[i]))
        h2 = _outproj(h.reshape(b * s, d), ya.reshape(b * s, n_qa), yd.reshape(b * s, n_vd), wo)
        h2 = _ffn(h2, g_ffn[i], wg, wu, wd)
        h2 = _ple(h2, p[i].reshape(b * s, -1), g_ple[i], wpg, w_ple_proj[i].astype(BF16), g_ple_out[i])
        h = h2.reshape(b, s, d)
    return h
```

```python
import functools
import math

import jax
import jax.numpy as jnp
import numpy as np
from jax import lax
from jax.experimental import pallas as pl
from jax.experimental.pallas import tpu as pltpu

F32 = jnp.float32
BF16 = jnp.bfloat16

HEAD_DIM = 64
SWA_BLOCK = 128
SWA_KV_HEADS = 2
SWA_GROUP = 8
DIFF_V_DIM = 2 * HEAD_DIM
EPS = 1e-6
NEG_INF = float("-inf")
LOG2E = math.log2(math.e)
MAX_UNSHIFTED_LOGIT = 60.0

TOKEN_TILE = 256
INPROJ_TILES_PER_STEP = 4
DIFF_HEADS_PER_STEP = 4
SWA_CAST_SLAB_BYTES = 12 * 1024 * 1024
VMEM_LIMIT_BYTES = 56 * 1024 * 1024

_NT = (((1,), (1,)), ((), ()))


def _resident(shape, index_map):
    return pl.BlockSpec(shape, index_map, pipeline_mode=pl.Buffered(1))


def _rms_rows(x, gain):
    ms = jnp.mean(x * x, axis=-1, keepdims=True)
    return x * lax.rsqrt(ms + EPS) * gain


def _cast_columns_kernel(blk_ref, w_ref, o_ref, *, transpose):
    del blk_ref
    w = w_ref[...]
    o_ref[...] = (w.T if transpose else w).astype(BF16)


def _cast_columns(w, col_ranges, transpose, cb=128):
    k, _ = w.shape
    blocks = [c0 // cb + i for c0, c1 in col_ranges for i in range((c1 - c0) // cb)]
    assert all(c0 % cb == 0 and c1 % cb == 0 for c0, c1 in col_ranges)
    n = len(blocks) * cb
    return pl.pallas_call(
        functools.partial(_cast_columns_kernel, transpose=transpose),
        grid_spec=pltpu.PrefetchScalarGridSpec(
            num_scalar_prefetch=1,
            grid=(len(blocks),),
            in_specs=[pl.BlockSpec((k, cb), lambda i, blk: (0, blk[i]))],
            out_specs=pl.BlockSpec((cb, k), lambda i, blk: (i, 0)) if transpose
            else pl.BlockSpec((k, cb), lambda i, blk: (0, i)),
        ),
        out_shape=jax.ShapeDtypeStruct((n, k) if transpose else (k, n), BF16),
        compiler_params=pltpu.CompilerParams(dimension_semantics=("parallel",)),
        name="weight_layout_t" if transpose else "weight_layout",
    )(jnp.asarray(blocks, jnp.int32), w)


def _inproj_kernel(x_ref, g_ref, wfm_ref, wtm_ref, gqa_ref, gqd_ref, gka_ref, gkd_ref, bd_ref,
                   qa_ref, va_ref, qd_ref, vd_ref, ka_ref, kd_ref, *, n_qa, n_va, n_qd, n_vd, n_ka, n_kd, fm_rows):
    tm = TOKEN_TILE
    per_tile = tm // SWA_BLOCK

    def headnorm_fm(z, gain, store):
        for h in range(z.shape[0] // HEAD_DIM):
            zh = z[h * HEAD_DIM:(h + 1) * HEAD_DIM, :]
            ms = jnp.mean(zh * zh, axis=0, keepdims=True)
            store(h, (zh * lax.rsqrt(ms + EPS) * gain).astype(BF16))

    def headnorm_tm(z, gain):
        ncols = z.shape[1]
        ssq = jnp.dot((z * z).astype(BF16), bd_ref[0:ncols, 0:ncols], preferred_element_type=F32)
        return (z * lax.rsqrt(ssq * (1.0 / HEAD_DIM) + EPS) * gain).astype(BF16)

    for st in range(x_ref.shape[0] // tm):
        rows = slice(st * tm, (st + 1) * tm)
        u = _rms_rows(x_ref[rows, :], g_ref[...]).astype(BF16)

        def fm(row0, nrows):
            return lax.dot_general(wfm_ref[row0:row0 + nrows, :], u, _NT, preferred_element_type=F32)

        def store_qa(h, val):
            for t in range(per_tile):
                qa_ref[0, st * per_tile + t, h * HEAD_DIM:(h + 1) * HEAD_DIM, :] = (
                    val[:, t * SWA_BLOCK:(t + 1) * SWA_BLOCK])

        def store_qd(h, val):
            qd_ref[0, st, h * HEAD_DIM:(h + 1) * HEAD_DIM, :] = val

        r_qa, r_va, r_qd, r_vd = fm_rows
        headnorm_fm(fm(r_qa, n_qa), gqa_ref[...], store_qa)
        zva = fm(r_va, n_va).astype(BF16)
        for t in range(per_tile):
            va_ref[0, st * per_tile + t] = zva[:, t * SWA_BLOCK:(t + 1) * SWA_BLOCK]
        headnorm_fm(fm(r_qd, n_qd), gqd_ref[...], store_qd)
        vd_ref[0, st] = fm(r_vd, n_vd).astype(BF16)

        zk = jnp.dot(u, wtm_ref[...], preferred_element_type=F32)
        ka_ref[0, rows, :] = headnorm_tm(zk[:, :n_ka], gka_ref[...])
        cw = bd_ref.shape[0]
        for c in range(n_kd // cw):
            kd_ref[0, rows, c * cw:(c + 1) * cw] = headnorm_tm(
                zk[:, n_ka + c * cw:n_ka + (c + 1) * cw], gkd_ref[...])


def _inproj(x, g_attn, wfm, wtm, qn_swa, kn_swa, qn_diff, kn_diff, dims, fm_rows):
    b, s, d = x.shape
    n_qa, n_ka, n_va, n_qd, n_kd, n_vd = dims
    tm = TOKEN_TILE
    nt = s // tm
    tps = min(INPROJ_TILES_PER_STEP, nt)
    assert nt % tps == 0
    blk = tps * tm
    scale = LOG2E / math.sqrt(HEAD_DIM)
    gqa = jnp.broadcast_to((qn_swa * scale)[:, None], (HEAD_DIM, tm)).astype(F32)
    gqd = jnp.broadcast_to((qn_diff * scale)[:, None], (HEAD_DIM, tm)).astype(F32)
    gka = jnp.tile(kn_swa, n_ka // HEAD_DIM)[None, :].astype(F32)
    gkd = jnp.tile(kn_diff, 256 // HEAD_DIM)[None, :].astype(F32)
    idx = np.arange(256) // HEAD_DIM
    bd = jnp.asarray(idx[:, None] == idx[None, :], dtype=BF16)
    const = lambda shape: _resident(shape, lambda bi, ti: (0,) * len(shape))
    kern = functools.partial(_inproj_kernel, n_qa=n_qa, n_va=n_va, n_qd=n_qd, n_vd=n_vd, n_ka=n_ka, n_kd=n_kd,
                             fm_rows=fm_rows)
    return pl.pallas_call(
        kern,
        grid=(b, nt // tps),
        in_specs=[
            pl.BlockSpec((None, blk, d), lambda bi, ti: (bi, ti, 0)),
            const((1, d)), const(wfm.shape), const(wtm.shape),
            const((HEAD_DIM, tm)), const((HEAD_DIM, tm)), const((1, n_ka)), const((1, 256)), const((256, 256)),
        ],
        out_specs=[
            pl.BlockSpec((1, blk // SWA_BLOCK, n_qa, SWA_BLOCK), lambda bi, ti: (bi, ti, 0, 0)),
            pl.BlockSpec((1, blk // SWA_BLOCK, n_va, SWA_BLOCK), lambda bi, ti: (bi, ti, 0, 0)),
            pl.BlockSpec((1, tps, n_qd, tm), lambda bi, ti: (bi, ti, 0, 0)),
            pl.BlockSpec((1, tps, n_vd, tm), lambda bi, ti: (bi, ti, 0, 0)),
            pl.BlockSpec((1, blk, n_ka), lambda bi, ti: (bi, ti, 0)),
            pl.BlockSpec((1, blk, n_kd), lambda bi, ti: (bi, ti, 0)),
        ],
        out_shape=[
            jax.ShapeDtypeStruct((b, s // SWA_BLOCK, n_qa, SWA_BLOCK), BF16),
            jax.ShapeDtypeStruct((b, s // SWA_BLOCK, n_va, SWA_BLOCK), BF16),
            jax.ShapeDtypeStruct((b, nt, n_qd, tm), BF16),
            jax.ShapeDtypeStruct((b, nt, n_vd, tm), BF16),
            jax.ShapeDtypeStruct((b, s, n_ka), BF16),
            jax.ShapeDtypeStruct((b, s, n_kd), BF16),
        ],
        compiler_params=pltpu.CompilerParams(
            dimension_semantics=("parallel", "parallel"), vmem_limit_bytes=VMEM_LIMIT_BYTES),
        name="inproj",
    )(x, g_attn[None, :], wfm, wtm, gqa, gqd, gka, gkd, bd)


def _staged_pipeline(n_steps, stages):
    depth = len(stages)

    def iteration(i, parity):
        for k in reversed(range(depth)):
            t = i - k
            if isinstance(i, int) and not 0 <= t < n_steps:
                continue
            stages[k](t, (parity - k) % 2)

    fill_end = min(depth - 1, n_steps)
    for i in range(fill_end):
        iteration(i, i % 2)
    n_pairs = (n_steps - fill_end) // 2

    def pair(m, carry):
        i = fill_end + 2 * m
        iteration(i, fill_end % 2)
        iteration(i + 1, (fill_end + 1) % 2)
        return carry

    lax.fori_loop(0, n_pairs, pair, 0)
    for i in range(fill_end + 2 * n_pairs, n_steps + depth - 1):
        iteration(i, i % 2)


def _swa_kernel(bounded_ref, q_ref, k_ref, v_ref, bias_ref, sink_ref, *refs, n_cast):
    cast_in, o_ref, cast_out = refs[:n_cast], refs[n_cast], refs[n_cast + 1:2 * n_cast + 1]
    s_ref, p_ref, d_ref = refs[2 * n_cast + 1:]
    for src, dst in zip(cast_in, cast_out):
        dst[...] = src[...].astype(BF16)

    hk = pl.program_id(0)
    nblk = q_ref.shape[1]
    w = SWA_BLOCK
    sink = sink_ref[0]

    def window_start(n):
        return jnp.maximum(n - 1, 0)

    def scores(n, slot):
        qblk = q_ref[0, n]
        qg = jnp.concatenate([qblk[g * HEAD_DIM:(g + 1) * HEAD_DIM, :] for g in range(SWA_GROUP)], axis=1)
        zero = jnp.zeros_like(qg)
        qpad = jnp.concatenate([jnp.where(hk == 0, qg, zero), jnp.where(hk == 1, qg, zero)], axis=0)
        kwin = k_ref[0, pl.ds(pl.multiple_of(window_start(n) * w, w), 2 * w), :]
        s_ref[slot] = jnp.dot(kwin, qpad, preferred_element_type=F32)

    def probs(n, slot, *, bounded):
        t = s_ref[slot] + bias_ref[jnp.minimum(n, 1)]
        if bounded:
            e = jnp.exp2(t)
            d_ref[slot] = jnp.sum(e, axis=0, keepdims=True) + jnp.exp2(sink)
        else:
            m = jnp.maximum(jnp.max(t, axis=0, keepdims=True), sink)
            e = jnp.exp2(t - m)
            d_ref[slot] = jnp.sum(e, axis=0, keepdims=True) + jnp.exp2(sink - m)
        p_ref[slot] = e.astype(BF16)

    def weighted_values(n, slot):
        first = window_start(n)
        vwin = jnp.concatenate([v_ref[0, first], v_ref[0, first + 1]], axis=1)
        o = jnp.dot(vwin, p_ref[slot], preferred_element_type=F32) * (1.0 / d_ref[slot])
        for gp in range(SWA_GROUP // 2):
            pair = jnp.concatenate([o[:, (2 * gp) * w:(2 * gp + 1) * w],
                                    o[:, (2 * gp + 1) * w:(2 * gp + 2) * w]], axis=0)
            o_ref[0, pl.ds(pl.multiple_of(n * w, w), w), gp * 2 * HEAD_DIM:(gp + 1) * 2 * HEAD_DIM] = (
                pair.T.astype(BF16))

    @pl.when(bounded_ref[0] != 0)
    def _():
        _staged_pipeline(nblk, [scores, functools.partial(probs, bounded=True), weighted_values])

    @pl.when(bounded_ref[0] == 0)
    def _():
        def block(n, carry):
            scores(n, 0)
            probs(n, 0, bounded=False)
            weighted_values(n, 0)
            return carry

        lax.fori_loop(0, nblk, block, 0)


def _swa_bias(slopes):
    w = SWA_BLOCK
    kj = np.arange(2 * w)[:, None]
    qi = np.arange(w)[None, :]
    out = np.empty((2, SWA_KV_HEADS, 2 * w, SWA_GROUP * w), np.float32)
    for first, dist in ((0, qi - kj), (1, qi - kj + w)):
        valid = (dist >= 0) & (dist < w)
        for hk in range(SWA_KV_HEADS):
            for g in range(SWA_GROUP):
                sl = slopes[hk * SWA_GROUP + g]
                out[first, hk, :, g * w:(g + 1) * w] = np.where(valid, -sl * dist, -np.inf)
    return out


def _swa_attention(qa_t, ka, va_t, sinks, slopes, logit_bound, f32_weights=()):
    b, nblk, n_qa, w = qa_t.shape
    s = nblk * w
    gw = SWA_GROUP * w
    bias = jnp.asarray(_swa_bias(slopes) * LOG2E)
    sinks = sinks.astype(F32)
    sink_cols = jnp.repeat((sinks * LOG2E).reshape(SWA_KV_HEADS, 1, SWA_GROUP), w, axis=-1)
    bounded = jnp.asarray(jnp.maximum(logit_bound, jnp.max(jnp.abs(sinks))) <= MAX_UNSHIFTED_LOGIT, jnp.int32)
    n_steps = SWA_KV_HEADS * b
    slab_bytes = sum(wt.size * 4 // n_steps for wt in f32_weights)
    if (any(wt.shape[0] % (16 * n_steps) for wt in f32_weights)
            or slab_bytes > SWA_CAST_SLAB_BYTES):
        ya, _ = _swa_attention(qa_t, ka, va_t, sinks, slopes, logit_bound)
        return ya, tuple(wt.astype(BF16) for wt in f32_weights)
    slab_specs = [pl.BlockSpec((wt.shape[0] // n_steps, wt.shape[1]), lambda hk, bi: (hk * b + bi, 0))
                  for wt in f32_weights]
    outs = pl.pallas_call(
        functools.partial(_swa_kernel, n_cast=len(f32_weights)),
        grid=(SWA_KV_HEADS, b),
        in_specs=[
            pl.BlockSpec(memory_space=pltpu.SMEM),
            pl.BlockSpec((1, nblk, SWA_GROUP * HEAD_DIM, w), lambda hk, bi: (bi, 0, hk, 0)),
            pl.BlockSpec((1, s, SWA_KV_HEADS * HEAD_DIM), lambda hk, bi: (bi, 0, 0)),
            pl.BlockSpec((1, nblk, HEAD_DIM, w), lambda hk, bi: (bi, 0, hk, 0)),
            pl.BlockSpec((2, None, 2 * w, gw), lambda hk, bi: (0, hk, 0, 0)),
            pl.BlockSpec((1, 1, gw), lambda hk, bi: (hk, 0, 0)),
        ] + slab_specs,
        out_specs=[pl.BlockSpec((1, s, SWA_GROUP * HEAD_DIM), lambda hk, bi: (bi, 0, hk))] + slab_specs,
        out_shape=[jax.ShapeDtypeStruct((b, s, n_qa), BF16)]
                  + [jax.ShapeDtypeStruct(wt.shape, BF16) for wt in f32_weights],
        scratch_shapes=[pltpu.VMEM((2, 2 * w, gw), F32), pltpu.VMEM((2, 2 * w, gw), BF16),
                        pltpu.VMEM((2, 1, gw), F32)],
        compiler_params=pltpu.CompilerParams(
            dimension_semantics=("parallel", "parallel"), vmem_limit_bytes=VMEM_LIMIT_BYTES),
        name="swa_attn",
    )(bounded.reshape(1), qa_t, ka, va_t, bias, sink_cols, *f32_weights)
    return outs[0], outs[1:]


def _diff_kernel(bounded_ref, iq_tbl_ref, j_tbl_ref, q_ref, qpos_ref, k_ref, kpos_ref, v_ref, mask_ref, lam_ref,
                 gsub_ref, o_ref, acc_ref, l_ref, p_ref, *, lam_init):
    nt, t_q = q_ref.shape[1], q_ref.shape[3]
    hpb = acc_ref.shape[0]
    n_plain = nt * (nt - 1) // 2
    dv = DIFF_V_DIM
    lp = lam_ref[...]
    lam = (jnp.exp(jnp.sum(lp[0:1] * lp[1:2], axis=-1, keepdims=True))
           - jnp.exp(jnp.sum(lp[2:3] * lp[3:4], axis=-1, keepdims=True)) + lam_init)
    zero = jnp.zeros((HEAD_DIM, t_q), BF16)

    def padded_q(iq, hh):
        q = q_ref[0, iq, hh * dv:(hh + 1) * dv, :]
        return jnp.concatenate([jnp.concatenate([q[:HEAD_DIM], zero], axis=0),
                                jnp.concatenate([zero, q[HEAD_DIM:]], axis=0)], axis=1)

    def logits(iq, j, hh, masked):
        rows = pl.ds(pl.multiple_of(j * t_q, t_q), t_q)
        lhs = jnp.concatenate([k_ref[0, rows, hh * dv:(hh + 1) * dv], kpos_ref[rows, :]], axis=1)
        rhs = jnp.concatenate([padded_q(iq, hh), qpos_ref[iq, hh * dv:(hh + 1) * dv, :]], axis=0)
        s = jnp.dot(lhs, rhs, preferred_element_type=F32)
        return s + mask_ref[...] if masked else s

    def finalize(iq, hh, acc, l):
        acc = acc * (1.0 / l)
        od = acc[:, :t_q] - lam * acc[:, t_q:]
        ms = jnp.mean(od * od, axis=0, keepdims=True)
        y = od * lax.rsqrt(ms + EPS) * gsub_ref[...]
        o_ref[0, iq, hh * dv:(hh + 1) * dv, :] = y.astype(BF16)

    def probs(t, slot, *, base, masked):
        iq, j = iq_tbl_ref[base + t], j_tbl_ref[base + t]
        for hh in range(hpb):
            p = jnp.exp2(logits(iq, j, hh, masked))
            l_ref[hh, iq] += jnp.sum(p.reshape(t_q // 8, 8, 2 * t_q), axis=0)
            p_ref[slot, hh] = p.astype(BF16)

    def pv(t, slot, *, base):
        iq, j = iq_tbl_ref[base + t], j_tbl_ref[base + t]
        for hh in range(hpb):
            acc_ref[hh, iq] += jnp.dot(v_ref[0, j, hh * dv:(hh + 1) * dv, :], p_ref[slot, hh],
                                       preferred_element_type=F32)

    @pl.when(bounded_ref[0] != 0)
    def _():
        acc_ref[...] = jnp.zeros_like(acc_ref)
        l_ref[...] = jnp.zeros_like(l_ref)
        _staged_pipeline(n_plain, [functools.partial(probs, base=0, masked=False),
                                   functools.partial(pv, base=0)])
        _staged_pipeline(nt, [functools.partial(probs, base=n_plain, masked=True),
                              functools.partial(pv, base=n_plain)])

        def fin(iq, carry):
            for hh in range(hpb):
                finalize(iq, hh, acc_ref[hh, iq], jnp.sum(l_ref[hh, iq], axis=0, keepdims=True))
            return carry

        lax.fori_loop(0, nt, fin, 0)

    @pl.when(bounded_ref[0] == 0)
    def _():
        def q_block(iq, carry):
            acc_ref[:, 0] = jnp.zeros((hpb, dv, 2 * t_q), F32)

            def step(j, stats, masked):
                out = []
                for hh in range(hpb):
                    m, l = stats[2 * hh], stats[2 * hh + 1]
                    s = logits(iq, j, hh, masked)
                    mnew = jnp.maximum(m, jnp.max(s, axis=0, keepdims=True))
                    alpha = jnp.exp2(m - mnew)
                    p = jnp.exp2(s - mnew)
                    out += [mnew, alpha * l + jnp.sum(p, axis=0, keepdims=True)]
                    acc_ref[hh, 0] = alpha * acc_ref[hh, 0] + jnp.dot(
                        v_ref[0, j, hh * dv:(hh + 1) * dv, :], p.astype(BF16), preferred_element_type=F32)
                return tuple(out)

            init = (jnp.full((1, 2 * t_q), NEG_INF, F32), jnp.zeros((1, 2 * t_q), F32)) * hpb
            stats = lax.fori_loop(0, iq, lambda j, c: step(j, c, False), init)
            stats = step(iq, stats, True)
            for hh in range(hpb):
                finalize(iq, hh, acc_ref[hh, 0], stats[2 * hh + 1])
            return carry

        lax.fori_loop(0, nt, q_block, 0)


def _split_bf16(x, parts=3):
    out, rest = [], np.asarray(x, np.float64)
    for _ in range(parts):
        piece = rest.astype(BF16).astype(np.float64)
        out.append(piece)
        rest = rest - piece
    return out


def _alibi_features(slopes, s, t):
    ROW = DIFF_V_DIM
    RADIX = 128
    assert s <= RADIX * 256
    kpos = np.arange(s)
    kfeat = np.zeros((s, ROW), np.float64)
    qfeat = np.zeros((s // t, len(slopes) * ROW, 2 * t), np.float64)
    qpos = (np.arange(s // t)[:, None] * t + np.arange(2 * t)[None, :] % t).astype(np.float64)
    for h, slope in enumerate(slopes):
        for i, piece in enumerate(_split_bf16(slope * LOG2E)):
            c = 5 * i
            kfeat[:, c], kfeat[:, c + 1], kfeat[:, c + 2:c + 5] = kpos // RADIX, kpos % RADIX, 1.0
            r = h * ROW + c
            qfeat[:, r], qfeat[:, r + 1] = float(piece) * RADIX, float(piece)
            for n, part in enumerate(_split_bf16(float(piece) * qpos)):
                qfeat[:, r + 2 + n] = -part
    return jnp.asarray(kfeat, BF16), jnp.asarray(qfeat, BF16)


def _diff_attention(qd_t, kd, vd_t, lam_params, g_sub, slopes, lam_init, logit_bound, hpb=DIFF_HEADS_PER_STEP):
    b, nt, n_qd, t = qd_t.shape
    s = nt * t
    nh = n_qd // DIFF_V_DIM
    kfeat, qfeat = _alibi_features(slopes, s, t)
    causal = np.arange(t)[:, None] <= np.arange(2 * t)[None, :] % t
    mask = jnp.asarray(np.where(causal, 0.0, -np.inf), F32)
    gsub = jnp.broadcast_to((g_sub * (1.0 - lam_init))[:, None], (DIFF_V_DIM, t)).astype(F32)
    bounded = jnp.asarray(logit_bound <= MAX_UNSHIFTED_LOGIT, jnp.int32)
    pairs = [(iq, j) for iq in range(nt) for j in range(iq)] + [(iq, iq) for iq in range(nt)]
    iq_tbl = jnp.asarray([pq[0] for pq in pairs], jnp.int32)
    j_tbl = jnp.asarray([pq[1] for pq in pairs], jnp.int32)
    kern = functools.partial(_diff_kernel, lam_init=lam_init)
    return pl.pallas_call(
        kern,
        grid=(nh // hpb, b),
        in_specs=[
            pl.BlockSpec(memory_space=pltpu.SMEM),
            pl.BlockSpec(memory_space=pltpu.SMEM),
            pl.BlockSpec(memory_space=pltpu.SMEM),
            pl.BlockSpec((1, nt, hpb * DIFF_V_DIM, t), lambda hg, bi: (bi, 0, hg, 0)),
            pl.BlockSpec((nt, hpb * DIFF_V_DIM, 2 * t), lambda hg, bi: (0, hg, 0)),
            pl.BlockSpec((1, s, hpb * DIFF_V_DIM), lambda hg, bi: (bi, 0, hg)),
            pl.BlockSpec((s, DIFF_V_DIM), lambda hg, bi: (0, 0)),
            pl.BlockSpec((1, nt, hpb * DIFF_V_DIM, t), lambda hg, bi: (bi, 0, hg, 0)),
            pl.BlockSpec((t, 2 * t), lambda hg, bi: (0, 0)),
            pl.BlockSpec((4, HEAD_DIM), lambda hg, bi: (0, 0)),
            pl.BlockSpec((DIFF_V_DIM, t), lambda hg, bi: (0, 0)),
        ],
        out_specs=pl.BlockSpec((1, nt, hpb * DIFF_V_DIM, t), lambda hg, bi: (bi, 0, hg, 0)),
        out_shape=jax.ShapeDtypeStruct((b, nt, n_qd, t), BF16),
        scratch_shapes=[pltpu.VMEM((hpb, nt, DIFF_V_DIM, 2 * t), F32),
                        pltpu.VMEM((hpb, nt, 8, 2 * t), F32),
                        pltpu.VMEM((2, hpb, t, 2 * t), BF16)],
        compiler_params=pltpu.CompilerParams(
            dimension_semantics=("parallel", "parallel"), vmem_limit_bytes=VMEM_LIMIT_BYTES),
        name="diff_attn",
    )(bounded.reshape(1), iq_tbl, j_tbl, qd_t, qfeat, kd, kfeat, vd_t, mask, lam_params, gsub)


def _outproj_kernel(x_ref, ya_ref, yd_ref, wa_ref, wd_ref, o_ref):
    yd_t = jnp.concatenate([yd_ref[k] for k in range(yd_ref.shape[0])], axis=1)
    o_ref[...] = (x_ref[...]
                  + jnp.dot(ya_ref[...], wa_ref[...], preferred_element_type=F32)
                  + lax.dot_general(yd_t, wd_ref[...], (((0,), (0,)), ((), ())), preferred_element_type=F32))


def _outproj(x2, ya2, yd_t, wo, tm=512):
    n, d = x2.shape
    tm = min(tm, n)
    n_a, n_d, t = ya2.shape[1], yd_t.shape[1], yd_t.shape[2]
    assert n % tm == 0 and tm % t == 0 and n_a == n_d and wo.shape == (n_a + n_d, d)
    return pl.pallas_call(
        _outproj_kernel,
        grid=(n // tm,),
        in_specs=[
            pl.BlockSpec((tm, d), lambda i: (i, 0)),
            pl.BlockSpec((tm, n_a), lambda i: (i, 0)),
            pl.BlockSpec((tm // t, n_d, t), lambda i: (i, 0, 0)),
            _resident((n_a, d), lambda i: (0, 0)),
            _resident((n_d, d), lambda i: (1, 0)),
        ],
        out_specs=pl.BlockSpec((tm, d), lambda i: (i, 0)),
        out_shape=jax.ShapeDtypeStruct((n, d), F32),
        compiler_params=pltpu.CompilerParams(
            dimension_semantics=("parallel",), vmem_limit_bytes=VMEM_LIMIT_BYTES),
        name="outproj",
    )(x2, ya2, yd_t, wo, wo)


def _ffn_kernel(h_ref, g_ref, wg_ref, wu_ref, wd_ref, o_ref, u_ref):
    f = pl.program_id(1)

    @pl.when(f == 0)
    def _():
        h = h_ref[...]
        u_ref[...] = _rms_rows(h, g_ref[...]).astype(BF16)
        o_ref[...] = h

    u = u_ref[...]
    gate = jnp.dot(u, wg_ref[...], preferred_element_type=F32)
    up = jnp.dot(u, wu_ref[...], preferred_element_type=F32)
    act = (gate * (1.0 / (1.0 + jnp.exp(-gate))) * up).astype(BF16)
    o_ref[...] += jnp.dot(act, wd_ref[...], preferred_element_type=F32)


def _ffn(h2, g_ffn, wg, wu, wd, tm=1024, tf=512):
    n, d = h2.shape
    tm = min(tm, n)
    assert n % tm == 0
    dff = wg.shape[1]
    return pl.pallas_call(
        _ffn_kernel,
        grid=(n // tm, dff // tf),
        in_specs=[
            pl.BlockSpec((tm, d), lambda i, f: (i, 0)),
            _resident((1, d), lambda i, f: (0, 0)),
            pl.BlockSpec((d, tf), lambda i, f: (0, f)),
            pl.BlockSpec((d, tf), lambda i, f: (0, f)),
            pl.BlockSpec((tf, d), lambda i, f: (f, 0)),
        ],
        out_specs=pl.BlockSpec((tm, d), lambda i, f: (i, 0)),
        out_shape=jax.ShapeDtypeStruct((n, d), F32),
        scratch_shapes=[pltpu.VMEM((tm, d), BF16)],
        compiler_params=pltpu.CompilerParams(
            dimension_semantics=("parallel", "arbitrary"), vmem_limit_bytes=VMEM_LIMIT_BYTES),
        name="ffn",
    )(h2, g_ffn[None, :], wg, wu, wd)


def _ple_kernel(h_ref, p_ref, g_ref, wg_ref, wp_ref, go_ref, o_ref, *, sub, ncol):
    d = h_ref.shape[1]
    for st in range(h_ref.shape[0] // sub):
        rows = slice(st * sub, (st + 1) * sub)
        u = _rms_rows(h_ref[rows, :], g_ref[...]).astype(BF16)
        pp = jnp.dot(p_ref[rows, :].astype(BF16), wp_ref[...], preferred_element_type=F32)
        ppn = _rms_rows(pp, go_ref[...])
        for c in range(d // ncol):
            cols = slice(c * ncol, (c + 1) * ncol)
            z = jnp.dot(u, wg_ref[:, cols], preferred_element_type=F32)
            gate = 1.0 / (1.0 + jnp.exp(-z))
            o_ref[rows, cols] = h_ref[rows, cols] + gate * ppn[:, cols]


def _ple(h2, p2, g_ple, wg, wp, g_out, tm=1024, sub=512, ncol=512):
    n, d = h2.shape
    tm = min(tm, n)
    assert n % tm == 0
    sub = min(sub, tm)
    assert tm % sub == 0
    const = lambda shape: _resident(shape, lambda i: (0, 0))
    return pl.pallas_call(
        functools.partial(_ple_kernel, sub=sub, ncol=min(ncol, d)),
        grid=(n // tm,),
        in_specs=[
            pl.BlockSpec((tm, d), lambda i: (i, 0)),
            pl.BlockSpec((tm, p2.shape[1]), lambda i: (i, 0)),
            const((1, d)), const(wg.shape), const(wp.shape), const((1, d)),
        ],
        out_specs=pl.BlockSpec((tm, d), lambda i: (i, 0)),
        out_shape=jax.ShapeDtypeStruct((n, d), F32),
        compiler_params=pltpu.CompilerParams(
            dimension_semantics=("parallel",), vmem_limit_bytes=VMEM_LIMIT_BYTES),
        name="ple",
    )(h2, p2, g_ple[None, :], wg, wp, g_out[None, :])


def _alibi_slopes(n):
    return [2.0 ** (-8.0 * (h + 1) / n) for h in range(n)]


def _logit_bound(q_gain, k_gain):
    return 1.02 * math.sqrt(HEAD_DIM) * jnp.max(jnp.abs(q_gain)) * jnp.max(jnp.abs(k_gain))


def kernel(x, p, g_attn, w_in, qn_swa, kn_swa, sinks, qn_diff, kn_diff, lambda_q1, lambda_k1, lambda_q2,
           lambda_k2, g_sub, w_out, g_ffn, w_gate, w_up, w_down, g_ple, w_ple_gate, w_ple_proj, g_ple_out):
    b, s, d = x.shape
    depth = p.shape[0]
    n_qa = d // 2
    n_ka = n_va = SWA_KV_HEADS * HEAD_DIM
    n_qd = n_kd = n_vd = d // 2
    diff_heads = n_vd // DIFF_V_DIM
    swa_heads = n_qa // HEAD_DIM
    assert swa_heads == SWA_KV_HEADS * SWA_GROUP and s % TOKEN_TILE == 0
    c = np.cumsum([0, n_qa, n_ka, n_va, n_qd, n_kd, n_vd])
    h = x
    for i in range(depth):
        lam_init = 0.8 - 0.6 * math.exp(-0.3 * i)
        wfm = _cast_columns(w_in[i], [(c[0], c[1]), (c[1], c[3]), (c[3], c[4]), (c[5], c[6])], transpose=True,
                            cb=2 * SWA_BLOCK)
        fm_rows = (0, n_qa + n_ka, n_qa + n_ka + n_va, n_qa + n_ka + n_va + n_qd)
        wtm = _cast_columns(w_in[i], [(c[1], c[2]), (c[4], c[5])], transpose=False)
        qa_t, va_t, qd_t, vd_t, ka, kd = _inproj(
            h, g_attn[i], wfm, wtm, qn_swa[i], kn_swa[i], qn_diff[i], kn_diff[i],
            (n_qa, n_ka, n_va, n_qd, n_kd, n_vd), fm_rows)
        ya, (wg, wu, wd, wo, wpg) = _swa_attention(
            qa_t, ka, va_t, sinks[i], _alibi_slopes(swa_heads), _logit_bound(qn_swa[i], kn_swa[i]),
            f32_weights=(w_gate[i], w_up[i], w_down[i], w_out[i], w_ple_gate[i]))
        lam_params = jnp.stack([lambda_q1[i], lambda_k1[i], lambda_q2[i], lambda_k2[i]]).astype(F32)
        yd = _diff_attention(qd_t, kd, vd_t, lam_params, g_sub[i], _alibi_slopes(diff_heads), lam_init,
                             _logit_bound(qn_diff[i], kn_diff[i]))
        h2 = _outproj(h.reshape(b * s, d), ya.reshape(b * s, n_qa), yd.reshape(-1, n_vd, yd.shape[-1]), wo)
        h2 = _ffn(h2, g_ffn[i], wg, wu, wd)
        h2 = _ple(h2, p[i].reshape(b * s, -1), g_ple[i], wpg, w_ple_proj[i].astype(BF16), g_ple_out[i])
        h = h2.reshape(b, s, d)
    return h
```

```python
import functools
import math

import jax
import jax.numpy as jnp
import numpy as np
from jax import lax
from jax.experimental import pallas as pl
from jax.experimental.pallas import tpu as pltpu

F32 = jnp.float32
BF16 = jnp.bfloat16

HEAD_DIM = 64
SWA_BLOCK = 128
SWA_KV_HEADS = 2
SWA_GROUP = 8
DIFF_V_DIM = 2 * HEAD_DIM
EPS = 1e-6
NEG_INF = float("-inf")
LOG2E = math.log2(math.e)
MAX_UNSHIFTED_LOGIT = 60.0

TOKEN_TILE = 256
INPROJ_TILES_PER_STEP = 4
DIFF_HEADS_PER_STEP = 4
SWA_CAST_SLAB_BYTES = 12 * 1024 * 1024
VMEM_LIMIT_BYTES = 56 * 1024 * 1024

_NT = (((1,), (1,)), ((), ()))
_TN = (((0,), (0,)), ((), ()))


def _resident(shape, index_map):
    return pl.BlockSpec(shape, index_map, pipeline_mode=pl.Buffered(1))


def _rms_rows(x, gain):
    ms = jnp.mean(x * x, axis=-1, keepdims=True)
    return x * lax.rsqrt(ms + EPS) * gain


def _cast_columns_kernel(blk_ref, w_ref, o_ref, *, transpose):
    del blk_ref
    w = w_ref[...]
    o_ref[...] = (w.T if transpose else w).astype(BF16)


def _cast_columns(w, col_ranges, transpose, cb=128):
    k, _ = w.shape
    blocks = [c0 // cb + i for c0, c1 in col_ranges for i in range((c1 - c0) // cb)]
    assert all(c0 % cb == 0 and c1 % cb == 0 for c0, c1 in col_ranges)
    n = len(blocks) * cb
    return pl.pallas_call(
        functools.partial(_cast_columns_kernel, transpose=transpose),
        grid_spec=pltpu.PrefetchScalarGridSpec(
            num_scalar_prefetch=1,
            grid=(len(blocks),),
            in_specs=[pl.BlockSpec((k, cb), lambda i, blk: (0, blk[i]))],
            out_specs=pl.BlockSpec((cb, k), lambda i, blk: (i, 0)) if transpose
            else pl.BlockSpec((k, cb), lambda i, blk: (0, i)),
        ),
        out_shape=jax.ShapeDtypeStruct((n, k) if transpose else (k, n), BF16),
        compiler_params=pltpu.CompilerParams(dimension_semantics=("parallel",)),
        name="weight_layout_t" if transpose else "weight_layout",
    )(jnp.asarray(blocks, jnp.int32), w)


def _inproj_kernel(x_ref, g_ref, wfm_ref, gqa_ref, gqd_ref, gka_ref, gkd_ref,
                   qa_ref, va_ref, qd_ref, vd_ref, ka_ref, kd_ref, *, n_qa, n_va, n_qd, n_vd, n_ka, n_kd, fm_rows):
    tm = TOKEN_TILE
    per_tile = tm // SWA_BLOCK

    def headnorm_fm(z, gain, store):
        for h in range(z.shape[0] // HEAD_DIM):
            zh = z[h * HEAD_DIM:(h + 1) * HEAD_DIM, :]
            ms = jnp.mean(zh * zh, axis=0, keepdims=True)
            store(h, (zh * lax.rsqrt(ms + EPS) * gain).astype(BF16))

    for st in range(x_ref.shape[0] // tm):
        rows = slice(st * tm, (st + 1) * tm)
        u = _rms_rows(x_ref[rows, :], g_ref[...]).astype(BF16)

        def fm(row0, nrows):
            return lax.dot_general(wfm_ref[row0:row0 + nrows, :], u, _NT, preferred_element_type=F32)

        def store_qa(h, val):
            for t in range(per_tile):
                qa_ref[0, st * per_tile + t, h * HEAD_DIM:(h + 1) * HEAD_DIM, :] = (
                    val[:, t * SWA_BLOCK:(t + 1) * SWA_BLOCK])

        def store_qd(h, val):
            qd_ref[0, st, h * HEAD_DIM:(h + 1) * HEAD_DIM, :] = val

        def store_ka(h, val):
            for t in range(per_tile):
                ka_ref[0, st * per_tile + t, h * HEAD_DIM:(h + 1) * HEAD_DIM, :] = (
                    val[:, t * SWA_BLOCK:(t + 1) * SWA_BLOCK])

        def store_kd(h, val):
            kd_ref[0, st, h * HEAD_DIM:(h + 1) * HEAD_DIM, :] = val

        r_qa, r_ka, r_va, r_qd, r_kd, r_vd = fm_rows
        headnorm_fm(fm(r_qa, n_qa), gqa_ref[...], store_qa)
        headnorm_fm(fm(r_ka, n_ka), gka_ref[...], store_ka)
        zva = fm(r_va, n_va).astype(BF16)
        for t in range(per_tile):
            va_ref[0, st * per_tile + t] = zva[:, t * SWA_BLOCK:(t + 1) * SWA_BLOCK]
        headnorm_fm(fm(r_qd, n_qd), gqd_ref[...], store_qd)
        headnorm_fm(fm(r_kd, n_kd), gkd_ref[...], store_kd)
        vd_ref[0, st] = fm(r_vd, n_vd).astype(BF16)


def _inproj(x, g_attn, wfm, qn_swa, kn_swa, qn_diff, kn_diff, dims, fm_rows):
    b, s, d = x.shape
    n_qa, n_ka, n_va, n_qd, n_kd, n_vd = dims
    tm = TOKEN_TILE
    nt = s // tm
    tps = min(INPROJ_TILES_PER_STEP, nt)
    assert nt % tps == 0
    blk = tps * tm
    scale = LOG2E / math.sqrt(HEAD_DIM)
    gqa = jnp.broadcast_to((qn_swa * scale)[:, None], (HEAD_DIM, tm)).astype(F32)
    gqd = jnp.broadcast_to((qn_diff * scale)[:, None], (HEAD_DIM, tm)).astype(F32)
    gka = jnp.broadcast_to(kn_swa[:, None], (HEAD_DIM, tm)).astype(F32)
    gkd = jnp.broadcast_to(kn_diff[:, None], (HEAD_DIM, tm)).astype(F32)
    const = lambda shape: _resident(shape, lambda bi, ti: (0,) * len(shape))
    kern = functools.partial(_inproj_kernel, n_qa=n_qa, n_va=n_va, n_qd=n_qd, n_vd=n_vd, n_ka=n_ka, n_kd=n_kd,
                             fm_rows=fm_rows)
    return pl.pallas_call(
        kern,
        grid=(b, nt // tps),
        in_specs=[
            pl.BlockSpec((None, blk, d), lambda bi, ti: (bi, ti, 0)),
            const((1, d)), const(wfm.shape),
            const((HEAD_DIM, tm)), const((HEAD_DIM, tm)), const((HEAD_DIM, tm)), const((HEAD_DIM, tm)),
        ],
        out_specs=[
            pl.BlockSpec((1, blk // SWA_BLOCK, n_qa, SWA_BLOCK), lambda bi, ti: (bi, ti, 0, 0)),
            pl.BlockSpec((1, blk // SWA_BLOCK, n_va, SWA_BLOCK), lambda bi, ti: (bi, ti, 0, 0)),
            pl.BlockSpec((1, tps, n_qd, tm), lambda bi, ti: (bi, ti, 0, 0)),
            pl.BlockSpec((1, tps, n_vd, tm), lambda bi, ti: (bi, ti, 0, 0)),
            pl.BlockSpec((1, blk // SWA_BLOCK, n_ka, SWA_BLOCK), lambda bi, ti: (bi, ti, 0, 0)),
            pl.BlockSpec((1, tps, n_kd, tm), lambda bi, ti: (bi, ti, 0, 0)),
        ],
        out_shape=[
            jax.ShapeDtypeStruct((b, s // SWA_BLOCK, n_qa, SWA_BLOCK), BF16),
            jax.ShapeDtypeStruct((b, s // SWA_BLOCK, n_va, SWA_BLOCK), BF16),
            jax.ShapeDtypeStruct((b, nt, n_qd, tm), BF16),
            jax.ShapeDtypeStruct((b, nt, n_vd, tm), BF16),
            jax.ShapeDtypeStruct((b, s // SWA_BLOCK, n_ka, SWA_BLOCK), BF16),
            jax.ShapeDtypeStruct((b, nt, n_kd, tm), BF16),
        ],
        compiler_params=pltpu.CompilerParams(
            dimension_semantics=("parallel", "parallel"), vmem_limit_bytes=VMEM_LIMIT_BYTES),
        name="inproj",
    )(x, g_attn[None, :], wfm, gqa, gqd, gka, gkd)


def _staged_pipeline(n_steps, stages):
    depth = len(stages)

    def iteration(i, parity):
        for k in reversed(range(depth)):
            t = i - k
            if isinstance(i, int) and not 0 <= t < n_steps:
                continue
            stages[k](t, (parity - k) % 2)

    fill_end = min(depth - 1, n_steps)
    for i in range(fill_end):
        iteration(i, i % 2)
    n_pairs = (n_steps - fill_end) // 2

    def pair(m, carry):
        i = fill_end + 2 * m
        iteration(i, fill_end % 2)
        iteration(i + 1, (fill_end + 1) % 2)
        return carry

    lax.fori_loop(0, n_pairs, pair, 0)
    for i in range(fill_end + 2 * n_pairs, n_steps + depth - 1):
        iteration(i, i % 2)


def _swa_kernel(bounded_ref, q_ref, k_ref, v_ref, bias_ref, sink_ref, *refs, n_cast):
    cast_in, o_ref, cast_out = refs[:n_cast], refs[n_cast], refs[n_cast + 1:2 * n_cast + 1]
    s_ref, p_ref, d_ref = refs[2 * n_cast + 1:]
    for src, dst in zip(cast_in, cast_out):
        dst[...] = src[...].astype(BF16)

    hk = pl.program_id(0)
    nblk = q_ref.shape[1]
    w = SWA_BLOCK
    sink = sink_ref[0]

    def window_start(n):
        return jnp.maximum(n - 1, 0)

    def scores(n, slot):
        qblk = q_ref[0, n]
        qg = jnp.concatenate([qblk[g * HEAD_DIM:(g + 1) * HEAD_DIM, :] for g in range(SWA_GROUP)], axis=1)
        zero = jnp.zeros_like(qg)
        qpad = jnp.concatenate([jnp.where(hk == 0, qg, zero), jnp.where(hk == 1, qg, zero)], axis=0)
        first = window_start(n)
        kwin_t = jnp.concatenate([k_ref[0, first], k_ref[0, first + 1]], axis=1)
        s_ref[slot] = lax.dot_general(kwin_t, qpad, _TN, preferred_element_type=F32)

    def probs(n, slot, *, bounded):
        t = s_ref[slot] + bias_ref[jnp.minimum(n, 1)]
        if bounded:
            e = jnp.exp2(t)
            d_ref[slot] = jnp.sum(e, axis=0, keepdims=True) + jnp.exp2(sink)
        else:
            m = jnp.maximum(jnp.max(t, axis=0, keepdims=True), sink)
            e = jnp.exp2(t - m)
            d_ref[slot] = jnp.sum(e, axis=0, keepdims=True) + jnp.exp2(sink - m)
        p_ref[slot] = e.astype(BF16)

    def weighted_values(n, slot):
        first = window_start(n)
        vwin = jnp.concatenate([v_ref[0, first], v_ref[0, first + 1]], axis=1)
        o = jnp.dot(vwin, p_ref[slot], preferred_element_type=F32) * (1.0 / d_ref[slot])
        for gp in range(SWA_GROUP // 2):
            pair = jnp.concatenate([o[:, (2 * gp) * w:(2 * gp + 1) * w],
                                    o[:, (2 * gp + 1) * w:(2 * gp + 2) * w]], axis=0)
            o_ref[0, pl.ds(pl.multiple_of(n * w, w), w), gp * 2 * HEAD_DIM:(gp + 1) * 2 * HEAD_DIM] = (
                pair.T.astype(BF16))

    @pl.when(bounded_ref[0] != 0)
    def _():
        _staged_pipeline(nblk, [scores, functools.partial(probs, bounded=True), weighted_values])

    @pl.when(bounded_ref[0] == 0)
    def _():
        def block(n, carry):
            scores(n, 0)
            probs(n, 0, bounded=False)
            weighted_values(n, 0)
            return carry

        lax.fori_loop(0, nblk, block, 0)


def _swa_bias(slopes):
    w = SWA_BLOCK
    kj = np.arange(2 * w)[:, None]
    qi = np.arange(w)[None, :]
    out = np.empty((2, SWA_KV_HEADS, 2 * w, SWA_GROUP * w), np.float32)
    for first, dist in ((0, qi - kj), (1, qi - kj + w)):
        valid = (dist >= 0) & (dist < w)
        for hk in range(SWA_KV_HEADS):
            for g in range(SWA_GROUP):
                sl = slopes[hk * SWA_GROUP + g]
                out[first, hk, :, g * w:(g + 1) * w] = np.where(valid, -sl * dist, -np.inf)
    return out


def _swa_attention(qa_t, ka, va_t, sinks, slopes, logit_bound, f32_weights=()):
    b, nblk, n_qa, w = qa_t.shape
    s = nblk * w
    gw = SWA_GROUP * w
    bias = jnp.asarray(_swa_bias(slopes) * LOG2E)
    sinks = sinks.astype(F32)
    sink_cols = jnp.repeat((sinks * LOG2E).reshape(SWA_KV_HEADS, 1, SWA_GROUP), w, axis=-1)
    bounded = jnp.asarray(jnp.maximum(logit_bound, jnp.max(jnp.abs(sinks))) <= MAX_UNSHIFTED_LOGIT, jnp.int32)
    n_steps = SWA_KV_HEADS * b
    slab_bytes = sum(wt.size * 4 // n_steps for wt in f32_weights)
    if (any(wt.shape[0] % (16 * n_steps) for wt in f32_weights)
            or slab_bytes > SWA_CAST_SLAB_BYTES):
        ya, _ = _swa_attention(qa_t, ka, va_t, sinks, slopes, logit_bound)
        return ya, tuple(wt.astype(BF16) for wt in f32_weights)
    slab_specs = [pl.BlockSpec((wt.shape[0] // n_steps, wt.shape[1]), lambda hk, bi: (hk * b + bi, 0))
                  for wt in f32_weights]
    outs = pl.pallas_call(
        functools.partial(_swa_kernel, n_cast=len(f32_weights)),
        grid=(SWA_KV_HEADS, b),
        in_specs=[
            pl.BlockSpec(memory_space=pltpu.SMEM),
            pl.BlockSpec((1, nblk, SWA_GROUP * HEAD_DIM, w), lambda hk, bi: (bi, 0, hk, 0)),
            pl.BlockSpec((1, nblk, SWA_KV_HEADS * HEAD_DIM, w), lambda hk, bi: (bi, 0, 0, 0)),
            pl.BlockSpec((1, nblk, HEAD_DIM, w), lambda hk, bi: (bi, 0, hk, 0)),
            pl.BlockSpec((2, None, 2 * w, gw), lambda hk, bi: (0, hk, 0, 0)),
            pl.BlockSpec((1, 1, gw), lambda hk, bi: (hk, 0, 0)),
        ] + slab_specs,
        out_specs=[pl.BlockSpec((1, s, SWA_GROUP * HEAD_DIM), lambda hk, bi: (bi, 0, hk))] + slab_specs,
        out_shape=[jax.ShapeDtypeStruct((b, s, n_qa), BF16)]
                  + [jax.ShapeDtypeStruct(wt.shape, BF16) for wt in f32_weights],
        scratch_shapes=[pltpu.VMEM((2, 2 * w, gw), F32), pltpu.VMEM((2, 2 * w, gw), BF16),
                        pltpu.VMEM((2, 1, gw), F32)],
        compiler_params=pltpu.CompilerParams(
            dimension_semantics=("parallel", "parallel"), vmem_limit_bytes=VMEM_LIMIT_BYTES),
        name="swa_attn",
    )(bounded.reshape(1), qa_t, ka, va_t, bias, sink_cols, *f32_weights)
    return outs[0], outs[1:]


def _diff_kernel(bounded_ref, iq_tbl_ref, j_tbl_ref, q_ref, qpos_ref, k_ref, kpos_ref, v_ref, mask_ref, lam_ref,
                 gsub_ref, o_ref, acc_ref, l_ref, p_ref, *, lam_init):
    nt, t_q = q_ref.shape[1], q_ref.shape[3]
    hpb = acc_ref.shape[0]
    n_plain = nt * (nt - 1) // 2
    dv = DIFF_V_DIM
    lp = lam_ref[...]
    lam = (jnp.exp(jnp.sum(lp[0:1] * lp[1:2], axis=-1, keepdims=True))
           - jnp.exp(jnp.sum(lp[2:3] * lp[3:4], axis=-1, keepdims=True)) + lam_init)
    zero = jnp.zeros((HEAD_DIM, t_q), BF16)

    def padded_q(iq, hh):
        q = q_ref[0, iq, hh * dv:(hh + 1) * dv, :]
        return jnp.concatenate([jnp.concatenate([q[:HEAD_DIM], zero], axis=0),
                                jnp.concatenate([zero, q[HEAD_DIM:]], axis=0)], axis=1)

    def logits(iq, j, hh, masked):
        lhs_t = jnp.concatenate([k_ref[0, j, hh * dv:(hh + 1) * dv, :], kpos_ref[j]], axis=0)
        rhs = jnp.concatenate([padded_q(iq, hh), qpos_ref[iq, hh * dv:(hh + 1) * dv, :]], axis=0)
        s = lax.dot_general(lhs_t, rhs, _TN, preferred_element_type=F32)
        return s + mask_ref[...] if masked else s

    def finalize(iq, hh, acc, l):
        acc = acc * (1.0 / l)
        od = acc[:, :t_q] - lam * acc[:, t_q:]
        ms = jnp.mean(od * od, axis=0, keepdims=True)
        y = od * lax.rsqrt(ms + EPS) * gsub_ref[...]
        o_ref[0, iq, hh * dv:(hh + 1) * dv, :] = y.astype(BF16)

    def probs(t, slot, *, base, masked):
        iq, j = iq_tbl_ref[base + t], j_tbl_ref[base + t]
        for hh in range(hpb):
            p = jnp.exp2(logits(iq, j, hh, masked))
            l_ref[hh, iq] += jnp.sum(p.reshape(t_q // 8, 8, 2 * t_q), axis=0)
            p_ref[slot, hh] = p.astype(BF16)

    def pv(t, slot, *, base):
        iq, j = iq_tbl_ref[base + t], j_tbl_ref[base + t]
        for hh in range(hpb):
            acc_ref[hh, iq] += jnp.dot(v_ref[0, j, hh * dv:(hh + 1) * dv, :], p_ref[slot, hh],
                                       preferred_element_type=F32)

    @pl.when(bounded_ref[0] != 0)
    def _():
        acc_ref[...] = jnp.zeros_like(acc_ref)
        l_ref[...] = jnp.zeros_like(l_ref)
        _staged_pipeline(n_plain, [functools.partial(probs, base=0, masked=False),
                                   functools.partial(pv, base=0)])
        _staged_pipeline(nt, [functools.partial(probs, base=n_plain, masked=True),
                              functools.partial(pv, base=n_plain)])

        def fin(iq, carry):
            for hh in range(hpb):
                finalize(iq, hh, acc_ref[hh, iq], jnp.sum(l_ref[hh, iq], axis=0, keepdims=True))
            return carry

        lax.fori_loop(0, nt, fin, 0)

    @pl.when(bounded_ref[0] == 0)
    def _():
        def q_block(iq, carry):
            acc_ref[:, 0] = jnp.zeros((hpb, dv, 2 * t_q), F32)

            def step(j, stats, masked):
                out = []
                for hh in range(hpb):
                    m, l = stats[2 * hh], stats[2 * hh + 1]
                    s = logits(iq, j, hh, masked)
                    mnew = jnp.maximum(m, jnp.max(s, axis=0, keepdims=True))
                    alpha = jnp.exp2(m - mnew)
                    p = jnp.exp2(s - mnew)
                    out += [mnew, alpha * l + jnp.sum(p, axis=0, keepdims=True)]
                    acc_ref[hh, 0] = alpha * acc_ref[hh, 0] + jnp.dot(
                        v_ref[0, j, hh * dv:(hh + 1) * dv, :], p.astype(BF16), preferred_element_type=F32)
                return tuple(out)

            init = (jnp.full((1, 2 * t_q), NEG_INF, F32), jnp.zeros((1, 2 * t_q), F32)) * hpb
            stats = lax.fori_loop(0, iq, lambda j, c: step(j, c, False), init)
            stats = step(iq, stats, True)
            for hh in range(hpb):
                finalize(iq, hh, acc_ref[hh, 0], stats[2 * hh + 1])
            return carry

        lax.fori_loop(0, nt, q_block, 0)


def _split_bf16(x, parts=3):
    out, rest = [], np.asarray(x, np.float64)
    for _ in range(parts):
        piece = rest.astype(BF16).astype(np.float64)
        out.append(piece)
        rest = rest - piece
    return out


def _alibi_features(slopes, s, t):
    ROW = DIFF_V_DIM
    RADIX = 128
    assert s <= RADIX * 256
    kpos = np.arange(s)
    kfeat = np.zeros((s, ROW), np.float64)
    qfeat = np.zeros((s // t, len(slopes) * ROW, 2 * t), np.float64)
    qpos = (np.arange(s // t)[:, None] * t + np.arange(2 * t)[None, :] % t).astype(np.float64)
    for h, slope in enumerate(slopes):
        for i, piece in enumerate(_split_bf16(slope * LOG2E)):
            c = 5 * i
            kfeat[:, c], kfeat[:, c + 1], kfeat[:, c + 2:c + 5] = kpos // RADIX, kpos % RADIX, 1.0
            r = h * ROW + c
            qfeat[:, r], qfeat[:, r + 1] = float(piece) * RADIX, float(piece)
            for n, part in enumerate(_split_bf16(float(piece) * qpos)):
                qfeat[:, r + 2 + n] = -part
    kfeat_t = kfeat.reshape(s // t, t, ROW).transpose(0, 2, 1)
    return jnp.asarray(kfeat_t, BF16), jnp.asarray(qfeat, BF16)


def _diff_attention(qd_t, kd, vd_t, lam_params, g_sub, slopes, lam_init, logit_bound, hpb=DIFF_HEADS_PER_STEP):
    b, nt, n_qd, t = qd_t.shape
    s = nt * t
    nh = n_qd // DIFF_V_DIM
    kfeat, qfeat = _alibi_features(slopes, s, t)
    causal = np.arange(t)[:, None] <= np.arange(2 * t)[None, :] % t
    mask = jnp.asarray(np.where(causal, 0.0, -np.inf), F32)
    gsub = jnp.broadcast_to((g_sub * (1.0 - lam_init))[:, None], (DIFF_V_DIM, t)).astype(F32)
    bounded = jnp.asarray(logit_bound <= MAX_UNSHIFTED_LOGIT, jnp.int32)
    pairs = [(iq, j) for iq in range(nt) for j in range(iq)] + [(iq, iq) for iq in range(nt)]
    iq_tbl = jnp.asarray([pq[0] for pq in pairs], jnp.int32)
    j_tbl = jnp.asarray([pq[1] for pq in pairs], jnp.int32)
    kern = functools.partial(_diff_kernel, lam_init=lam_init)
    return pl.pallas_call(
        kern,
        grid=(nh // hpb, b),
        in_specs=[
            pl.BlockSpec(memory_space=pltpu.SMEM),
            pl.BlockSpec(memory_space=pltpu.SMEM),
            pl.BlockSpec(memory_space=pltpu.SMEM),
            pl.BlockSpec((1, nt, hpb * DIFF_V_DIM, t), lambda hg, bi: (bi, 0, hg, 0)),
            pl.BlockSpec((nt, hpb * DIFF_V_DIM, 2 * t), lambda hg, bi: (0, hg, 0)),
            pl.BlockSpec((1, nt, hpb * DIFF_V_DIM, t), lambda hg, bi: (bi, 0, hg, 0)),
            pl.BlockSpec((nt, DIFF_V_DIM, t), lambda hg, bi: (0, 0, 0)),
            pl.BlockSpec((1, nt, hpb * DIFF_V_DIM, t), lambda hg, bi: (bi, 0, hg, 0)),
            pl.BlockSpec((t, 2 * t), lambda hg, bi: (0, 0)),
            pl.BlockSpec((4, HEAD_DIM), lambda hg, bi: (0, 0)),
            pl.BlockSpec((DIFF_V_DIM, t), lambda hg, bi: (0, 0)),
        ],
        out_specs=pl.BlockSpec((1, nt, hpb * DIFF_V_DIM, t), lambda hg, bi: (bi, 0, hg, 0)),
        out_shape=jax.ShapeDtypeStruct((b, nt, n_qd, t), BF16),
        scratch_shapes=[pltpu.VMEM((hpb, nt, DIFF_V_DIM, 2 * t), F32),
                        pltpu.VMEM((hpb, nt, 8, 2 * t), F32),
                        pltpu.VMEM((2, hpb, t, 2 * t), BF16)],
        compiler_params=pltpu.CompilerParams(
            dimension_semantics=("parallel", "parallel"), vmem_limit_bytes=VMEM_LIMIT_BYTES),
        name="diff_attn",
    )(bounded.reshape(1), iq_tbl, j_tbl, qd_t, qfeat, kd, kfeat, vd_t, mask, lam_params, gsub)


def _outproj_kernel(x_ref, ya_ref, yd_ref, wa_ref, wd_ref, o_ref):
    yd_t = jnp.concatenate([yd_ref[k] for k in range(yd_ref.shape[0])], axis=1)
    o_ref[...] = (x_ref[...]
                  + jnp.dot(ya_ref[...], wa_ref[...], preferred_element_type=F32)
                  + lax.dot_general(yd_t, wd_ref[...], (((0,), (0,)), ((), ())), preferred_element_type=F32))


def _outproj(x2, ya2, yd_t, wo, tm=512):
    n, d = x2.shape
    tm = min(tm, n)
    n_a, n_d, t = ya2.shape[1], yd_t.shape[1], yd_t.shape[2]
    assert n % tm == 0 and tm % t == 0 and n_a == n_d and wo.shape == (n_a + n_d, d)
    return pl.pallas_call(
        _outproj_kernel,
        grid=(n // tm,),
        in_specs=[
            pl.BlockSpec((tm, d), lambda i: (i, 0)),
            pl.BlockSpec((tm, n_a), lambda i: (i, 0)),
            pl.BlockSpec((tm // t, n_d, t), lambda i: (i, 0, 0)),
            _resident((n_a, d), lambda i: (0, 0)),
            _resident((n_d, d), lambda i: (1, 0)),
        ],
        out_specs=pl.BlockSpec((tm, d), lambda i: (i, 0)),
        out_shape=jax.ShapeDtypeStruct((n, d), F32),
        compiler_params=pltpu.CompilerParams(
            dimension_semantics=("parallel",), vmem_limit_bytes=VMEM_LIMIT_BYTES),
        name="outproj",
    )(x2, ya2, yd_t, wo, wo)


def _ffn_kernel(h_ref, g_ref, wg_ref, wu_ref, wd_ref, o_ref, u_ref):
    f = pl.program_id(1)

    @pl.when(f == 0)
    def _():
        h = h_ref[...]
        u_ref[...] = _rms_rows(h, g_ref[...]).astype(BF16)
        o_ref[...] = h

    u = u_ref[...]
    gate = jnp.dot(u, wg_ref[...], preferred_element_type=F32)
    up = jnp.dot(u, wu_ref[...], preferred_element_type=F32)
    act = (gate * (1.0 / (1.0 + jnp.exp(-gate))) * up).astype(BF16)
    o_ref[...] += jnp.dot(act, wd_ref[...], preferred_element_type=F32)


def _ffn(h2, g_ffn, wg, wu, wd, tm=1024, tf=512):
    n, d = h2.shape
    tm = min(tm, n)
    assert n % tm == 0
    dff = wg.shape[1]
    return pl.pallas_call(
        _ffn_kernel,
        grid=(n // tm, dff // tf),
        in_specs=[
            pl.BlockSpec((tm, d), lambda i, f: (i, 0)),
            _resident((1, d), lambda i, f: (0, 0)),
            pl.BlockSpec((d, tf), lambda i, f: (0, f)),
            pl.BlockSpec((d, tf), lambda i, f: (0, f)),
            pl.BlockSpec((tf, d), lambda i, f: (f, 0)),
        ],
        out_specs=pl.BlockSpec((tm, d), lambda i, f: (i, 0)),
        out_shape=jax.ShapeDtypeStruct((n, d), F32),
        scratch_shapes=[pltpu.VMEM((tm, d), BF16)],
        compiler_params=pltpu.CompilerParams(
            dimension_semantics=("parallel", "arbitrary"), vmem_limit_bytes=VMEM_LIMIT_BYTES),
        name="ffn",
    )(h2, g_ffn[None, :], wg, wu, wd)


def _ple_kernel(h_ref, p_ref, g_ref, wg_ref, wp_ref, go_ref, o_ref, *, sub, ncol):
    d = h_ref.shape[1]
    for st in range(h_ref.shape[0] // sub):
        rows = slice(st * sub, (st + 1) * sub)
        u = _rms_rows(h_ref[rows, :], g_ref[...]).astype(BF16)
        pp = jnp.dot(p_ref[rows, :].astype(BF16), wp_ref[...], preferred_element_type=F32)
        ppn = _rms_rows(pp, go_ref[...])
        for c in range(d // ncol):
            cols = slice(c * ncol, (c + 1) * ncol)
            z = jnp.dot(u, wg_ref[:, cols], preferred_element_type=F32)
            gate = 1.0 / (1.0 + jnp.exp(-z))
            o_ref[rows, cols] = h_ref[rows, cols] + gate * ppn[:, cols]


def _ple(h2, p2, g_ple, wg, wp, g_out, tm=1024, sub=512, ncol=512):
    n, d = h2.shape
    tm = min(tm, n)
    assert n % tm == 0
    sub = min(sub, tm)
    assert tm % sub == 0
    const = lambda shape: _resident(shape, lambda i: (0, 0))
    return pl.pallas_call(
        functools.partial(_ple_kernel, sub=sub, ncol=min(ncol, d)),
        grid=(n // tm,),
        in_specs=[
            pl.BlockSpec((tm, d), lambda i: (i, 0)),
            pl.BlockSpec((tm, p2.shape[1]), lambda i: (i, 0)),
            const((1, d)), const(wg.shape), const(wp.shape), const((1, d)),
        ],
        out_specs=pl.BlockSpec((tm, d), lambda i: (i, 0)),
        out_shape=jax.ShapeDtypeStruct((n, d), F32),
        compiler_params=pltpu.CompilerParams(
            dimension_semantics=("parallel",), vmem_limit_bytes=VMEM_LIMIT_BYTES),
        name="ple",
    )(h2, p2, g_ple[None, :], wg, wp, g_out[None, :])


def _alibi_slopes(n):
    return [2.0 ** (-8.0 * (h + 1) / n) for h in range(n)]


def _logit_bound(q_gain, k_gain):
    return 1.02 * math.sqrt(HEAD_DIM) * jnp.max(jnp.abs(q_gain)) * jnp.max(jnp.abs(k_gain))


def kernel(x, p, g_attn, w_in, qn_swa, kn_swa, sinks, qn_diff, kn_diff, lambda_q1, lambda_k1, lambda_q2,
           lambda_k2, g_sub, w_out, g_ffn, w_gate, w_up, w_down, g_ple, w_ple_gate, w_ple_proj, g_ple_out):
    b, s, d = x.shape
    depth = p.shape[0]
    n_qa = d // 2
    n_ka = n_va = SWA_KV_HEADS * HEAD_DIM
    n_qd = n_kd = n_vd = d // 2
    diff_heads = n_vd // DIFF_V_DIM
    swa_heads = n_qa // HEAD_DIM
    assert swa_heads == SWA_KV_HEADS * SWA_GROUP and s % TOKEN_TILE == 0
    c = np.cumsum([0, n_qa, n_ka, n_va, n_qd, n_kd, n_vd])
    h = x
    for i in range(depth):
        lam_init = 0.8 - 0.6 * math.exp(-0.3 * i)
        wfm = _cast_columns(w_in[i], [(int(c[0]), int(c[6]))], transpose=True, cb=2 * SWA_BLOCK)
        fm_rows = tuple(int(r) for r in c[:6])
        qa_t, va_t, qd_t, vd_t, ka, kd = _inproj(
            h, g_attn[i], wfm, qn_swa[i], kn_swa[i], qn_diff[i], kn_diff[i],
            (n_qa, n_ka, n_va, n_qd, n_kd, n_vd), fm_rows)
        ya, (wg, wu, wd, wo, wpg) = _swa_attention(
            qa_t, ka, va_t, sinks[i], _alibi_slopes(swa_heads), _logit_bound(qn_swa[i], kn_swa[i]),
            f32_weights=(w_gate[i], w_up[i], w_down[i], w_out[i], w_ple_gate[i]))
        lam_params = jnp.stack([lambda_q1[i], lambda_k1[i], lambda_q2[i], lambda_k2[i]]).astype(F32)
        yd = _diff_attention(qd_t, kd, vd_t, lam_params, g_sub[i], _alibi_slopes(diff_heads), lam_init,
                             _logit_bound(qn_diff[i], kn_diff[i]))
        h2 = _outproj(h.reshape(b * s, d), ya.reshape(b * s, n_qa), yd.reshape(-1, n_vd, yd.shape[-1]), wo)
        h2 = _ffn(h2, g_ffn[i], wg, wu, wd)
        h2 = _ple(h2, p[i].reshape(b * s, -1), g_ple[i], wpg, w_ple_proj[i].astype(BF16), g_ple_out[i])
        h = h2.reshape(b, s, d)
    return h
```

```python
import functools
import math

import jax
import jax.numpy as jnp
import numpy as np
from jax import lax
from jax.experimental import pallas as pl
from jax.experimental.pallas import tpu as pltpu

F32 = jnp.float32
BF16 = jnp.bfloat16

HEAD_DIM = 64
SWA_BLOCK = 128
SWA_KV_HEADS = 2
SWA_GROUP = 8
DIFF_V_DIM = 2 * HEAD_DIM
EPS = 1e-6
NEG_INF = float("-inf")
LOG2E = math.log2(math.e)
MAX_UNSHIFTED_LOGIT = 60.0

TOKEN_TILE = 256
INPROJ_TILES_PER_STEP = 4
DIFF_HEADS_PER_STEP = 4
SWA_CAST_SLAB_BYTES = 12 * 1024 * 1024
VMEM_LIMIT_BYTES = 56 * 1024 * 1024

_NT = (((1,), (1,)), ((), ()))
_TN = (((0,), (0,)), ((), ()))


def _resident(shape, index_map):
    return pl.BlockSpec(shape, index_map, pipeline_mode=pl.Buffered(1))


def _rms_rows(x, gain):
    ms = jnp.mean(x * x, axis=-1, keepdims=True)
    return x * lax.rsqrt(ms + EPS) * gain


def _cast_columns_kernel(blk_ref, w_ref, o_ref, *, transpose):
    del blk_ref
    w = w_ref[...]
    o_ref[...] = (w.T if transpose else w).astype(BF16)


def _cast_columns(w, col_ranges, transpose, cb=128):
    k, _ = w.shape
    blocks = [c0 // cb + i for c0, c1 in col_ranges for i in range((c1 - c0) // cb)]
    assert all(c0 % cb == 0 and c1 % cb == 0 for c0, c1 in col_ranges)
    n = len(blocks) * cb
    return pl.pallas_call(
        functools.partial(_cast_columns_kernel, transpose=transpose),
        grid_spec=pltpu.PrefetchScalarGridSpec(
            num_scalar_prefetch=1,
            grid=(len(blocks),),
            in_specs=[pl.BlockSpec((k, cb), lambda i, blk: (0, blk[i]))],
            out_specs=pl.BlockSpec((cb, k), lambda i, blk: (i, 0)) if transpose
            else pl.BlockSpec((k, cb), lambda i, blk: (0, i)),
        ),
        out_shape=jax.ShapeDtypeStruct((n, k) if transpose else (k, n), BF16),
        compiler_params=pltpu.CompilerParams(dimension_semantics=("parallel",)),
        name="weight_layout_t" if transpose else "weight_layout",
    )(jnp.asarray(blocks, jnp.int32), w)


def _inproj_kernel(x_ref, g_ref, wfm_ref, gqa_ref, gqd_ref, gka_ref, gkd_ref,
                   qa_ref, va_ref, qd_ref, vd_ref, ka_ref, kd_ref, *, n_qa, n_va, n_qd, n_vd, n_ka, n_kd, fm_rows):
    tm = TOKEN_TILE
    per_tile = tm // SWA_BLOCK

    def headnorm_fm(z, gain, store):
        for h in range(z.shape[0] // HEAD_DIM):
            zh = z[h * HEAD_DIM:(h + 1) * HEAD_DIM, :]
            ms = jnp.mean(zh * zh, axis=0, keepdims=True)
            store(h, (zh * lax.rsqrt(ms + EPS) * gain).astype(BF16))

    for st in range(x_ref.shape[0] // tm):
        rows = slice(st * tm, (st + 1) * tm)
        u = _rms_rows(x_ref[rows, :], g_ref[...]).astype(BF16)

        def fm(row0, nrows):
            return lax.dot_general(wfm_ref[row0:row0 + nrows, :], u, _NT, preferred_element_type=F32)

        def store_qa(h, val):
            for t in range(per_tile):
                qa_ref[0, st * per_tile + t, h * HEAD_DIM:(h + 1) * HEAD_DIM, :] = (
                    val[:, t * SWA_BLOCK:(t + 1) * SWA_BLOCK])

        def store_qd(h, val):
            qd_ref[0, st, h * HEAD_DIM:(h + 1) * HEAD_DIM, :] = val

        def store_ka(h, val):
            for t in range(per_tile):
                ka_ref[0, st * per_tile + t, h * HEAD_DIM:(h + 1) * HEAD_DIM, :] = (
                    val[:, t * SWA_BLOCK:(t + 1) * SWA_BLOCK])

        def store_kd(h, val):
            kd_ref[0, st, h * HEAD_DIM:(h + 1) * HEAD_DIM, :] = val

        r_qa, r_ka, r_va, r_qd, r_kd, r_vd = fm_rows
        headnorm_fm(fm(r_qa, n_qa), gqa_ref[...], store_qa)
        headnorm_fm(fm(r_ka, n_ka), gka_ref[...], store_ka)
        zva = fm(r_va, n_va).astype(BF16)
        for t in range(per_tile):
            va_ref[0, st * per_tile + t] = zva[:, t * SWA_BLOCK:(t + 1) * SWA_BLOCK]
        headnorm_fm(fm(r_qd, n_qd), gqd_ref[...], store_qd)
        headnorm_fm(fm(r_kd, n_kd), gkd_ref[...], store_kd)
        vd_ref[0, st] = fm(r_vd, n_vd).astype(BF16)


def _inproj(x, g_attn, wfm, qn_swa, kn_swa, qn_diff, kn_diff, dims, fm_rows):
    b, s, d = x.shape
    n_qa, n_ka, n_va, n_qd, n_kd, n_vd = dims
    tm = TOKEN_TILE
    nt = s // tm
    tps = min(INPROJ_TILES_PER_STEP, nt)
    assert nt % tps == 0
    blk = tps * tm
    scale = LOG2E / math.sqrt(HEAD_DIM)
    gqa = jnp.broadcast_to((qn_swa * scale)[:, None], (HEAD_DIM, tm)).astype(F32)
    gqd = jnp.broadcast_to((qn_diff * scale)[:, None], (HEAD_DIM, tm)).astype(F32)
    gka = jnp.broadcast_to(kn_swa[:, None], (HEAD_DIM, tm)).astype(F32)
    gkd = jnp.broadcast_to(kn_diff[:, None], (HEAD_DIM, tm)).astype(F32)
    const = lambda shape: _resident(shape, lambda bi, ti: (0,) * len(shape))
    kern = functools.partial(_inproj_kernel, n_qa=n_qa, n_va=n_va, n_qd=n_qd, n_vd=n_vd, n_ka=n_ka, n_kd=n_kd,
                             fm_rows=fm_rows)
    return pl.pallas_call(
        kern,
        grid=(b, nt // tps),
        in_specs=[
            pl.BlockSpec((None, blk, d), lambda bi, ti: (bi, ti, 0)),
            const((1, d)), const(wfm.shape),
            const((HEAD_DIM, tm)), const((HEAD_DIM, tm)), const((HEAD_DIM, tm)), const((HEAD_DIM, tm)),
        ],
        out_specs=[
            pl.BlockSpec((1, blk // SWA_BLOCK, n_qa, SWA_BLOCK), lambda bi, ti: (bi, ti, 0, 0)),
            pl.BlockSpec((1, blk // SWA_BLOCK, n_va, SWA_BLOCK), lambda bi, ti: (bi, ti, 0, 0)),
            pl.BlockSpec((1, tps, n_qd, tm), lambda bi, ti: (bi, ti, 0, 0)),
            pl.BlockSpec((1, tps, n_vd, tm), lambda bi, ti: (bi, ti, 0, 0)),
            pl.BlockSpec((1, blk // SWA_BLOCK, n_ka, SWA_BLOCK), lambda bi, ti: (bi, ti, 0, 0)),
            pl.BlockSpec((1, tps, n_kd, tm), lambda bi, ti: (bi, ti, 0, 0)),
        ],
        out_shape=[
            jax.ShapeDtypeStruct((b, s // SWA_BLOCK, n_qa, SWA_BLOCK), BF16),
            jax.ShapeDtypeStruct((b, s // SWA_BLOCK, n_va, SWA_BLOCK), BF16),
            jax.ShapeDtypeStruct((b, nt, n_qd, tm), BF16),
            jax.ShapeDtypeStruct((b, nt, n_vd, tm), BF16),
            jax.ShapeDtypeStruct((b, s // SWA_BLOCK, n_ka, SWA_BLOCK), BF16),
            jax.ShapeDtypeStruct((b, nt, n_kd, tm), BF16),
        ],
        compiler_params=pltpu.CompilerParams(
            dimension_semantics=("parallel", "parallel"), vmem_limit_bytes=VMEM_LIMIT_BYTES),
        name="inproj",
    )(x, g_attn[None, :], wfm, gqa, gqd, gka, gkd)


def _staged_pipeline(n_steps, stages):
    depth = len(stages)

    def iteration(i, parity):
        for k in reversed(range(depth)):
            t = i - k
            if isinstance(i, int) and not 0 <= t < n_steps:
                continue
            stages[k](t, (parity - k) % 2)

    fill_end = min(depth - 1, n_steps)
    for i in range(fill_end):
        iteration(i, i % 2)
    n_pairs = (n_steps - fill_end) // 2

    def pair(m, carry):
        i = fill_end + 2 * m
        iteration(i, fill_end % 2)
        iteration(i + 1, (fill_end + 1) % 2)
        return carry

    lax.fori_loop(0, n_pairs, pair, 0)
    for i in range(fill_end + 2 * n_pairs, n_steps + depth - 1):
        iteration(i, i % 2)


def _swa_kernel(bounded_ref, q_ref, k_ref, v_ref, bias_ref, sink_ref, *refs, n_cast):
    cast_in, o_ref, cast_out = refs[:n_cast], refs[n_cast], refs[n_cast + 1:2 * n_cast + 1]
    s_ref, p_ref, d_ref = refs[2 * n_cast + 1:]
    for src, dst in zip(cast_in, cast_out):
        dst[...] = src[...].astype(BF16)

    hk = pl.program_id(0)
    nblk = q_ref.shape[1]
    w = SWA_BLOCK
    sink = sink_ref[0]

    def window_start(n):
        return jnp.maximum(n - 1, 0)

    def scores(n, slot):
        qblk = q_ref[0, n]
        qg = jnp.concatenate([qblk[g * HEAD_DIM:(g + 1) * HEAD_DIM, :] for g in range(SWA_GROUP)], axis=1)
        zero = jnp.zeros_like(qg)
        qpad = jnp.concatenate([jnp.where(hk == 0, qg, zero), jnp.where(hk == 1, qg, zero)], axis=0)
        first = window_start(n)
        kwin_t = jnp.concatenate([k_ref[0, first], k_ref[0, first + 1]], axis=1)
        s_ref[slot] = lax.dot_general(kwin_t, qpad, _TN, preferred_element_type=F32)

    def probs(n, slot, *, bounded):
        t = s_ref[slot] + bias_ref[jnp.minimum(n, 1)]
        if bounded:
            e = jnp.exp2(t)
            d_ref[slot] = jnp.sum(e, axis=0, keepdims=True) + jnp.exp2(sink)
        else:
            m = jnp.maximum(jnp.max(t, axis=0, keepdims=True), sink)
            e = jnp.exp2(t - m)
            d_ref[slot] = jnp.sum(e, axis=0, keepdims=True) + jnp.exp2(sink - m)
        p_ref[slot] = e.astype(BF16)

    def weighted_values(n, slot):
        first = window_start(n)
        vwin = jnp.concatenate([v_ref[0, first], v_ref[0, first + 1]], axis=1)
        o = jnp.dot(vwin, p_ref[slot], preferred_element_type=F32) * (1.0 / d_ref[slot])
        for gp in range(SWA_GROUP // 2):
            pair = jnp.concatenate([o[:, (2 * gp) * w:(2 * gp + 1) * w],
                                    o[:, (2 * gp + 1) * w:(2 * gp + 2) * w]], axis=0)
            o_ref[0, pl.ds(pl.multiple_of(n * w, w), w), gp * 2 * HEAD_DIM:(gp + 1) * 2 * HEAD_DIM] = (
                pair.T.astype(BF16))

    @pl.when(bounded_ref[0] != 0)
    def _():
        _staged_pipeline(nblk, [scores, functools.partial(probs, bounded=True), weighted_values])

    @pl.when(bounded_ref[0] == 0)
    def _():
        def block(n, carry):
            scores(n, 0)
            probs(n, 0, bounded=False)
            weighted_values(n, 0)
            return carry

        lax.fori_loop(0, nblk, block, 0)


def _swa_bias(slopes):
    w = SWA_BLOCK
    kj = np.arange(2 * w)[:, None]
    qi = np.arange(w)[None, :]
    out = np.empty((2, SWA_KV_HEADS, 2 * w, SWA_GROUP * w), np.float32)
    for first, dist in ((0, qi - kj), (1, qi - kj + w)):
        valid = (dist >= 0) & (dist < w)
        for hk in range(SWA_KV_HEADS):
            for g in range(SWA_GROUP):
                sl = slopes[hk * SWA_GROUP + g]
                out[first, hk, :, g * w:(g + 1) * w] = np.where(valid, -sl * dist, -np.inf)
    return out


def _swa_attention(qa_t, ka, va_t, sinks, slopes, logit_bound, f32_weights=()):
    b, nblk, n_qa, w = qa_t.shape
    s = nblk * w
    gw = SWA_GROUP * w
    bias = jnp.asarray(_swa_bias(slopes) * LOG2E)
    sinks = sinks.astype(F32)
    sink_cols = jnp.repeat((sinks * LOG2E).reshape(SWA_KV_HEADS, 1, SWA_GROUP), w, axis=-1)
    bounded = jnp.asarray(jnp.maximum(logit_bound, jnp.max(jnp.abs(sinks))) <= MAX_UNSHIFTED_LOGIT, jnp.int32)
    n_steps = SWA_KV_HEADS * b
    slab_bytes = sum(wt.size * 4 // n_steps for wt in f32_weights)
    if (any(wt.shape[0] % (16 * n_steps) for wt in f32_weights)
            or slab_bytes > SWA_CAST_SLAB_BYTES):
        ya, _ = _swa_attention(qa_t, ka, va_t, sinks, slopes, logit_bound)
        return ya, tuple(wt.astype(BF16) for wt in f32_weights)
    slab_specs = [pl.BlockSpec((wt.shape[0] // n_steps, wt.shape[1]), lambda hk, bi: (hk * b + bi, 0))
                  for wt in f32_weights]
    outs = pl.pallas_call(
        functools.partial(_swa_kernel, n_cast=len(f32_weights)),
        grid=(SWA_KV_HEADS, b),
        in_specs=[
            pl.BlockSpec(memory_space=pltpu.SMEM),
            pl.BlockSpec((1, nblk, SWA_GROUP * HEAD_DIM, w), lambda hk, bi: (bi, 0, hk, 0)),
            pl.BlockSpec((1, nblk, SWA_KV_HEADS * HEAD_DIM, w), lambda hk, bi: (bi, 0, 0, 0)),
            pl.BlockSpec((1, nblk, HEAD_DIM, w), lambda hk, bi: (bi, 0, hk, 0)),
            pl.BlockSpec((2, None, 2 * w, gw), lambda hk, bi: (0, hk, 0, 0)),
            pl.BlockSpec((1, 1, gw), lambda hk, bi: (hk, 0, 0)),
        ] + slab_specs,
        out_specs=[pl.BlockSpec((1, s, SWA_GROUP * HEAD_DIM), lambda hk, bi: (bi, 0, hk))] + slab_specs,
        out_shape=[jax.ShapeDtypeStruct((b, s, n_qa), BF16)]
                  + [jax.ShapeDtypeStruct(wt.shape, BF16) for wt in f32_weights],
        scratch_shapes=[pltpu.VMEM((2, 2 * w, gw), F32), pltpu.VMEM((2, 2 * w, gw), BF16),
                        pltpu.VMEM((2, 1, gw), F32)],
        compiler_params=pltpu.CompilerParams(
            dimension_semantics=("parallel", "parallel"), vmem_limit_bytes=VMEM_LIMIT_BYTES),
        name="swa_attn",
    )(bounded.reshape(1), qa_t, ka, va_t, bias, sink_cols, *f32_weights)
    return outs[0], outs[1:]


def _diff_kernel(bounded_ref, iq_tbl_ref, j_tbl_ref, q_ref, qpos_ref, k_ref, kpos_ref, v_ref, mask_ref, lam_ref,
                 gsub_ref, o_ref, acc_ref, l_ref, p_ref, *, lam_init):
    nt, t_q = q_ref.shape[1], q_ref.shape[3]
    hpb = acc_ref.shape[0]
    n_plain = nt * (nt - 1) // 2
    dv = DIFF_V_DIM
    lp = lam_ref[...]
    lam = (jnp.exp(jnp.sum(lp[0:1] * lp[1:2], axis=-1, keepdims=True))
           - jnp.exp(jnp.sum(lp[2:3] * lp[3:4], axis=-1, keepdims=True)) + lam_init)
    zero = jnp.zeros((HEAD_DIM, t_q), BF16)

    def padded_q(iq, hh):
        q = q_ref[0, iq, hh * dv:(hh + 1) * dv, :]
        return jnp.concatenate([jnp.concatenate([q[:HEAD_DIM], zero], axis=0),
                                jnp.concatenate([zero, q[HEAD_DIM:]], axis=0)], axis=1)

    def logits(iq, j, hh, masked):
        rows = pl.ds(pl.multiple_of(j * t_q, t_q), t_q)
        lhs = jnp.concatenate([k_ref[0, j, hh * dv:(hh + 1) * dv, :].T, kpos_ref[rows, :]], axis=1)
        rhs = jnp.concatenate([padded_q(iq, hh), qpos_ref[iq, hh * dv:(hh + 1) * dv, :]], axis=0)
        s = jnp.dot(lhs, rhs, preferred_element_type=F32)
        return s + mask_ref[...] if masked else s

    def finalize(iq, hh, acc, l):
        acc = acc * (1.0 / l)
        od = acc[:, :t_q] - lam * acc[:, t_q:]
        ms = jnp.mean(od * od, axis=0, keepdims=True)
        y = od * lax.rsqrt(ms + EPS) * gsub_ref[...]
        o_ref[0, iq, hh * dv:(hh + 1) * dv, :] = y.astype(BF16)

    def probs(t, slot, *, base, masked):
        iq, j = iq_tbl_ref[base + t], j_tbl_ref[base + t]
        for hh in range(hpb):
            p = jnp.exp2(logits(iq, j, hh, masked))
            l_ref[hh, iq] += jnp.sum(p.reshape(t_q // 8, 8, 2 * t_q), axis=0)
            p_ref[slot, hh] = p.astype(BF16)

    def pv(t, slot, *, base):
        iq, j = iq_tbl_ref[base + t], j_tbl_ref[base + t]
        for hh in range(hpb):
            acc_ref[hh, iq] += jnp.dot(v_ref[0, j, hh * dv:(hh + 1) * dv, :], p_ref[slot, hh],
                                       preferred_element_type=F32)

    @pl.when(bounded_ref[0] != 0)
    def _():
        acc_ref[...] = jnp.zeros_like(acc_ref)
        l_ref[...] = jnp.zeros_like(l_ref)
        _staged_pipeline(n_plain, [functools.partial(probs, base=0, masked=False),
                                   functools.partial(pv, base=0)])
        _staged_pipeline(nt, [functools.partial(probs, base=n_plain, masked=True),
                              functools.partial(pv, base=n_plain)])

        def fin(iq, carry):
            for hh in range(hpb):
                finalize(iq, hh, acc_ref[hh, iq], jnp.sum(l_ref[hh, iq], axis=0, keepdims=True))
            return carry

        lax.fori_loop(0, nt, fin, 0)

    @pl.when(bounded_ref[0] == 0)
    def _():
        def q_block(iq, carry):
            acc_ref[:, 0] = jnp.zeros((hpb, dv, 2 * t_q), F32)

            def step(j, stats, masked):
                out = []
                for hh in range(hpb):
                    m, l = stats[2 * hh], stats[2 * hh + 1]
                    s = logits(iq, j, hh, masked)
                    mnew = jnp.maximum(m, jnp.max(s, axis=0, keepdims=True))
                    alpha = jnp.exp2(m - mnew)
                    p = jnp.exp2(s - mnew)
                    out += [mnew, alpha * l + jnp.sum(p, axis=0, keepdims=True)]
                    acc_ref[hh, 0] = alpha * acc_ref[hh, 0] + jnp.dot(
                        v_ref[0, j, hh * dv:(hh + 1) * dv, :], p.astype(BF16), preferred_element_type=F32)
                return tuple(out)

            init = (jnp.full((1, 2 * t_q), NEG_INF, F32), jnp.zeros((1, 2 * t_q), F32)) * hpb
            stats = lax.fori_loop(0, iq, lambda j, c: step(j, c, False), init)
            stats = step(iq, stats, True)
            for hh in range(hpb):
                finalize(iq, hh, acc_ref[hh, 0], stats[2 * hh + 1])
            return carry

        lax.fori_loop(0, nt, q_block, 0)


def _split_bf16(x, parts=3):
    out, rest = [], np.asarray(x, np.float64)
    for _ in range(parts):
        piece = rest.astype(BF16).astype(np.float64)
        out.append(piece)
        rest = rest - piece
    return out


def _alibi_features(slopes, s, t):
    ROW = DIFF_V_DIM
    RADIX = 128
    assert s <= RADIX * 256
    kpos = np.arange(s)
    kfeat = np.zeros((s, ROW), np.float64)
    qfeat = np.zeros((s // t, len(slopes) * ROW, 2 * t), np.float64)
    qpos = (np.arange(s // t)[:, None] * t + np.arange(2 * t)[None, :] % t).astype(np.float64)
    for h, slope in enumerate(slopes):
        for i, piece in enumerate(_split_bf16(slope * LOG2E)):
            c = 5 * i
            kfeat[:, c], kfeat[:, c + 1], kfeat[:, c + 2:c + 5] = kpos // RADIX, kpos % RADIX, 1.0
            r = h * ROW + c
            qfeat[:, r], qfeat[:, r + 1] = float(piece) * RADIX, float(piece)
            for n, part in enumerate(_split_bf16(float(piece) * qpos)):
                qfeat[:, r + 2 + n] = -part
    return jnp.asarray(kfeat, BF16), jnp.asarray(qfeat, BF16)


def _diff_attention(qd_t, kd, vd_t, lam_params, g_sub, slopes, lam_init, logit_bound, hpb=DIFF_HEADS_PER_STEP):
    b, nt, n_qd, t = qd_t.shape
    s = nt * t
    nh = n_qd // DIFF_V_DIM
    kfeat, qfeat = _alibi_features(slopes, s, t)
    causal = np.arange(t)[:, None] <= np.arange(2 * t)[None, :] % t
    mask = jnp.asarray(np.where(causal, 0.0, -np.inf), F32)
    gsub = jnp.broadcast_to((g_sub * (1.0 - lam_init))[:, None], (DIFF_V_DIM, t)).astype(F32)
    bounded = jnp.asarray(logit_bound <= MAX_UNSHIFTED_LOGIT, jnp.int32)
    pairs = [(iq, j) for iq in range(nt) for j in range(iq)] + [(iq, iq) for iq in range(nt)]
    iq_tbl = jnp.asarray([pq[0] for pq in pairs], jnp.int32)
    j_tbl = jnp.asarray([pq[1] for pq in pairs], jnp.int32)
    kern = functools.partial(_diff_kernel, lam_init=lam_init)
    return pl.pallas_call(
        kern,
        grid=(nh // hpb, b),
        in_specs=[
            pl.BlockSpec(memory_space=pltpu.SMEM),
            pl.BlockSpec(memory_space=pltpu.SMEM),
            pl.BlockSpec(memory_space=pltpu.SMEM),
            pl.BlockSpec((1, nt, hpb * DIFF_V_DIM, t), lambda hg, bi: (bi, 0, hg, 0)),
            pl.BlockSpec((nt, hpb * DIFF_V_DIM, 2 * t), lambda hg, bi: (0, hg, 0)),
            pl.BlockSpec((1, nt, hpb * DIFF_V_DIM, t), lambda hg, bi: (bi, 0, hg, 0)),
            pl.BlockSpec((s, DIFF_V_DIM), lambda hg, bi: (0, 0)),
            pl.BlockSpec((1, nt, hpb * DIFF_V_DIM, t), lambda hg, bi: (bi, 0, hg, 0)),
            pl.BlockSpec((t, 2 * t), lambda hg, bi: (0, 0)),
            pl.BlockSpec((4, HEAD_DIM), lambda hg, bi: (0, 0)),
            pl.BlockSpec((DIFF_V_DIM, t), lambda hg, bi: (0, 0)),
        ],
        out_specs=pl.BlockSpec((1, nt, hpb * DIFF_V_DIM, t), lambda hg, bi: (bi, 0, hg, 0)),
        out_shape=jax.ShapeDtypeStruct((b, nt, n_qd, t), BF16),
        scratch_shapes=[pltpu.VMEM((hpb, nt, DIFF_V_DIM, 2 * t), F32),
                        pltpu.VMEM((hpb, nt, 8, 2 * t), F32),
                        pltpu.VMEM((2, hpb, t, 2 * t), BF16)],
        compiler_params=pltpu.CompilerParams(
            dimension_semantics=("parallel", "parallel"), vmem_limit_bytes=VMEM_LIMIT_BYTES),
        name="diff_attn",
    )(bounded.reshape(1), iq_tbl, j_tbl, qd_t, qfeat, kd, kfeat, vd_t, mask, lam_params, gsub)


def _outproj_kernel(x_ref, ya_ref, yd_ref, wa_ref, wd_ref, o_ref):
    yd_t = jnp.concatenate([yd_ref[k] for k in range(yd_ref.shape[0])], axis=1)
    o_ref[...] = (x_ref[...]
                  + jnp.dot(ya_ref[...], wa_ref[...], preferred_element_type=F32)
                  + lax.dot_general(yd_t, wd_ref[...], (((0,), (0,)), ((), ())), preferred_element_type=F32))


def _outproj(x2, ya2, yd_t, wo, tm=512):
    n, d = x2.shape
    tm = min(tm, n)
    n_a, n_d, t = ya2.shape[1], yd_t.shape[1], yd_t.shape[2]
    assert n % tm == 0 and tm % t == 0 and n_a == n_d and wo.shape == (n_a + n_d, d)
    return pl.pallas_call(
        _outproj_kernel,
        grid=(n // tm,),
        in_specs=[
            pl.BlockSpec((tm, d), lambda i: (i, 0)),
            pl.BlockSpec((tm, n_a), lambda i: (i, 0)),
            pl.BlockSpec((tm // t, n_d, t), lambda i: (i, 0, 0)),
            _resident((n_a, d), lambda i: (0, 0)),
            _resident((n_d, d), lambda i: (1, 0)),
        ],
        out_specs=pl.BlockSpec((tm, d), lambda i: (i, 0)),
        out_shape=jax.ShapeDtypeStruct((n, d), F32),
        compiler_params=pltpu.CompilerParams(
            dimension_semantics=("parallel",), vmem_limit_bytes=VMEM_LIMIT_BYTES),
        name="outproj",
    )(x2, ya2, yd_t, wo, wo)


def _ffn_kernel(h_ref, g_ref, wg_ref, wu_ref, wd_ref, o_ref, u_ref):
    f = pl.program_id(1)

    @pl.when(f == 0)
    def _():
        h = h_ref[...]
        u_ref[...] = _rms_rows(h, g_ref[...]).astype(BF16)
        o_ref[...] = h

    u = u_ref[...]
    gate = jnp.dot(u, wg_ref[...], preferred_element_type=F32)
    up = jnp.dot(u, wu_ref[...], preferred_element_type=F32)
    act = (gate * (1.0 / (1.0 + jnp.exp(-gate))) * up).astype(BF16)
    o_ref[...] += jnp.dot(act, wd_ref[...], preferred_element_type=F32)


def _ffn(h2, g_ffn, wg, wu, wd, tm=1024, tf=512):
    n, d = h2.shape
    tm = min(tm, n)
    assert n % tm == 0
    dff = wg.shape[1]
    return pl.pallas_call(
        _ffn_kernel,
        grid=(n // tm, dff // tf),
        in_specs=[
            pl.BlockSpec((tm, d), lambda i, f: (i, 0)),
            _resident((1, d), lambda i, f: (0, 0)),
            pl.BlockSpec((d, tf), lambda i, f: (0, f)),
            pl.BlockSpec((d, tf), lambda i, f: (0, f)),
            pl.BlockSpec((tf, d), lambda i, f: (f, 0)),
        ],
        out_specs=pl.BlockSpec((tm, d), lambda i, f: (i, 0)),
        out_shape=jax.ShapeDtypeStruct((n, d), F32),
        scratch_shapes=[pltpu.VMEM((tm, d), BF16)],
        compiler_params=pltpu.CompilerParams(
            dimension_semantics=("parallel", "arbitrary"), vmem_limit_bytes=VMEM_LIMIT_BYTES),
        name="ffn",
    )(h2, g_ffn[None, :], wg, wu, wd)


def _ple_kernel(h_ref, p_ref, g_ref, wg_ref, wp_ref, go_ref, o_ref, *, sub, ncol):
    d = h_ref.shape[1]
    for st in range(h_ref.shape[0] // sub):
        rows = slice(st * sub, (st + 1) * sub)
        u = _rms_rows(h_ref[rows, :], g_ref[...]).astype(BF16)
        pp = jnp.dot(p_ref[rows, :].astype(BF16), wp_ref[...], preferred_element_type=F32)
        ppn = _rms_rows(pp, go_ref[...])
        for c in range(d // ncol):
            cols = slice(c * ncol, (c + 1) * ncol)
            z = jnp.dot(u, wg_ref[:, cols], preferred_element_type=F32)
            gate = 1.0 / (1.0 + jnp.exp(-z))
            o_ref[rows, cols] = h_ref[rows, cols] + gate * ppn[:, cols]


def _ple(h2, p2, g_ple, wg, wp, g_out, tm=1024, sub=512, ncol=512):
    n, d = h2.shape
    tm = min(tm, n)
    assert n % tm == 0
    sub = min(sub, tm)
    assert tm % sub == 0
    const = lambda shape: _resident(shape, lambda i: (0, 0))
    return pl.pallas_call(
        functools.partial(_ple_kernel, sub=sub, ncol=min(ncol, d)),
        grid=(n // tm,),
        in_specs=[
            pl.BlockSpec((tm, d), lambda i: (i, 0)),
            pl.BlockSpec((tm, p2.shape[1]), lambda i: (i, 0)),
            const((1, d)), const(wg.shape), const(wp.shape), const((1, d)),
        ],
        out_specs=pl.BlockSpec((tm, d), lambda i: (i, 0)),
        out_shape=jax.ShapeDtypeStruct((n, d), F32),
        compiler_params=pltpu.CompilerParams(
            dimension_semantics=("parallel",), vmem_limit_bytes=VMEM_LIMIT_BYTES),
        name="ple",
    )(h2, p2, g_ple[None, :], wg, wp, g_out[None, :])


def _alibi_slopes(n):
    return [2.0 ** (-8.0 * (h + 1) / n) for h in range(n)]


def _logit_bound(q_gain, k_gain):
    return 1.02 * math.sqrt(HEAD_DIM) * jnp.max(jnp.abs(q_gain)) * jnp.max(jnp.abs(k_gain))


def kernel(x, p, g_attn, w_in, qn_swa, kn_swa, sinks, qn_diff, kn_diff, lambda_q1, lambda_k1, lambda_q2,
           lambda_k2, g_sub, w_out, g_ffn, w_gate, w_up, w_down, g_ple, w_ple_gate, w_ple_proj, g_ple_out):
    b, s, d = x.shape
    depth = p.shape[0]
    n_qa = d // 2
    n_ka = n_va = SWA_KV_HEADS * HEAD_DIM
    n_qd = n_kd = n_vd = d // 2
    diff_heads = n_vd // DIFF_V_DIM
    swa_heads = n_qa // HEAD_DIM
    assert swa_heads == SWA_KV_HEADS * SWA_GROUP and s % TOKEN_TILE == 0
    c = np.cumsum([0, n_qa, n_ka, n_va, n_qd, n_kd, n_vd])
    h = x
    for i in range(depth):
        lam_init = 0.8 - 0.6 * math.exp(-0.3 * i)
        wfm = _cast_columns(w_in[i], [(int(c[0]), int(c[6]))], transpose=True, cb=2 * SWA_BLOCK)
        fm_rows = tuple(int(r) for r in c[:6])
        qa_t, va_t, qd_t, vd_t, ka, kd = _inproj(
            h, g_attn[i], wfm, qn_swa[i], kn_swa[i], qn_diff[i], kn_diff[i],
            (n_qa, n_ka, n_va, n_qd, n_kd, n_vd), fm_rows)
        ya, (wg, wu, wd, wo, wpg) = _swa_attention(
            qa_t, ka, va_t, sinks[i], _alibi_slopes(swa_heads), _logit_bound(qn_swa[i], kn_swa[i]),
            f32_weights=(w_gate[i], w_up[i], w_down[i], w_out[i], w_ple_gate[i]))
        lam_params = jnp.stack([lambda_q1[i], lambda_k1[i], lambda_q2[i], lambda_k2[i]]).astype(F32)
        yd = _diff_attention(qd_t, kd, vd_t, lam_params, g_sub[i], _alibi_slopes(diff_heads), lam_init,
                             _logit_bound(qn_diff[i], kn_diff[i]))
        h2 = _outproj(h.reshape(b * s, d), ya.reshape(b * s, n_qa), yd.reshape(-1, n_vd, yd.shape[-1]), wo)
        h2 = _ffn(h2, g_ffn[i], wg, wu, wd)
        h2 = _ple(h2, p[i].reshape(b * s, -1), g_ple[i], wpg, w_ple_proj[i].astype(BF16), g_ple_out[i])
        h = h2.reshape(b, s, d)
    return h
```

```python
import functools
import math

import jax
import jax.numpy as jnp
import numpy as np
from jax import lax
from jax.experimental import pallas as pl
from jax.experimental.pallas import tpu as pltpu

F32 = jnp.float32
BF16 = jnp.bfloat16

HEAD_DIM = 64
SWA_BLOCK = 128
SWA_KV_HEADS = 2
SWA_GROUP = 8
DIFF_V_DIM = 2 * HEAD_DIM
EPS = 1e-6
NEG_INF = float("-inf")
LOG2E = math.log2(math.e)
MAX_UNSHIFTED_LOGIT = 60.0

TOKEN_TILE = 256
INPROJ_TILES_PER_STEP = 4
DIFF_HEADS_PER_STEP = 4
SWA_CAST_SLAB_BYTES = 12 * 1024 * 1024
VMEM_LIMIT_BYTES = 56 * 1024 * 1024

_NT = (((1,), (1,)), ((), ()))
_TN = (((0,), (0,)), ((), ()))


def _resident(shape, index_map):
    return pl.BlockSpec(shape, index_map, pipeline_mode=pl.Buffered(1))


def _rms_rows(x, gain):
    ms = jnp.mean(x * x, axis=-1, keepdims=True)
    return x * lax.rsqrt(ms + EPS) * gain


def _transpose_cast_kernel(w_ref, o_ref):
    o_ref[...] = w_ref[...].T.astype(BF16)


def _transpose_cast(w, cb=2 * SWA_BLOCK):
    k, n = w.shape
    assert n % cb == 0
    return pl.pallas_call(
        _transpose_cast_kernel,
        grid=(n // cb,),
        in_specs=[pl.BlockSpec((k, cb), lambda i: (0, i))],
        out_specs=pl.BlockSpec((cb, k), lambda i: (i, 0)),
        out_shape=jax.ShapeDtypeStruct((n, k), BF16),
        compiler_params=pltpu.CompilerParams(dimension_semantics=("parallel",)),
        name="weight_layout_t",
    )(w)


def _inproj_kernel(x_ref, g_ref, wfm_ref, gqa_ref, gqd_ref, gka_ref, gkd_ref,
                   qa_ref, va_ref, qd_ref, vd_ref, ka_ref, kd_ref, *, n_qa, n_va, n_qd, n_vd, n_ka, n_kd, fm_rows):
    tm = TOKEN_TILE
    per_tile = tm // SWA_BLOCK

    def headnorm_fm(z, gain, store):
        for h in range(z.shape[0] // HEAD_DIM):
            zh = z[h * HEAD_DIM:(h + 1) * HEAD_DIM, :]
            ms = jnp.mean(zh * zh, axis=0, keepdims=True)
            store(h, (zh * lax.rsqrt(ms + EPS) * gain).astype(BF16))

    for st in range(x_ref.shape[0] // tm):
        rows = slice(st * tm, (st + 1) * tm)
        u = _rms_rows(x_ref[rows, :], g_ref[...]).astype(BF16)

        def fm(row0, nrows):
            return lax.dot_general(wfm_ref[row0:row0 + nrows, :], u, _NT, preferred_element_type=F32)

        def store_qa(h, val):
            for t in range(per_tile):
                qa_ref[0, st * per_tile + t, h * HEAD_DIM:(h + 1) * HEAD_DIM, :] = (
                    val[:, t * SWA_BLOCK:(t + 1) * SWA_BLOCK])

        def store_qd(h, val):
            qd_ref[0, st, h * HEAD_DIM:(h + 1) * HEAD_DIM, :] = val

        def store_ka(h, val):
            for t in range(per_tile):
                ka_ref[0, st * per_tile + t, h * HEAD_DIM:(h + 1) * HEAD_DIM, :] = (
                    val[:, t * SWA_BLOCK:(t + 1) * SWA_BLOCK])

        def store_kd(h, val):
            kd_ref[0, st, h * HEAD_DIM:(h + 1) * HEAD_DIM, :] = val

        r_qa, r_ka, r_va, r_qd, r_kd, r_vd = fm_rows
        headnorm_fm(fm(r_qa, n_qa), gqa_ref[...], store_qa)
        headnorm_fm(fm(r_ka, n_ka), gka_ref[...], store_ka)
        zva = fm(r_va, n_va).astype(BF16)
        for t in range(per_tile):
            va_ref[0, st * per_tile + t] = zva[:, t * SWA_BLOCK:(t + 1) * SWA_BLOCK]
        headnorm_fm(fm(r_qd, n_qd), gqd_ref[...], store_qd)
        headnorm_fm(fm(r_kd, n_kd), gkd_ref[...], store_kd)
        vd_ref[0, st] = fm(r_vd, n_vd).astype(BF16)


def _inproj(x, g_attn, wfm, qn_swa, kn_swa, qn_diff, kn_diff, dims, fm_rows):
    b, s, d = x.shape
    n_qa, n_ka, n_va, n_qd, n_kd, n_vd = dims
    tm = TOKEN_TILE
    nt = s // tm
    tps = min(INPROJ_TILES_PER_STEP, nt)
    assert nt % tps == 0
    blk = tps * tm
    scale = LOG2E / math.sqrt(HEAD_DIM)
    gqa = jnp.broadcast_to((qn_swa * scale)[:, None], (HEAD_DIM, tm)).astype(F32)
    gqd = jnp.broadcast_to((qn_diff * scale)[:, None], (HEAD_DIM, tm)).astype(F32)
    gka = jnp.broadcast_to(kn_swa[:, None], (HEAD_DIM, tm)).astype(F32)
    gkd = jnp.broadcast_to(kn_diff[:, None], (HEAD_DIM, tm)).astype(F32)
    const = lambda shape: _resident(shape, lambda bi, ti: (0,) * len(shape))
    kern = functools.partial(_inproj_kernel, n_qa=n_qa, n_va=n_va, n_qd=n_qd, n_vd=n_vd, n_ka=n_ka, n_kd=n_kd,
                             fm_rows=fm_rows)
    return pl.pallas_call(
        kern,
        grid=(b, nt // tps),
        in_specs=[
            pl.BlockSpec((None, blk, d), lambda bi, ti: (bi, ti, 0)),
            const((1, d)), const(wfm.shape),
            const((HEAD_DIM, tm)), const((HEAD_DIM, tm)), const((HEAD_DIM, tm)), const((HEAD_DIM, tm)),
        ],
        out_specs=[
            pl.BlockSpec((1, blk // SWA_BLOCK, n_qa, SWA_BLOCK), lambda bi, ti: (bi, ti, 0, 0)),
            pl.BlockSpec((1, blk // SWA_BLOCK, n_va, SWA_BLOCK), lambda bi, ti: (bi, ti, 0, 0)),
            pl.BlockSpec((1, tps, n_qd, tm), lambda bi, ti: (bi, ti, 0, 0)),
            pl.BlockSpec((1, tps, n_vd, tm), lambda bi, ti: (bi, ti, 0, 0)),
            pl.BlockSpec((1, blk // SWA_BLOCK, n_ka, SWA_BLOCK), lambda bi, ti: (bi, ti, 0, 0)),
            pl.BlockSpec((1, tps, n_kd, tm), lambda bi, ti: (bi, ti, 0, 0)),
        ],
        out_shape=[
            jax.ShapeDtypeStruct((b, s // SWA_BLOCK, n_qa, SWA_BLOCK), BF16),
            jax.ShapeDtypeStruct((b, s // SWA_BLOCK, n_va, SWA_BLOCK), BF16),
            jax.ShapeDtypeStruct((b, nt, n_qd, tm), BF16),
            jax.ShapeDtypeStruct((b, nt, n_vd, tm), BF16),
            jax.ShapeDtypeStruct((b, s // SWA_BLOCK, n_ka, SWA_BLOCK), BF16),
            jax.ShapeDtypeStruct((b, nt, n_kd, tm), BF16),
        ],
        compiler_params=pltpu.CompilerParams(
            dimension_semantics=("parallel", "parallel"), vmem_limit_bytes=VMEM_LIMIT_BYTES),
        name="inproj",
    )(x, g_attn[None, :], wfm, gqa, gqd, gka, gkd)


def _staged_pipeline(n_steps, stages):
    depth = len(stages)

    def iteration(i, parity):
        for k in reversed(range(depth)):
            t = i - k
            if isinstance(i, int) and not 0 <= t < n_steps:
                continue
            stages[k](t, (parity - k) % 2)

    fill_end = min(depth - 1, n_steps)
    for i in range(fill_end):
        iteration(i, i % 2)
    n_pairs = (n_steps - fill_end) // 2

    def pair(m, carry):
        i = fill_end + 2 * m
        iteration(i, fill_end % 2)
        iteration(i + 1, (fill_end + 1) % 2)
        return carry

    lax.fori_loop(0, n_pairs, pair, 0)
    for i in range(fill_end + 2 * n_pairs, n_steps + depth - 1):
        iteration(i, i % 2)


def _swa_kernel(bounded_ref, q_ref, k_ref, v_ref, bias_ref, sink_ref, *refs, n_cast):
    cast_in, o_ref, cast_out = refs[:n_cast], refs[n_cast], refs[n_cast + 1:2 * n_cast + 1]
    s_ref, p_ref, d_ref = refs[2 * n_cast + 1:]
    for src, dst in zip(cast_in, cast_out):
        dst[...] = src[...].astype(BF16)

    hk = pl.program_id(0)
    nblk = q_ref.shape[1]
    w = SWA_BLOCK
    sink = sink_ref[0]

    def window_start(n):
        return jnp.maximum(n - 1, 0)

    def scores(n, slot):
        qblk = q_ref[0, n]
        qg = jnp.concatenate([qblk[g * HEAD_DIM:(g + 1) * HEAD_DIM, :] for g in range(SWA_GROUP)], axis=1)
        zero = jnp.zeros_like(qg)
        qpad = jnp.concatenate([jnp.where(hk == 0, qg, zero), jnp.where(hk == 1, qg, zero)], axis=0)
        first = window_start(n)
        kwin_t = jnp.concatenate([k_ref[0, first], k_ref[0, first + 1]], axis=1)
        s_ref[slot] = lax.dot_general(kwin_t, qpad, _TN, preferred_element_type=F32)

    def probs(n, slot, *, bounded):
        t = s_ref[slot] + bias_ref[jnp.minimum(n, 1)]
        if bounded:
            e = jnp.exp2(t)
            d_ref[slot] = jnp.sum(e, axis=0, keepdims=True) + jnp.exp2(sink)
        else:
            m = jnp.maximum(jnp.max(t, axis=0, keepdims=True), sink)
            e = jnp.exp2(t - m)
            d_ref[slot] = jnp.sum(e, axis=0, keepdims=True) + jnp.exp2(sink - m)
        p_ref[slot] = e.astype(BF16)

    def weighted_values(n, slot):
        first = window_start(n)
        vwin = jnp.concatenate([v_ref[0, first], v_ref[0, first + 1]], axis=1)
        o = jnp.dot(vwin, p_ref[slot], preferred_element_type=F32) * (1.0 / d_ref[slot])
        for gp in range(SWA_GROUP // 2):
            pair = jnp.concatenate([o[:, (2 * gp) * w:(2 * gp + 1) * w],
                                    o[:, (2 * gp + 1) * w:(2 * gp + 2) * w]], axis=0)
            o_ref[0, pl.ds(pl.multiple_of(n * w, w), w), gp * 2 * HEAD_DIM:(gp + 1) * 2 * HEAD_DIM] = (
                pair.T.astype(BF16))

    @pl.when(bounded_ref[0] != 0)
    def _():
        _staged_pipeline(nblk, [scores, functools.partial(probs, bounded=True), weighted_values])

    @pl.when(bounded_ref[0] == 0)
    def _():
        def block(n, carry):
            scores(n, 0)
            probs(n, 0, bounded=False)
            weighted_values(n, 0)
            return carry

        lax.fori_loop(0, nblk, block, 0)


def _swa_bias(slopes):
    w = SWA_BLOCK
    kj = np.arange(2 * w)[:, None]
    qi = np.arange(w)[None, :]
    out = np.empty((2, SWA_KV_HEADS, 2 * w, SWA_GROUP * w), np.float32)
    for first, dist in ((0, qi - kj), (1, qi - kj + w)):
        valid = (dist >= 0) & (dist < w)
        for hk in range(SWA_KV_HEADS):
            for g in range(SWA_GROUP):
                sl = slopes[hk * SWA_GROUP + g]
                out[first, hk, :, g * w:(g + 1) * w] = np.where(valid, -sl * dist, -np.inf)
    return out


def _swa_attention(qa_t, ka, va_t, sinks, slopes, logit_bound, f32_weights=()):
    b, nblk, n_qa, w = qa_t.shape
    s = nblk * w
    gw = SWA_GROUP * w
    bias = jnp.asarray(_swa_bias(slopes) * LOG2E)
    sinks = sinks.astype(F32)
    sink_cols = jnp.repeat((sinks * LOG2E).reshape(SWA_KV_HEADS, 1, SWA_GROUP), w, axis=-1)
    bounded = jnp.asarray(jnp.maximum(logit_bound, jnp.max(jnp.abs(sinks))) <= MAX_UNSHIFTED_LOGIT, jnp.int32)
    n_steps = SWA_KV_HEADS * b
    slab_bytes = sum(wt.size * 4 // n_steps for wt in f32_weights)
    if (any(wt.shape[0] % (16 * n_steps) for wt in f32_weights)
            or slab_bytes > SWA_CAST_SLAB_BYTES):
        ya, _ = _swa_attention(qa_t, ka, va_t, sinks, slopes, logit_bound)
        return ya, tuple(wt.astype(BF16) for wt in f32_weights)
    slab_specs = [pl.BlockSpec((wt.shape[0] // n_steps, wt.shape[1]), lambda hk, bi: (hk * b + bi, 0))
                  for wt in f32_weights]
    outs = pl.pallas_call(
        functools.partial(_swa_kernel, n_cast=len(f32_weights)),
        grid=(SWA_KV_HEADS, b),
        in_specs=[
            pl.BlockSpec(memory_space=pltpu.SMEM),
            pl.BlockSpec((1, nblk, SWA_GROUP * HEAD_DIM, w), lambda hk, bi: (bi, 0, hk, 0)),
            pl.BlockSpec((1, nblk, SWA_KV_HEADS * HEAD_DIM, w), lambda hk, bi: (bi, 0, 0, 0)),
            pl.BlockSpec((1, nblk, HEAD_DIM, w), lambda hk, bi: (bi, 0, hk, 0)),
            pl.BlockSpec((2, None, 2 * w, gw), lambda hk, bi: (0, hk, 0, 0)),
            pl.BlockSpec((1, 1, gw), lambda hk, bi: (hk, 0, 0)),
        ] + slab_specs,
        out_specs=[pl.BlockSpec((1, s, SWA_GROUP * HEAD_DIM), lambda hk, bi: (bi, 0, hk))] + slab_specs,
        out_shape=[jax.ShapeDtypeStruct((b, s, n_qa), BF16)]
                  + [jax.ShapeDtypeStruct(wt.shape, BF16) for wt in f32_weights],
        scratch_shapes=[pltpu.VMEM((2, 2 * w, gw), F32), pltpu.VMEM((2, 2 * w, gw), BF16),
                        pltpu.VMEM((2, 1, gw), F32)],
        compiler_params=pltpu.CompilerParams(
            dimension_semantics=("parallel", "parallel"), vmem_limit_bytes=VMEM_LIMIT_BYTES),
        name="swa_attn",
    )(bounded.reshape(1), qa_t, ka, va_t, bias, sink_cols, *f32_weights)
    return outs[0], outs[1:]


def _diff_kernel(bounded_ref, iq_tbl_ref, j_tbl_ref, q_ref, qpos_ref, k_ref, kpos_ref, v_ref, mask_ref, lam_ref,
                 gsub_ref, o_ref, acc_ref, l_ref, p_ref, *, lam_init):
    nt, t_q = q_ref.shape[1], q_ref.shape[3]
    hpb = acc_ref.shape[0]
    n_plain = nt * (nt - 1) // 2
    dv = DIFF_V_DIM
    lp = lam_ref[...]
    lam = (jnp.exp(jnp.sum(lp[0:1] * lp[1:2], axis=-1, keepdims=True))
           - jnp.exp(jnp.sum(lp[2:3] * lp[3:4], axis=-1, keepdims=True)) + lam_init)
    zero = jnp.zeros((HEAD_DIM, t_q), BF16)

    def padded_q(iq, hh):
        q = q_ref[0, iq, hh * dv:(hh + 1) * dv, :]
        return jnp.concatenate([jnp.concatenate([q[:HEAD_DIM], zero], axis=0),
                                jnp.concatenate([zero, q[HEAD_DIM:]], axis=0)], axis=1)

    def logits(iq, j, hh, masked):
        rows = pl.ds(pl.multiple_of(j * t_q, t_q), t_q)
        lhs = jnp.concatenate([k_ref[0, j, hh * dv:(hh + 1) * dv, :].T, kpos_ref[rows, :]], axis=1)
        rhs = jnp.concatenate([padded_q(iq, hh), qpos_ref[iq, hh * dv:(hh + 1) * dv, :]], axis=0)
        s = jnp.dot(lhs, rhs, preferred_element_type=F32)
        return s + mask_ref[...] if masked else s

    def finalize(iq, hh, acc, l):
        acc = acc * (1.0 / l)
        od = acc[:, :t_q] - lam * acc[:, t_q:]
        ms = jnp.mean(od * od, axis=0, keepdims=True)
        y = od * lax.rsqrt(ms + EPS) * gsub_ref[...]
        o_ref[0, iq, hh * dv:(hh + 1) * dv, :] = y.astype(BF16)

    def probs(t, slot, *, base, masked):
        iq, j = iq_tbl_ref[base + t], j_tbl_ref[base + t]
        for hh in range(hpb):
            p = jnp.exp2(logits(iq, j, hh, masked))
            l_ref[hh, iq] += jnp.sum(p.reshape(t_q // 8, 8, 2 * t_q), axis=0)
            p_ref[slot, hh] = p.astype(BF16)

    def pv(t, slot, *, base):
        iq, j = iq_tbl_ref[base + t], j_tbl_ref[base + t]
        for hh in range(hpb):
            acc_ref[hh, iq] += jnp.dot(v_ref[0, j, hh * dv:(hh + 1) * dv, :], p_ref[slot, hh],
                                       preferred_element_type=F32)

    @pl.when(bounded_ref[0] != 0)
    def _():
        acc_ref[...] = jnp.zeros_like(acc_ref)
        l_ref[...] = jnp.zeros_like(l_ref)
        _staged_pipeline(n_plain, [functools.partial(probs, base=0, masked=False),
                                   functools.partial(pv, base=0)])
        _staged_pipeline(nt, [functools.partial(probs, base=n_plain, masked=True),
                              functools.partial(pv, base=n_plain)])

        def fin(iq, carry):
            for hh in range(hpb):
                finalize(iq, hh, acc_ref[hh, iq], jnp.sum(l_ref[hh, iq], axis=0, keepdims=True))
            return carry

        lax.fori_loop(0, nt, fin, 0)

    @pl.when(bounded_ref[0] == 0)
    def _():
        def q_block(iq, carry):
            acc_ref[:, 0] = jnp.zeros((hpb, dv, 2 * t_q), F32)

            def step(j, stats, masked):
                out = []
                for hh in range(hpb):
                    m, l = stats[2 * hh], stats[2 * hh + 1]
                    s = logits(iq, j, hh, masked)
                    mnew = jnp.maximum(m, jnp.max(s, axis=0, keepdims=True))
                    alpha = jnp.exp2(m - mnew)
                    p = jnp.exp2(s - mnew)
                    out += [mnew, alpha * l + jnp.sum(p, axis=0, keepdims=True)]
                    acc_ref[hh, 0] = alpha * acc_ref[hh, 0] + jnp.dot(
                        v_ref[0, j, hh * dv:(hh + 1) * dv, :], p.astype(BF16), preferred_element_type=F32)
                return tuple(out)

            init = (jnp.full((1, 2 * t_q), NEG_INF, F32), jnp.zeros((1, 2 * t_q), F32)) * hpb
            stats = lax.fori_loop(0, iq, lambda j, c: step(j, c, False), init)
            stats = step(iq, stats, True)
            for hh in range(hpb):
                finalize(iq, hh, acc_ref[hh, 0], stats[2 * hh + 1])
            return carry

        lax.fori_loop(0, nt, q_block, 0)


def _split_bf16(x, parts=3):
    out, rest = [], np.asarray(x, np.float64)
    for _ in range(parts):
        piece = rest.astype(BF16).astype(np.float64)
        out.append(piece)
        rest = rest - piece
    return out


def _alibi_features(slopes, s, t):
    ROW = DIFF_V_DIM
    RADIX = 128
    assert s <= RADIX * 256
    kpos = np.arange(s)
    kfeat = np.zeros((s, ROW), np.float64)
    qfeat = np.zeros((s // t, len(slopes) * ROW, 2 * t), np.float64)
    qpos = (np.arange(s // t)[:, None] * t + np.arange(2 * t)[None, :] % t).astype(np.float64)
    for h, slope in enumerate(slopes):
        for i, piece in enumerate(_split_bf16(slope * LOG2E)):
            c = 5 * i
            kfeat[:, c], kfeat[:, c + 1], kfeat[:, c + 2:c + 5] = kpos // RADIX, kpos % RADIX, 1.0
            r = h * ROW + c
            qfeat[:, r], qfeat[:, r + 1] = float(piece) * RADIX, float(piece)
            for n, part in enumerate(_split_bf16(float(piece) * qpos)):
                qfeat[:, r + 2 + n] = -part
    return jnp.asarray(kfeat, BF16), jnp.asarray(qfeat, BF16)


def _diff_attention(qd_t, kd, vd_t, lam_params, g_sub, slopes, lam_init, logit_bound, hpb=DIFF_HEADS_PER_STEP):
    b, nt, n_qd, t = qd_t.shape
    s = nt * t
    nh = n_qd // DIFF_V_DIM
    kfeat, qfeat = _alibi_features(slopes, s, t)
    causal = np.arange(t)[:, None] <= np.arange(2 * t)[None, :] % t
    mask = jnp.asarray(np.where(causal, 0.0, -np.inf), F32)
    gsub = jnp.broadcast_to((g_sub * (1.0 - lam_init))[:, None], (DIFF_V_DIM, t)).astype(F32)
    bounded = jnp.asarray(logit_bound <= MAX_UNSHIFTED_LOGIT, jnp.int32)
    pairs = [(iq, j) for iq in range(nt) for j in range(iq)] + [(iq, iq) for iq in range(nt)]
    iq_tbl = jnp.asarray([pq[0] for pq in pairs], jnp.int32)
    j_tbl = jnp.asarray([pq[1] for pq in pairs], jnp.int32)
    kern = functools.partial(_diff_kernel, lam_init=lam_init)
    return pl.pallas_call(
        kern,
        grid=(nh // hpb, b),
        in_specs=[
            pl.BlockSpec(memory_space=pltpu.SMEM),
            pl.BlockSpec(memory_space=pltpu.SMEM),
            pl.BlockSpec(memory_space=pltpu.SMEM),
            pl.BlockSpec((1, nt, hpb * DIFF_V_DIM, t), lambda hg, bi: (bi, 0, hg, 0)),
            pl.BlockSpec((nt, hpb * DIFF_V_DIM, 2 * t), lambda hg, bi: (0, hg, 0)),
            pl.BlockSpec((1, nt, hpb * DIFF_V_DIM, t), lambda hg, bi: (bi, 0, hg, 0)),
            pl.BlockSpec((s, DIFF_V_DIM), lambda hg, bi: (0, 0)),
            pl.BlockSpec((1, nt, hpb * DIFF_V_DIM, t), lambda hg, bi: (bi, 0, hg, 0)),
            pl.BlockSpec((t, 2 * t), lambda hg, bi: (0, 0)),
            pl.BlockSpec((4, HEAD_DIM), lambda hg, bi: (0, 0)),
            pl.BlockSpec((DIFF_V_DIM, t), lambda hg, bi: (0, 0)),
        ],
        out_specs=pl.BlockSpec((1, nt, hpb * DIFF_V_DIM, t), lambda hg, bi: (bi, 0, hg, 0)),
        out_shape=jax.ShapeDtypeStruct((b, nt, n_qd, t), BF16),
        scratch_shapes=[pltpu.VMEM((hpb, nt, DIFF_V_DIM, 2 * t), F32),
                        pltpu.VMEM((hpb, nt, 8, 2 * t), F32),
                        pltpu.VMEM((2, hpb, t, 2 * t), BF16)],
        compiler_params=pltpu.CompilerParams(
            dimension_semantics=("parallel", "parallel"), vmem_limit_bytes=VMEM_LIMIT_BYTES),
        name="diff_attn",
    )(bounded.reshape(1), iq_tbl, j_tbl, qd_t, qfeat, kd, kfeat, vd_t, mask, lam_params, gsub)


def _outproj_kernel(x_ref, ya_ref, yd_ref, wa_ref, wd_ref, o_ref):
    yd_t = jnp.concatenate([yd_ref[k] for k in range(yd_ref.shape[0])], axis=1)
    o_ref[...] = (x_ref[...]
                  + jnp.dot(ya_ref[...], wa_ref[...], preferred_element_type=F32)
                  + lax.dot_general(yd_t, wd_ref[...], (((0,), (0,)), ((), ())), preferred_element_type=F32))


def _outproj(x2, ya2, yd_t, wo, tm=512):
    n, d = x2.shape
    tm = min(tm, n)
    n_a, n_d, t = ya2.shape[1], yd_t.shape[1], yd_t.shape[2]
    assert n % tm == 0 and tm % t == 0 and n_a == n_d and wo.shape == (n_a + n_d, d)
    return pl.pallas_call(
        _outproj_kernel,
        grid=(n // tm,),
        in_specs=[
            pl.BlockSpec((tm, d), lambda i: (i, 0)),
            pl.BlockSpec((tm, n_a), lambda i: (i, 0)),
            pl.BlockSpec((tm // t, n_d, t), lambda i: (i, 0, 0)),
            _resident((n_a, d), lambda i: (0, 0)),
            _resident((n_d, d), lambda i: (1, 0)),
        ],
        out_specs=pl.BlockSpec((tm, d), lambda i: (i, 0)),
        out_shape=jax.ShapeDtypeStruct((n, d), F32),
        compiler_params=pltpu.CompilerParams(
            dimension_semantics=("parallel",), vmem_limit_bytes=VMEM_LIMIT_BYTES),
        name="outproj",
    )(x2, ya2, yd_t, wo, wo)


def _ffn_kernel(h_ref, g_ref, wg_ref, wu_ref, wd_ref, o_ref, u_ref):
    f = pl.program_id(1)

    @pl.when(f == 0)
    def _():
        h = h_ref[...]
        u_ref[...] = _rms_rows(h, g_ref[...]).astype(BF16)
        o_ref[...] = h

    u = u_ref[...]
    gate = jnp.dot(u, wg_ref[...], preferred_element_type=F32)
    up = jnp.dot(u, wu_ref[...], preferred_element_type=F32)
    act = (gate * (1.0 / (1.0 + jnp.exp(-gate))) * up).astype(BF16)
    o_ref[...] += jnp.dot(act, wd_ref[...], preferred_element_type=F32)


def _ffn(h2, g_ffn, wg, wu, wd, tm=1024, tf=512):
    n, d = h2.shape
    tm = min(tm, n)
    assert n % tm == 0
    dff = wg.shape[1]
    return pl.pallas_call(
        _ffn_kernel,
        grid=(n // tm, dff // tf),
        in_specs=[
            pl.BlockSpec((tm, d), lambda i, f: (i, 0)),
            _resident((1, d), lambda i, f: (0, 0)),
            pl.BlockSpec((d, tf), lambda i, f: (0, f)),
            pl.BlockSpec((d, tf), lambda i, f: (0, f)),
            pl.BlockSpec((tf, d), lambda i, f: (f, 0)),
        ],
        out_specs=pl.BlockSpec((tm, d), lambda i, f: (i, 0)),
        out_shape=jax.ShapeDtypeStruct((n, d), F32),
        scratch_shapes=[pltpu.VMEM((tm, d), BF16)],
        compiler_params=pltpu.CompilerParams(
            dimension_semantics=("parallel", "arbitrary"), vmem_limit_bytes=VMEM_LIMIT_BYTES),
        name="ffn",
    )(h2, g_ffn[None, :], wg, wu, wd)


def _ple_kernel(h_ref, p_ref, g_ref, wg_ref, wp_ref, go_ref, o_ref, *, sub, ncol):
    d = h_ref.shape[1]
    for st in range(h_ref.shape[0] // sub):
        rows = slice(st * sub, (st + 1) * sub)
        u = _rms_rows(h_ref[rows, :], g_ref[...]).astype(BF16)
        pp = jnp.dot(p_ref[rows, :].astype(BF16), wp_ref[...], preferred_element_type=F32)
        ppn = _rms_rows(pp, go_ref[...])
        for c in range(d // ncol):
            cols = slice(c * ncol, (c + 1) * ncol)
            z = jnp.dot(u, wg_ref[:, cols], preferred_element_type=F32)
            gate = 1.0 / (1.0 + jnp.exp(-z))
            o_ref[rows, cols] = h_ref[rows, cols] + gate * ppn[:, cols]


def _ple(h2, p2, g_ple, wg, wp, g_out, tm=1024, sub=512, ncol=512):
    n, d = h2.shape
    tm = min(tm, n)
    assert n % tm == 0
    sub = min(sub, tm)
    assert tm % sub == 0
    const = lambda shape: _resident(shape, lambda i: (0, 0))
    return pl.pallas_call(
        functools.partial(_ple_kernel, sub=sub, ncol=min(ncol, d)),
        grid=(n // tm,),
        in_specs=[
            pl.BlockSpec((tm, d), lambda i: (i, 0)),
            pl.BlockSpec((tm, p2.shape[1]), lambda i: (i, 0)),
            const((1, d)), const(wg.shape), const(wp.shape), const((1, d)),
        ],
        out_specs=pl.BlockSpec((tm, d), lambda i: (i, 0)),
        out_shape=jax.ShapeDtypeStruct((n, d), F32),
        compiler_params=pltpu.CompilerParams(
            dimension_semantics=("parallel",), vmem_limit_bytes=VMEM_LIMIT_BYTES),
        name="ple",
    )(h2, p2, g_ple[None, :], wg, wp, g_out[None, :])


def _alibi_slopes(n):
    return [2.0 ** (-8.0 * (h + 1) / n) for h in range(n)]


def _logit_bound(q_gain, k_gain):
    return 1.02 * math.sqrt(HEAD_DIM) * jnp.max(jnp.abs(q_gain)) * jnp.max(jnp.abs(k_gain))


def kernel(x, p, g_attn, w_in, qn_swa, kn_swa, sinks, qn_diff, kn_diff, lambda_q1, lambda_k1, lambda_q2,
           lambda_k2, g_sub, w_out, g_ffn, w_gate, w_up, w_down, g_ple, w_ple_gate, w_ple_proj, g_ple_out):
    b, s, d = x.shape
    depth = p.shape[0]
    n_qa = d // 2
    n_ka = n_va = SWA_KV_HEADS * HEAD_DIM
    n_qd = n_kd = n_vd = d // 2
    diff_heads = n_vd // DIFF_V_DIM
    swa_heads = n_qa // HEAD_DIM
    assert swa_heads == SWA_KV_HEADS * SWA_GROUP and s % TOKEN_TILE == 0
    c = np.cumsum([0, n_qa, n_ka, n_va, n_qd, n_kd, n_vd])
    h = x
    for i in range(depth):
        lam_init = 0.8 - 0.6 * math.exp(-0.3 * i)
        wfm = _transpose_cast(w_in[i])
        fm_rows = tuple(int(r) for r in c[:6])
        qa_t, va_t, qd_t, vd_t, ka, kd = _inproj(
            h, g_attn[i], wfm, qn_swa[i], kn_swa[i], qn_diff[i], kn_diff[i],
            (n_qa, n_ka, n_va, n_qd, n_kd, n_vd), fm_rows)
        ya, (wg, wu, wd, wo, wpg) = _swa_attention(
            qa_t, ka, va_t, sinks[i], _alibi_slopes(swa_heads), _logit_bound(qn_swa[i], kn_swa[i]),
            f32_weights=(w_gate[i], w_up[i], w_down[i], w_out[i], w_ple_gate[i]))
        lam_params = jnp.stack([lambda_q1[i], lambda_k1[i], lambda_q2[i], lambda_k2[i]]).astype(F32)
        yd = _diff_attention(qd_t, kd, vd_t, lam_params, g_sub[i], _alibi_slopes(diff_heads), lam_init,
                             _logit_bound(qn_diff[i], kn_diff[i]))
        h2 = _outproj(h.reshape(b * s, d), ya.reshape(b * s, n_qa), yd.reshape(-1, n_vd, yd.shape[-1]), wo)
        h2 = _ffn(h2, g_ffn[i], wg, wu, wd)
        h2 = _ple(h2, p[i].reshape(b * s, -1), g_ple[i], wpg, w_ple_proj[i].astype(BF16), g_ple_out[i])
        h = h2.reshape(b, s, d)
    return h
```

```python
import functools
import math

import jax
import jax.numpy as jnp
import numpy as np
from jax import lax
from jax.experimental import pallas as pl
from jax.experimental.pallas import tpu as pltpu

F32 = jnp.float32
BF16 = jnp.bfloat16

HEAD_DIM = 64
SWA_BLOCK = 128
SWA_KV_HEADS = 2
SWA_GROUP = 8
DIFF_V_DIM = 2 * HEAD_DIM
EPS = 1e-6
NEG_INF = float("-inf")
LOG2E = math.log2(math.e)
MAX_UNSHIFTED_LOGIT = 60.0

TOKEN_TILE = 256
INPROJ_TILES_PER_STEP = 4
DIFF_HEADS_PER_STEP = 4
SWA_CAST_SLAB_BYTES = 12 * 1024 * 1024
VMEM_LIMIT_BYTES = 56 * 1024 * 1024

_NT = (((1,), (1,)), ((), ()))
_TN = (((0,), (0,)), ((), ()))


def _resident(shape, index_map):
    return pl.BlockSpec(shape, index_map, pipeline_mode=pl.Buffered(1))


def _rms_rows(x, gain):
    ms = jnp.mean(x * x, axis=-1, keepdims=True)
    return x * lax.rsqrt(ms + EPS) * gain


def _cast_columns_kernel(blk_ref, w_ref, o_ref, *, transpose):
    del blk_ref
    w = w_ref[...]
    o_ref[...] = (w.T if transpose else w).astype(BF16)


def _cast_columns(w, col_ranges, transpose, cb=128):
    k, _ = w.shape
    blocks = [c0 // cb + i for c0, c1 in col_ranges for i in range((c1 - c0) // cb)]
    assert all(c0 % cb == 0 and c1 % cb == 0 for c0, c1 in col_ranges)
    n = len(blocks) * cb
    return pl.pallas_call(
        functools.partial(_cast_columns_kernel, transpose=transpose),
        grid_spec=pltpu.PrefetchScalarGridSpec(
            num_scalar_prefetch=1,
            grid=(len(blocks),),
            in_specs=[pl.BlockSpec((k, cb), lambda i, blk: (0, blk[i]))],
            out_specs=pl.BlockSpec((cb, k), lambda i, blk: (i, 0)) if transpose
            else pl.BlockSpec((k, cb), lambda i, blk: (0, i)),
        ),
        out_shape=jax.ShapeDtypeStruct((n, k) if transpose else (k, n), BF16),
        compiler_params=pltpu.CompilerParams(dimension_semantics=("parallel",)),
        name="weight_layout_t" if transpose else "weight_layout",
    )(jnp.asarray(blocks, jnp.int32), w)


def _inproj_kernel(x_ref, g_ref, wfm_ref, gqa_ref, gqd_ref, gka_ref, gkd_ref,
                   qa_ref, va_ref, qd_ref, vd_ref, ka_ref, kd_ref, *, n_qa, n_va, n_qd, n_vd, n_ka, n_kd, fm_rows):
    tm = TOKEN_TILE
    per_tile = tm // SWA_BLOCK

    def headnorm_fm(z, gain, store):
        for h in range(z.shape[0] // HEAD_DIM):
            zh = z[h * HEAD_DIM:(h + 1) * HEAD_DIM, :]
            ms = jnp.mean(zh * zh, axis=0, keepdims=True)
            store(h, (zh * lax.rsqrt(ms + EPS) * gain).astype(BF16))

    for st in range(x_ref.shape[0] // tm):
        rows = slice(st * tm, (st + 1) * tm)
        u = _rms_rows(x_ref[rows, :], g_ref[...]).astype(BF16)

        def fm(row0, nrows):
            return lax.dot_general(wfm_ref[row0:row0 + nrows, :], u, _NT, preferred_element_type=F32)

        def store_qa(h, val):
            for t in range(per_tile):
                qa_ref[0, st * per_tile + t, h * HEAD_DIM:(h + 1) * HEAD_DIM, :] = (
                    val[:, t * SWA_BLOCK:(t + 1) * SWA_BLOCK])

        def store_qd(h, val):
            qd_ref[0, st, h * HEAD_DIM:(h + 1) * HEAD_DIM, :] = val

        def store_ka(h, val):
            for t in range(per_tile):
                ka_ref[0, st * per_tile + t, h * HEAD_DIM:(h + 1) * HEAD_DIM, :] = (
                    val[:, t * SWA_BLOCK:(t + 1) * SWA_BLOCK])

        def store_kd(h, val):
            kd_ref[0, st, h * HEAD_DIM:(h + 1) * HEAD_DIM, :] = val

        r_qa, r_ka, r_va, r_qd, r_kd, r_vd = fm_rows
        headnorm_fm(fm(r_qa, n_qa), gqa_ref[...], store_qa)
        headnorm_fm(fm(r_ka, n_ka), gka_ref[...], store_ka)
        zva = fm(r_va, n_va).astype(BF16)
        for t in range(per_tile):
            va_ref[0, st * per_tile + t] = zva[:, t * SWA_BLOCK:(t + 1) * SWA_BLOCK]
        headnorm_fm(fm(r_qd, n_qd), gqd_ref[...], store_qd)
        headnorm_fm(fm(r_kd, n_kd), gkd_ref[...], store_kd)
        vd_ref[0, st] = fm(r_vd, n_vd).astype(BF16)


def _inproj(x, g_attn, wfm, qn_swa, kn_swa, qn_diff, kn_diff, dims, fm_rows):
    b, s, d = x.shape
    n_qa, n_ka, n_va, n_qd, n_kd, n_vd = dims
    tm = TOKEN_TILE
    nt = s // tm
    tps = min(INPROJ_TILES_PER_STEP, nt)
    assert nt % tps == 0
    blk = tps * tm
    scale = LOG2E / math.sqrt(HEAD_DIM)
    gqa = jnp.broadcast_to((qn_swa * scale)[:, None], (HEAD_DIM, tm)).astype(F32)
    gqd = jnp.broadcast_to((qn_diff * scale)[:, None], (HEAD_DIM, tm)).astype(F32)
    gka = jnp.broadcast_to(kn_swa[:, None], (HEAD_DIM, tm)).astype(F32)
    gkd = jnp.broadcast_to(kn_diff[:, None], (HEAD_DIM, tm)).astype(F32)
    const = lambda shape: _resident(shape, lambda bi, ti: (0,) * len(shape))
    kern = functools.partial(_inproj_kernel, n_qa=n_qa, n_va=n_va, n_qd=n_qd, n_vd=n_vd, n_ka=n_ka, n_kd=n_kd,
                             fm_rows=fm_rows)
    return pl.pallas_call(
        kern,
        grid=(b, nt // tps),
        in_specs=[
            pl.BlockSpec((None, blk, d), lambda bi, ti: (bi, ti, 0)),
            const((1, d)), const(wfm.shape),
            const((HEAD_DIM, tm)), const((HEAD_DIM, tm)), const((HEAD_DIM, tm)), const((HEAD_DIM, tm)),
        ],
        out_specs=[
            pl.BlockSpec((1, blk // SWA_BLOCK, n_qa, SWA_BLOCK), lambda bi, ti: (bi, ti, 0, 0)),
            pl.BlockSpec((1, blk // SWA_BLOCK, n_va, SWA_BLOCK), lambda bi, ti: (bi, ti, 0, 0)),
            pl.BlockSpec((1, tps, n_qd, tm), lambda bi, ti: (bi, ti, 0, 0)),
            pl.BlockSpec((1, tps, n_vd, tm), lambda bi, ti: (bi, ti, 0, 0)),
            pl.BlockSpec((1, blk // SWA_BLOCK, n_ka, SWA_BLOCK), lambda bi, ti: (bi, ti, 0, 0)),
            pl.BlockSpec((1, tps, n_kd, tm), lambda bi, ti: (bi, ti, 0, 0)),
        ],
        out_shape=[
            jax.ShapeDtypeStruct((b, s // SWA_BLOCK, n_qa, SWA_BLOCK), BF16),
            jax.ShapeDtypeStruct((b, s // SWA_BLOCK, n_va, SWA_BLOCK), BF16),
            jax.ShapeDtypeStruct((b, nt, n_qd, tm), BF16),
            jax.ShapeDtypeStruct((b, nt, n_vd, tm), BF16),
            jax.ShapeDtypeStruct((b, s // SWA_BLOCK, n_ka, SWA_BLOCK), BF16),
            jax.ShapeDtypeStruct((b, nt, n_kd, tm), BF16),
        ],
        compiler_params=pltpu.CompilerParams(
            dimension_semantics=("parallel", "parallel"), vmem_limit_bytes=VMEM_LIMIT_BYTES),
        name="inproj",
    )(x, g_attn[None, :], wfm, gqa, gqd, gka, gkd)


def _staged_pipeline(n_steps, stages):
    depth = len(stages)

    def iteration(i, parity):
        for k in reversed(range(depth)):
            t = i - k
            if isinstance(i, int) and not 0 <= t < n_steps:
                continue
            stages[k](t, (parity - k) % 2)

    fill_end = min(depth - 1, n_steps)
    for i in range(fill_end):
        iteration(i, i % 2)
    n_pairs = (n_steps - fill_end) // 2

    def pair(m, carry):
        i = fill_end + 2 * m
        iteration(i, fill_end % 2)
        iteration(i + 1, (fill_end + 1) % 2)
        return carry

    lax.fori_loop(0, n_pairs, pair, 0)
    for i in range(fill_end + 2 * n_pairs, n_steps + depth - 1):
        iteration(i, i % 2)


def _swa_kernel(bounded_ref, q_ref, k_ref, v_ref, bias_ref, sink_ref, *refs, n_cast):
    cast_in, o_ref, cast_out = refs[:n_cast], refs[n_cast], refs[n_cast + 1:2 * n_cast + 1]
    s_ref, p_ref, d_ref = refs[2 * n_cast + 1:]
    for src, dst in zip(cast_in, cast_out):
        dst[...] = src[...].astype(BF16)

    hk = pl.program_id(0)
    nblk = q_ref.shape[1]
    w = SWA_BLOCK
    sink = sink_ref[0]

    def window_start(n):
        return jnp.maximum(n - 1, 0)

    def scores(n, slot):
        qblk = q_ref[0, n]
        qg = jnp.concatenate([qblk[g * HEAD_DIM:(g + 1) * HEAD_DIM, :] for g in range(SWA_GROUP)], axis=1)
        zero = jnp.zeros_like(qg)
        qpad = jnp.concatenate([jnp.where(hk == 0, qg, zero), jnp.where(hk == 1, qg, zero)], axis=0)
        first = window_start(n)
        kwin_t = jnp.concatenate([k_ref[0, first], k_ref[0, first + 1]], axis=1)
        s_ref[slot] = lax.dot_general(kwin_t, qpad, _TN, preferred_element_type=F32)

    def probs(n, slot, *, bounded):
        t = s_ref[slot] + bias_ref[jnp.minimum(n, 1)]
        if bounded:
            e = jnp.exp2(t)
            d_ref[slot] = jnp.sum(e, axis=0, keepdims=True) + jnp.exp2(sink)
        else:
            m = jnp.maximum(jnp.max(t, axis=0, keepdims=True), sink)
            e = jnp.exp2(t - m)
            d_ref[slot] = jnp.sum(e, axis=0, keepdims=True) + jnp.exp2(sink - m)
        p_ref[slot] = e.astype(BF16)

    def weighted_values(n, slot):
        first = window_start(n)
        vwin = jnp.concatenate([v_ref[0, first], v_ref[0, first + 1]], axis=1)
        o = jnp.dot(vwin, p_ref[slot], preferred_element_type=F32) * (1.0 / d_ref[slot])
        for gp in range(SWA_GROUP // 2):
            pair = jnp.concatenate([o[:, (2 * gp) * w:(2 * gp + 1) * w],
                                    o[:, (2 * gp + 1) * w:(2 * gp + 2) * w]], axis=0)
            o_ref[0, pl.ds(pl.multiple_of(n * w, w), w), gp * 2 * HEAD_DIM:(gp + 1) * 2 * HEAD_DIM] = (
                pair.T.astype(BF16))

    @pl.when(bounded_ref[0] != 0)
    def _():
        _staged_pipeline(nblk, [scores, functools.partial(probs, bounded=True), weighted_values])

    @pl.when(bounded_ref[0] == 0)
    def _():
        def block(n, carry):
            scores(n, 0)
            probs(n, 0, bounded=False)
            weighted_values(n, 0)
            return carry

        lax.fori_loop(0, nblk, block, 0)


def _swa_bias(slopes):
    w = SWA_BLOCK
    kj = np.arange(2 * w)[:, None]
    qi = np.arange(w)[None, :]
    out = np.empty((2, SWA_KV_HEADS, 2 * w, SWA_GROUP * w), np.float32)
    for first, dist in ((0, qi - kj), (1, qi - kj + w)):
        valid = (dist >= 0) & (dist < w)
        for hk in range(SWA_KV_HEADS):
            for g in range(SWA_GROUP):
                sl = slopes[hk * SWA_GROUP + g]
                out[first, hk, :, g * w:(g + 1) * w] = np.where(valid, -sl * dist, -np.inf)
    return out


def _swa_attention(qa_t, ka, va_t, sinks, slopes, logit_bound, f32_weights=()):
    b, nblk, n_qa, w = qa_t.shape
    s = nblk * w
    gw = SWA_GROUP * w
    bias = jnp.asarray(_swa_bias(slopes) * LOG2E)
    sinks = sinks.astype(F32)
    sink_cols = jnp.repeat((sinks * LOG2E).reshape(SWA_KV_HEADS, 1, SWA_GROUP), w, axis=-1)
    bounded = jnp.asarray(jnp.maximum(logit_bound, jnp.max(jnp.abs(sinks))) <= MAX_UNSHIFTED_LOGIT, jnp.int32)
    n_steps = SWA_KV_HEADS * b
    slab_bytes = sum(wt.size * 4 // n_steps for wt in f32_weights)
    if (any(wt.shape[0] % (16 * n_steps) for wt in f32_weights)
            or slab_bytes > SWA_CAST_SLAB_BYTES):
        ya, _ = _swa_attention(qa_t, ka, va_t, sinks, slopes, logit_bound)
        return ya, tuple(wt.astype(BF16) for wt in f32_weights)
    slab_specs = [pl.BlockSpec((wt.shape[0] // n_steps, wt.shape[1]), lambda hk, bi: (hk * b + bi, 0))
                  for wt in f32_weights]
    outs = pl.pallas_call(
        functools.partial(_swa_kernel, n_cast=len(f32_weights)),
        grid=(SWA_KV_HEADS, b),
        in_specs=[
            pl.BlockSpec(memory_space=pltpu.SMEM),
            pl.BlockSpec((1, nblk, SWA_GROUP * HEAD_DIM, w), lambda hk, bi: (bi, 0, hk, 0)),
            pl.BlockSpec((1, nblk, SWA_KV_HEADS * HEAD_DIM, w), lambda hk, bi: (bi, 0, 0, 0)),
            pl.BlockSpec((1, nblk, HEAD_DIM, w), lambda hk, bi: (bi, 0, hk, 0)),
            pl.BlockSpec((2, None, 2 * w, gw), lambda hk, bi: (0, hk, 0, 0)),
            pl.BlockSpec((1, 1, gw), lambda hk, bi: (hk, 0, 0)),
        ] + slab_specs,
        out_specs=[pl.BlockSpec((1, s, SWA_GROUP * HEAD_DIM), lambda hk, bi: (bi, 0, hk))] + slab_specs,
        out_shape=[jax.ShapeDtypeStruct((b, s, n_qa), BF16)]
                  + [jax.ShapeDtypeStruct(wt.shape, BF16) for wt in f32_weights],
        scratch_shapes=[pltpu.VMEM((2, 2 * w, gw), F32), pltpu.VMEM((2, 2 * w, gw), BF16),
                        pltpu.VMEM((2, 1, gw), F32)],
        compiler_params=pltpu.CompilerParams(
            dimension_semantics=("parallel", "parallel"), vmem_limit_bytes=VMEM_LIMIT_BYTES),
        name="swa_attn",
    )(bounded.reshape(1), qa_t, ka, va_t, bias, sink_cols, *f32_weights)
    return outs[0], outs[1:]


def _diff_kernel(bounded_ref, iq_tbl_ref, j_tbl_ref, q_ref, qpos_ref, k_ref, kpos_ref, v_ref, mask_ref, lam_ref,
                 gsub_ref, o_ref, acc_ref, l_ref, p_ref, *, lam_init):
    nt, t_q = q_ref.shape[1], q_ref.shape[3]
    hpb = acc_ref.shape[0]
    n_plain = nt * (nt - 1) // 2
    dv = DIFF_V_DIM
    lp = lam_ref[...]
    lam = (jnp.exp(jnp.sum(lp[0:1] * lp[1:2], axis=-1, keepdims=True))
           - jnp.exp(jnp.sum(lp[2:3] * lp[3:4], axis=-1, keepdims=True)) + lam_init)
    zero = jnp.zeros((HEAD_DIM, t_q), BF16)

    def padded_q(iq, hh):
        q = q_ref[0, iq, hh * dv:(hh + 1) * dv, :]
        return jnp.concatenate([jnp.concatenate([q[:HEAD_DIM], zero], axis=0),
                                jnp.concatenate([zero, q[HEAD_DIM:]], axis=0)], axis=1)

    def logits(iq, j, hh, masked):
        rows = pl.ds(pl.multiple_of(j * t_q, t_q), t_q)
        lhs = jnp.concatenate([k_ref[0, j, hh * dv:(hh + 1) * dv, :].T, kpos_ref[rows, :]], axis=1)
        rhs = jnp.concatenate([padded_q(iq, hh), qpos_ref[iq, hh * dv:(hh + 1) * dv, :]], axis=0)
        s = jnp.dot(lhs, rhs, preferred_element_type=F32)
        return s + mask_ref[...] if masked else s

    def finalize(iq, hh, acc, l):
        acc = acc * (1.0 / l)
        od = acc[:, :t_q] - lam * acc[:, t_q:]
        ms = jnp.mean(od * od, axis=0, keepdims=True)
        y = od * lax.rsqrt(ms + EPS) * gsub_ref[...]
        o_ref[0, iq, hh * dv:(hh + 1) * dv, :] = y.astype(BF16)

    def probs(t, slot, *, base, diagonal):
        iq, j = iq_tbl_ref[base + t], j_tbl_ref[base + t]
        for hh in range(hpb):
            p = jnp.exp2(logits(iq, j, hh, diagonal))
            psum = jnp.sum(p.reshape(t_q // 8, 8, 2 * t_q), axis=0)
            l_ref[hh, iq] = psum if diagonal else l_ref[hh, iq] + psum
            p_ref[slot, hh] = p.astype(BF16)

    def pv(t, slot, *, base, diagonal):
        iq, j = iq_tbl_ref[base + t], j_tbl_ref[base + t]
        for hh in range(hpb):
            o = jnp.dot(v_ref[0, j, hh * dv:(hh + 1) * dv, :], p_ref[slot, hh], preferred_element_type=F32)
            acc_ref[hh, iq] = o if diagonal else acc_ref[hh, iq] + o

    @pl.when(bounded_ref[0] != 0)
    def _():
        _staged_pipeline(nt, [functools.partial(probs, base=n_plain, diagonal=True),
                              functools.partial(pv, base=n_plain, diagonal=True)])
        _staged_pipeline(n_plain, [functools.partial(probs, base=0, diagonal=False),
                                   functools.partial(pv, base=0, diagonal=False)])

        def fin(iq, carry):
            for hh in range(hpb):
                finalize(iq, hh, acc_ref[hh, iq], jnp.sum(l_ref[hh, iq], axis=0, keepdims=True))
            return carry

        lax.fori_loop(0, nt, fin, 0)

    @pl.when(bounded_ref[0] == 0)
    def _():
        def q_block(iq, carry):
            acc_ref[:, 0] = jnp.zeros((hpb, dv, 2 * t_q), F32)

            def step(j, stats, masked):
                out = []
                for hh in range(hpb):
                    m, l = stats[2 * hh], stats[2 * hh + 1]
                    s = logits(iq, j, hh, masked)
                    mnew = jnp.maximum(m, jnp.max(s, axis=0, keepdims=True))
                    alpha = jnp.exp2(m - mnew)
                    p = jnp.exp2(s - mnew)
                    out += [mnew, alpha * l + jnp.sum(p, axis=0, keepdims=True)]
                    acc_ref[hh, 0] = alpha * acc_ref[hh, 0] + jnp.dot(
                        v_ref[0, j, hh * dv:(hh + 1) * dv, :], p.astype(BF16), preferred_element_type=F32)
                return tuple(out)

            init = (jnp.full((1, 2 * t_q), NEG_INF, F32), jnp.zeros((1, 2 * t_q), F32)) * hpb
            stats = lax.fori_loop(0, iq, lambda j, c: step(j, c, False), init)
            stats = step(iq, stats, True)
            for hh in range(hpb):
                finalize(iq, hh, acc_ref[hh, 0], stats[2 * hh + 1])
            return carry

        lax.fori_loop(0, nt, q_block, 0)


def _split_bf16(x, parts=3):
    out, rest = [], np.asarray(x, np.float64)
    for _ in range(parts):
        piece = rest.astype(BF16).astype(np.float64)
        out.append(piece)
        rest = rest - piece
    return out


def _alibi_features(slopes, s, t):
    ROW = DIFF_V_DIM
    RADIX = 128
    assert s <= RADIX * 256
    kpos = np.arange(s)
    kfeat = np.zeros((s, ROW), np.float64)
    qfeat = np.zeros((s // t, len(slopes) * ROW, 2 * t), np.float64)
    qpos = (np.arange(s // t)[:, None] * t + np.arange(2 * t)[None, :] % t).astype(np.float64)
    for h, slope in enumerate(slopes):
        for i, piece in enumerate(_split_bf16(slope * LOG2E)):
            c = 5 * i
            kfeat[:, c], kfeat[:, c + 1], kfeat[:, c + 2:c + 5] = kpos // RADIX, kpos % RADIX, 1.0
            r = h * ROW + c
            qfeat[:, r], qfeat[:, r + 1] = float(piece) * RADIX, float(piece)
            for n, part in enumerate(_split_bf16(float(piece) * qpos)):
                qfeat[:, r + 2 + n] = -part
    return jnp.asarray(kfeat, BF16), jnp.asarray(qfeat, BF16)


def _diff_attention(qd_t, kd, vd_t, lam_params, g_sub, slopes, lam_init, logit_bound, hpb=DIFF_HEADS_PER_STEP):
    b, nt, n_qd, t = qd_t.shape
    s = nt * t
    nh = n_qd // DIFF_V_DIM
    kfeat, qfeat = _alibi_features(slopes, s, t)
    causal = np.arange(t)[:, None] <= np.arange(2 * t)[None, :] % t
    mask = jnp.asarray(np.where(causal, 0.0, -np.inf), F32)
    gsub = jnp.broadcast_to((g_sub * (1.0 - lam_init))[:, None], (DIFF_V_DIM, t)).astype(F32)
    bounded = jnp.asarray(logit_bound <= MAX_UNSHIFTED_LOGIT, jnp.int32)
    pairs = [(iq, j) for iq in range(nt) for j in range(iq)] + [(iq, iq) for iq in range(nt)]
    iq_tbl = jnp.asarray([pq[0] for pq in pairs], jnp.int32)
    j_tbl = jnp.asarray([pq[1] for pq in pairs], jnp.int32)
    kern = functools.partial(_diff_kernel, lam_init=lam_init)
    return pl.pallas_call(
        kern,
        grid=(nh // hpb, b),
        in_specs=[
            pl.BlockSpec(memory_space=pltpu.SMEM),
            pl.BlockSpec(memory_space=pltpu.SMEM),
            pl.BlockSpec(memory_space=pltpu.SMEM),
            pl.BlockSpec((1, nt, hpb * DIFF_V_DIM, t), lambda hg, bi: (bi, 0, hg, 0)),
            pl.BlockSpec((nt, hpb * DIFF_V_DIM, 2 * t), lambda hg, bi: (0, hg, 0)),
            pl.BlockSpec((1, nt, hpb * DIFF_V_DIM, t), lambda hg, bi: (bi, 0, hg, 0)),
            pl.BlockSpec((s, DIFF_V_DIM), lambda hg, bi: (0, 0)),
            pl.BlockSpec((1, nt, hpb * DIFF_V_DIM, t), lambda hg, bi: (bi, 0, hg, 0)),
            pl.BlockSpec((t, 2 * t), lambda hg, bi: (0, 0)),
            pl.BlockSpec((4, HEAD_DIM), lambda hg, bi: (0, 0)),
            pl.BlockSpec((DIFF_V_DIM, t), lambda hg, bi: (0, 0)),
        ],
        out_specs=pl.BlockSpec((1, nt, hpb * DIFF_V_DIM, t), lambda hg, bi: (bi, 0, hg, 0)),
        out_shape=jax.ShapeDtypeStruct((b, nt, n_qd, t), BF16),
        scratch_shapes=[pltpu.VMEM((hpb, nt, DIFF_V_DIM, 2 * t), F32),
                        pltpu.VMEM((hpb, nt, 8, 2 * t), F32),
                        pltpu.VMEM((2, hpb, t, 2 * t), BF16)],
        compiler_params=pltpu.CompilerParams(
            dimension_semantics=("parallel", "parallel"), vmem_limit_bytes=VMEM_LIMIT_BYTES),
        name="diff_attn",
    )(bounded.reshape(1), iq_tbl, j_tbl, qd_t, qfeat, kd, kfeat, vd_t, mask, lam_params, gsub)


def _outproj_kernel(x_ref, ya_ref, yd_ref, wa_ref, wd_ref, o_ref):
    yd_t = jnp.concatenate([yd_ref[k] for k in range(yd_ref.shape[0])], axis=1)
    o_ref[...] = (x_ref[...]
                  + jnp.dot(ya_ref[...], wa_ref[...], preferred_element_type=F32)
                  + lax.dot_general(yd_t, wd_ref[...], (((0,), (0,)), ((), ())), preferred_element_type=F32))


def _outproj(x2, ya2, yd_t, wo, tm=512):
    n, d = x2.shape
    tm = min(tm, n)
    n_a, n_d, t = ya2.shape[1], yd_t.shape[1], yd_t.shape[2]
    assert n % tm == 0 and tm % t == 0 and n_a == n_d and wo.shape == (n_a + n_d, d)
    return pl.pallas_call(
        _outproj_kernel,
        grid=(n // tm,),
        in_specs=[
            pl.BlockSpec((tm, d), lambda i: (i, 0)),
            pl.BlockSpec((tm, n_a), lambda i: (i, 0)),
            pl.BlockSpec((tm // t, n_d, t), lambda i: (i, 0, 0)),
            _resident((n_a, d), lambda i: (0, 0)),
            _resident((n_d, d), lambda i: (1, 0)),
        ],
        out_specs=pl.BlockSpec((tm, d), lambda i: (i, 0)),
        out_shape=jax.ShapeDtypeStruct((n, d), F32),
        compiler_params=pltpu.CompilerParams(
            dimension_semantics=("parallel",), vmem_limit_bytes=VMEM_LIMIT_BYTES),
        name="outproj",
    )(x2, ya2, yd_t, wo, wo)


def _ffn_kernel(h_ref, g_ref, wg_ref, wu_ref, wd_ref, o_ref, u_ref):
    f = pl.program_id(1)

    @pl.when(f == 0)
    def _():
        h = h_ref[...]
        u_ref[...] = _rms_rows(h, g_ref[...]).astype(BF16)
        o_ref[...] = h

    u = u_ref[...]
    gate = jnp.dot(u, wg_ref[...], preferred_element_type=F32)
    up = jnp.dot(u, wu_ref[...], preferred_element_type=F32)
    act = (gate * (1.0 / (1.0 + jnp.exp(-gate))) * up).astype(BF16)
    o_ref[...] += jnp.dot(act, wd_ref[...], preferred_element_type=F32)


def _ffn(h2, g_ffn, wg, wu, wd, tm=1024, tf=512):
    n, d = h2.shape
    tm = min(tm, n)
    assert n % tm == 0
    dff = wg.shape[1]
    return pl.pallas_call(
        _ffn_kernel,
        grid=(n // tm, dff // tf),
        in_specs=[
            pl.BlockSpec((tm, d), lambda i, f: (i, 0)),
            _resident((1, d), lambda i, f: (0, 0)),
            pl.BlockSpec((d, tf), lambda i, f: (0, f)),
            pl.BlockSpec((d, tf), lambda i, f: (0, f)),
            pl.BlockSpec((tf, d), lambda i, f: (f, 0)),
        ],
        out_specs=pl.BlockSpec((tm, d), lambda i, f: (i, 0)),
        out_shape=jax.ShapeDtypeStruct((n, d), F32),
        scratch_shapes=[pltpu.VMEM((tm, d), BF16)],
        compiler_params=pltpu.CompilerParams(
            dimension_semantics=("parallel", "arbitrary"), vmem_limit_bytes=VMEM_LIMIT_BYTES),
        name="ffn",
    )(h2, g_ffn[None, :], wg, wu, wd)


def _ple_kernel(h_ref, p_ref, g_ref, wg_ref, wp_ref, go_ref, o_ref, *, sub, ncol):
    d = h_ref.shape[1]
    for st in range(h_ref.shape[0] // sub):
        rows = slice(st * sub, (st + 1) * sub)
        u = _rms_rows(h_ref[rows, :], g_ref[...]).astype(BF16)
        pp = jnp.dot(p_ref[rows, :].astype(BF16), wp_ref[...], preferred_element_type=F32)
        ppn = _rms_rows(pp, go_ref[...])
        for c in range(d // ncol):
            cols = slice(c * ncol, (c + 1) * ncol)
            z = jnp.dot(u, wg_ref[:, cols], preferred_element_type=F32)
            gate = 1.0 / (1.0 + jnp.exp(-z))
            o_ref[rows, cols] = h_ref[rows, cols] + gate * ppn[:, cols]


def _ple(h2, p2, g_ple, wg, wp, g_out, tm=1024, sub=512, ncol=512):
    n, d = h2.shape
    tm = min(tm, n)
    assert n % tm == 0
    sub = min(sub, tm)
    assert tm % sub == 0
    const = lambda shape: _resident(shape, lambda i: (0, 0))
    return pl.pallas_call(
        functools.partial(_ple_kernel, sub=sub, ncol=min(ncol, d)),
        grid=(n // tm,),
        in_specs=[
            pl.BlockSpec((tm, d), lambda i: (i, 0)),
            pl.BlockSpec((tm, p2.shape[1]), lambda i: (i, 0)),
            const((1, d)), const(wg.shape), const(wp.shape), const((1, d)),
        ],
        out_specs=pl.BlockSpec((tm, d), lambda i: (i, 0)),
        out_shape=jax.ShapeDtypeStruct((n, d), F32),
        compiler_params=pltpu.CompilerParams(
            dimension_semantics=("parallel",), vmem_limit_bytes=VMEM_LIMIT_BYTES),
        name="ple",
    )(h2, p2, g_ple[None, :], wg, wp, g_out[None, :])


def _alibi_slopes(n):
    return [2.0 ** (-8.0 * (h + 1) / n) for h in range(n)]


def _logit_bound(q_gain, k_gain):
    return 1.02 * math.sqrt(HEAD_DIM) * jnp.max(jnp.abs(q_gain)) * jnp.max(jnp.abs(k_gain))


def kernel(x, p, g_attn, w_in, qn_swa, kn_swa, sinks, qn_diff, kn_diff, lambda_q1, lambda_k1, lambda_q2,
           lambda_k2, g_sub, w_out, g_ffn, w_gate, w_up, w_down, g_ple, w_ple_gate, w_ple_proj, g_ple_out):
    b, s, d = x.shape
    depth = p.shape[0]
    n_qa = d // 2
    n_ka = n_va = SWA_KV_HEADS * HEAD_DIM
    n_qd = n_kd = n_vd = d // 2
    diff_heads = n_vd // DIFF_V_DIM
    swa_heads = n_qa // HEAD_DIM
    assert swa_heads == SWA_KV_HEADS * SWA_GROUP and s % TOKEN_TILE == 0
    c = np.cumsum([0, n_qa, n_ka, n_va, n_qd, n_kd, n_vd])
    h = x
    for i in range(depth):
        lam_init = 0.8 - 0.6 * math.exp(-0.3 * i)
        wfm = _cast_columns(w_in[i], [(int(c[0]), int(c[6]))], transpose=True, cb=2 * SWA_BLOCK)
        fm_rows = tuple(int(r) for r in c[:6])
        qa_t, va_t, qd_t, vd_t, ka, kd = _inproj(
            h, g_attn[i], wfm, qn_swa[i], kn_swa[i], qn_diff[i], kn_diff[i],
            (n_qa, n_ka, n_va, n_qd, n_kd, n_vd), fm_rows)
        ya, (wg, wu, wd, wo, wpg) = _swa_attention(
            qa_t, ka, va_t, sinks[i], _alibi_slopes(swa_heads), _logit_bound(qn_swa[i], kn_swa[i]),
            f32_weights=(w_gate[i], w_up[i], w_down[i], w_out[i], w_ple_gate[i]))
        lam_params = jnp.stack([lambda_q1[i], lambda_k1[i], lambda_q2[i], lambda_k2[i]]).astype(F32)
        yd = _diff_attention(qd_t, kd, vd_t, lam_params, g_sub[i], _alibi_slopes(diff_heads), lam_init,
                             _logit_bound(qn_diff[i], kn_diff[i]))
        h2 = _outproj(h.reshape(b * s, d), ya.reshape(b * s, n_qa), yd.reshape(-1, n_vd, yd.shape[-1]), wo)
        h2 = _ffn(h2, g_ffn[i], wg, wu, wd)
        h2 = _ple(h2, p[i].reshape(b * s, -1), g_ple[i], wpg, w_ple_proj[i].astype(BF16), g_ple_out[i])
        h = h2.reshape(b, s, d)
    return h
```
